```python
import numpy as np
import jax, jax.numpy as jnp
from jax import lax

D_MODEL = 1024
BATCH = 8
SEQ = 4096
DEPTH = 1

HEAD_DIM = 64
N_FOX_HEADS = 8
N_NSA_HEADS = 8
N_NSA_KV = 2
NSA_GROUP = N_NSA_HEADS // N_NSA_KV
FOX_WIDTH = N_FOX_HEADS * HEAD_DIM
NSA_WIDTH = N_NSA_HEADS * HEAD_DIM
MIX_WIDTH = FOX_WIDTH + NSA_WIDTH
KV_WIDTH = N_NSA_KV * HEAD_DIM
CMP_LEN = 32
CMP_STRIDE = 16
CMP_HIDDEN = 256
SEL_LEN = 64
SEL_TOPK = 16
WINDOW = 512
Q_BLOCK = 128
SEL_Q_BLOCK = 64
D_FF = 4 * D_MODEL
ROPE_THETA = 10000.0
EPS = 1e-6
NEG = -1e30
FORCE_BONUS = 1e4
IN_SIZES = [FOX_WIDTH, FOX_WIDTH, FOX_WIDTH, N_FOX_HEADS,
            NSA_WIDTH,
            KV_WIDTH, KV_WIDTH,
            KV_WIDTH, KV_WIDTH,
            KV_WIDTH, KV_WIDTH,
            3 * N_NSA_HEADS]
N_IN = sum(IN_SIZES)

kernel_name = "hymba_fox_nsa_hybrid_layer"


def rmsnorm(x, g):
    xf = x.astype(jnp.float32)
    y = xf * lax.rsqrt(jnp.mean(xf * xf, axis=-1, keepdims=True) + EPS)
    return (y * g.astype(jnp.float32)).astype(x.dtype)


def rope(x, pos):
    half = HEAD_DIM // 2
    inv = ROPE_THETA ** (-jnp.arange(half, dtype=jnp.float32) / half)
    ang = pos.astype(jnp.float32)[:, None] * inv[None, :]
    cos = jnp.cos(ang)[:, None, :]
    sin = jnp.sin(ang)[:, None, :]
    xf = x.astype(jnp.float32)
    x1, x2 = xf[..., :half], xf[..., half:]
    out = jnp.concatenate([x1 * cos - x2 * sin, x1 * sin + x2 * cos], axis=-1)
    return out.astype(x.dtype)


def forgetting_attention(q, k, v, f_logit):
    B, S, H, dh = q.shape
    c = jnp.cumsum(jax.nn.log_sigmoid(f_logit.astype(jnp.float32)), axis=1)
    nq = S // Q_BLOCK
    qb = q.reshape(B, nq, Q_BLOCK, H, dh).transpose(1, 0, 2, 3, 4)
    cb = c.reshape(B, nq, Q_BLOCK, H).transpose(1, 0, 2, 3)
    starts = jnp.arange(nq) * Q_BLOCK
    kpos = jnp.arange(S)
    c_k = c.transpose(0, 2, 1)
    scale = HEAD_DIM ** -0.5

    def block(args):
        q_blk, c_blk, start = args
        qpos = start + jnp.arange(Q_BLOCK)
        s = jnp.einsum('bqhd,bkhd->bhqk', q_blk, k).astype(jnp.float32) * scale
        s = s + c_blk.transpose(0, 2, 1)[..., None] - c_k[:, :, None, :]
        s = jnp.where(kpos[None, :] <= qpos[:, None], s, NEG)
        p = jax.nn.softmax(s, axis=-1)
        return jnp.einsum('bhqk,bkhd->bqhd', p.astype(v.dtype), v)

    o = lax.map(block, (qb, cb, starts))
    return o.transpose(1, 0, 2, 3, 4).reshape(B, S, H, dh)


def compress(x_raw, pe, w1, b1, w2, b2):
    B, S, G, dh = x_raw.shape
    nc = (S - CMP_LEN) // CMP_STRIDE + 1
    idx = jnp.arange(nc)[:, None] * CMP_STRIDE + jnp.arange(CMP_LEN)[None, :]
    blk = x_raw[:, idx] + pe[None, None, :, None, :]
    blk = blk.transpose(0, 1, 3, 2, 4).reshape(B, nc, G, CMP_LEN * dh)
    h = jax.nn.gelu(blk @ w1 + b1)
    return h @ w2 + b2


def nsa_compressed_selected(q, k_cmp, v_cmp, k_slc, v_slc):
    B, S, H, dh = q.shape
    G = N_NSA_KV
    nc = k_cmp.shape[1]
    nb = S // SEL_LEN
    topk = min(SEL_TOPK, nb)
    scale = HEAD_DIM ** -0.5
    cmp_start = jnp.arange(nc) * CMP_STRIDE
    cmp_end = cmp_start + CMP_LEN - 1
    sel_start = jnp.arange(nb) * SEL_LEN
    overlap = ((cmp_start[:, None] < sel_start[None, :] + SEL_LEN)
               & (cmp_start[:, None] + CMP_LEN > sel_start[None, :])).astype(jnp.float32)
    k_t = k_slc.transpose(0, 2, 1, 3)
    v_t = v_slc.transpose(0, 2, 1, 3)
    b_ix = jnp.arange(B)[:, None, None]
    g_ix = jnp.arange(G)[None, :, None]
    jb = jnp.arange(nb)
    nq = S // SEL_Q_BLOCK
    qb = q.reshape(B, nq, SEL_Q_BLOCK, G, NSA_GROUP, dh).transpose(1, 0, 2, 3, 4, 5)
    starts = jnp.arange(nq) * SEL_Q_BLOCK

    def block(args):
        q_blk, start = args
        qpos = start + jnp.arange(SEL_Q_BLOCK)
        s = jnp.einsum('bqgnd,bcgd->bgnqc', q_blk, k_cmp).astype(jnp.float32) * scale
        vis = cmp_end[None, :] <= qpos[:, None]
        p_cmp = jnp.where(vis, jax.nn.softmax(jnp.where(vis, s, NEG), axis=-1), 0.0)
        o_cmp = jnp.einsum('bgnqc,bcgd->bqgnd', p_cmp.astype(v_cmp.dtype), v_cmp)
        imp = jnp.einsum('bgnqc,cj->bgqj', p_cmp, overlap)
        cur = qpos // SEL_LEN
        valid = jb[None, :] <= cur[:, None]
        forced = (jb[None, :] == 0) | (jb[None, :] == cur[:, None]) | (jb[None, :] == cur[:, None] - 1)
        score = jnp.where(valid, imp + jnp.where(forced, FORCE_BONUS, 0.0), -1.0)
        _, sel = lax.top_k(score, topk)
        kpos = (sel[..., None] * SEL_LEN + jnp.arange(SEL_LEN)).reshape(B, G, SEL_Q_BLOCK * topk * SEL_LEN)
        ks = k_t[b_ix, g_ix, kpos].reshape(B, G, SEL_Q_BLOCK, topk * SEL_LEN, dh)
        vs = v_t[b_ix, g_ix, kpos].reshape(B, G, SEL_Q_BLOCK, topk * SEL_LEN, dh)
        kpos = kpos.reshape(B, G, SEL_Q_BLOCK, topk * SEL_LEN)
        s2 = jnp.einsum('bqgnd,bgqkd->bgnqk', q_blk, ks).astype(jnp.float32) * scale
        mask = (kpos <= qpos[None, None, :, None])[:, :, None]
        p2 = jax.nn.softmax(jnp.where(mask, s2, NEG), axis=-1)
        o_slc = jnp.einsum('bgnqk,bgqkd->bqgnd', p2.astype(vs.dtype), vs)
        return o_cmp, o_slc

    o_cmp, o_slc = lax.map(block, (qb, starts))
    o_cmp = o_cmp.transpose(1, 0, 2, 3, 4, 5).reshape(B, S, H, dh)
    o_slc = o_slc.transpose(1, 0, 2, 3, 4, 5).reshape(B, S, H, dh)
    return o_cmp, o_slc


def nsa_window(q, k, v):
    B, S, H, dh = q.shape
    G = N_NSA_KV
    scale = HEAD_DIM ** -0.5
    span = WINDOW + Q_BLOCK
    kp = jnp.pad(k, ((0, 0), (WINDOW, 0), (0, 0), (0, 0)))
    vp = jnp.pad(v, ((0, 0), (WINDOW, 0), (0, 0), (0, 0)))
    nq = S // Q_BLOCK
    qb = q.reshape(B, nq, Q_BLOCK, G, NSA_GROUP, dh).transpose(1, 0, 2, 3, 4, 5)
    starts = jnp.arange(nq) * Q_BLOCK

    def block(args):
        q_blk, start = args
        kb = lax.dynamic_slice_in_dim(kp, start, span, axis=1)
        vb = lax.dynamic_slice_in_dim(vp, start, span, axis=1)
        qpos = start + jnp.arange(Q_BLOCK)
        kpos = start - WINDOW + jnp.arange(span)
        s = jnp.einsum('bqgnd,bkgd->bgnqk', q_blk, kb).astype(jnp.float32) * scale
        diff = qpos[:, None] - kpos[None, :]
        mask = (kpos[None, :] >= 0) & (diff >= 0) & (diff < WINDOW)
        p = jax.nn.softmax(jnp.where(mask, s, NEG), axis=-1)
        return jnp.einsum('bgnqk,bkgd->bqgnd', p.astype(vb.dtype), vb)

    o = lax.map(block, (qb, starts))
    return o.transpose(1, 0, 2, 3, 4, 5).reshape(B, S, H, dh)


def hybrid_mixer(h, w_in, b_f, b_gate, cmpk_pe, cmpk_w1, cmpk_b1, cmpk_w2, cmpk_b2,
                 cmpv_pe, cmpv_w1, cmpv_b1, cmpv_w2, cmpv_b2, g_fox, g_nsa, w_out):
    B, S, _ = h.shape
    pos = jnp.arange(S)
    proj = h @ w_in
    offsets = np.cumsum(IN_SIZES)[:-1].tolist()
    (q_f, k_f, v_f, f_logit, q_n, kc, vc, ks, vs, kw, vw, gate) = jnp.split(proj, offsets, axis=-1)
    fh = lambda t: t.reshape(B, S, N_FOX_HEADS, HEAD_DIM)
    o_fox = forgetting_attention(fh(q_f), fh(k_f), fh(v_f), f_logit + b_f)
    kvh = lambda t: t.reshape(B, S, N_NSA_KV, HEAD_DIM)
    q_n = rope(q_n.reshape(B, S, N_NSA_HEADS, HEAD_DIM), pos)
    k_cmp = compress(kvh(kc), cmpk_pe, cmpk_w1, cmpk_b1, cmpk_w2, cmpk_b2)
    nc = k_cmp.shape[1]
    k_cmp = rope(k_cmp, jnp.arange(nc) * CMP_STRIDE + CMP_LEN - 1)
    v_cmp = compress(kvh(vc), cmpv_pe, cmpv_w1, cmpv_b1, cmpv_w2, cmpv_b2)
    o_cmp, o_slc = nsa_compressed_selected(q_n, k_cmp, v_cmp, rope(kvh(ks), pos), kvh(vs))
    o_win = nsa_window(q_n, rope(kvh(kw), pos), kvh(vw))
    g = jax.nn.sigmoid((gate + b_gate).astype(jnp.float32)).reshape(B, S, N_NSA_HEADS, 3).astype(h.dtype)
    o_nsa = g[..., 0:1] * o_cmp + g[..., 1:2] * o_slc + g[..., 2:3] * o_win
    y = jnp.concatenate([rmsnorm(o_fox.reshape(B, S, FOX_WIDTH), g_fox),
                         rmsnorm(o_nsa.reshape(B, S, NSA_WIDTH), g_nsa)], axis=-1)
    return y @ w_out


def squared_relu_mlp(h, w_up, w_down):
    return jnp.square(jax.nn.relu(h @ w_up)) @ w_down


def setup_inputs(seed: int = 0) -> dict:
    key = jax.random.key(seed)
    ks = jax.random.split(key, 24)
    nrm = lambda k, shape, fan: jax.random.normal(k, shape, jnp.float32) * (fan ** -0.5)
    gain = lambda k, shape: 1.0 + 0.02 * jax.random.normal(k, shape, jnp.float32)
    small = lambda k, shape: 0.01 * jax.random.normal(k, shape, jnp.float32)
    L = DEPTH
    return {
        "x": jax.random.normal(ks[0], (BATCH, SEQ, D_MODEL), jnp.float32),
        "g_attn": gain(ks[1], (L, D_MODEL)),
        "w_in": nrm(ks[2], (L, D_MODEL, N_IN), D_MODEL),
        "b_f": jax.random.uniform(ks[3], (L, N_FOX_HEADS), jnp.float32, 1.0, 5.0),
        "b_gate": small(ks[4], (L, 3 * N_NSA_HEADS)),
        "cmpk_pe": 0.1 * jax.random.normal(ks[5], (L, CMP_LEN, HEAD_DIM), jnp.float32),
        "cmpk_w1": nrm(ks[6], (L, CMP_LEN * HEAD_DIM, CMP_HIDDEN), CMP_LEN * HEAD_DIM),
        "cmpk_b1": small(ks[7], (L, CMP_HIDDEN)),
        "cmpk_w2": nrm(ks[8], (L, CMP_HIDDEN, HEAD_DIM), CMP_HIDDEN),
        "cmpk_b2": small(ks[9], (L, HEAD_DIM)),
        "cmpv_pe": 0.1 * jax.random.normal(ks[10], (L, CMP_LEN, HEAD_DIM), jnp.float32),
        "cmpv_w1": nrm(ks[11], (L, CMP_LEN * HEAD_DIM, CMP_HIDDEN), CMP_LEN * HEAD_DIM),
        "cmpv_b1": small(ks[12], (L, CMP_HIDDEN)),
        "cmpv_w2": nrm(ks[13], (L, CMP_HIDDEN, HEAD_DIM), CMP_HIDDEN),
        "cmpv_b2": small(ks[14], (L, HEAD_DIM)),
        "g_fox": gain(ks[15], (L, FOX_WIDTH)),
        "g_nsa": gain(ks[16], (L, NSA_WIDTH)),
        "w_out": nrm(ks[17], (L, MIX_WIDTH, D_MODEL), MIX_WIDTH),
        "g_mlp": gain(ks[18], (L, D_MODEL)),
        "w_up": nrm(ks[19], (L, D_MODEL, D_FF), D_MODEL),
        "w_down": nrm(ks[20], (L, D_FF, D_MODEL), D_FF),
        "g_final": gain(ks[21], (D_MODEL,)),
    }


def reference(x, g_attn, w_in, b_f, b_gate, cmpk_pe, cmpk_w1, cmpk_b1, cmpk_w2, cmpk_b2,
              cmpv_pe, cmpv_w1, cmpv_b1, cmpv_w2, cmpv_b2, g_fox, g_nsa, w_out,
              g_mlp, w_up, w_down, g_final):
    h = x
    for l in range(DEPTH):
        h = h + hybrid_mixer(rmsnorm(h, g_attn[l]), w_in[l], b_f[l], b_gate[l],
                             cmpk_pe[l], cmpk_w1[l], cmpk_b1[l], cmpk_w2[l], cmpk_b2[l],
                             cmpv_pe[l], cmpv_w1[l], cmpv_b1[l], cmpv_w2[l], cmpv_b2[l],
                             g_fox[l], g_nsa[l], w_out[l])
        h = h + squared_relu_mlp(rmsnorm(h, g_mlp[l]), w_up[l], w_down[l])
    return rmsnorm(h, g_final)
```

```python
import functools

import numpy as np
import jax
import jax.numpy as jnp
from jax import lax
from jax.experimental import pallas as pl
from jax.experimental.pallas import tpu as pltpu

F32 = jnp.float32
BF16 = jnp.bfloat16

HEAD_DIM = 64
N_HEADS = 8
N_KV = 2
GROUP = N_HEADS // N_KV
WIDTH = N_HEADS * HEAD_DIM
KV_WIDTH = N_KV * HEAD_DIM
CMP_LEN = 32
CMP_STRIDE = 16
CMP_HIDDEN = 256
SEL_LEN = 64
SEL_TOPK = 16
SEL_SLOTS = 64
WINDOW = 512
ROPE_THETA = 10000.0
EPS = 1e-6
NEG = -1e30
MASK_FILL = -(2.0 ** 100)
FORCE_BONUS = 1e4
LANES = 128
HALF = LANES // 2

VMEM_LIMIT = 56 * 1024 * 1024


def _dot(a, b, precision=None):
    return jnp.dot(a, b, preferred_element_type=F32, precision=precision)


def _dot_nt(a, b, precision=None):
    return lax.dot_general(a, b, (((1,), (1,)), ((), ())), preferred_element_type=F32, precision=precision)


def _rms(x, g):
    return x * lax.rsqrt(jnp.mean(x * x, axis=-1, keepdims=True) + EPS) * g


def _rope(x, cos, sin_signed):
    w = x.shape[-1]
    lane = lax.broadcasted_iota(jnp.int32, (1, w), 1)
    first = (lane % HEAD_DIM) < (HEAD_DIM // 2)
    partner = jnp.where(first, pltpu.roll(x, w - HEAD_DIM // 2, 1), pltpu.roll(x, HEAD_DIM // 2, 1))
    return x * cos + partner * sin_signed


def _inproj_kernel(x_ref, g_ref, w_ref, cos_ref, sin_ref,
                   qf_ref, kf_ref, vf_ref, qn_ref, kc_ref, vc_ref, ks_ref, vs_ref, kw_ref, vw_ref, fg_ref):
    hb = _rms(x_ref[...], g_ref[...]).astype(BF16)

    def proj(lo, width):
        return _dot(hb, w_ref[:, lo:lo + width])

    cos = cos_ref[...]
    sin = sin_ref[...]
    qf_ref[...] = proj(0, WIDTH).astype(BF16)
    kf_ref[...] = proj(WIDTH, WIDTH).astype(BF16)
    vf_ref[...] = proj(2 * WIDTH, WIDTH).astype(BF16)
    qn_ref[...] = _rope(proj(3 * WIDTH, WIDTH), cos, sin).astype(BF16)
    base = 4 * WIDTH
    kc_ref[...] = proj(base, KV_WIDTH).astype(BF16)
    vc_ref[...] = proj(base + KV_WIDTH, KV_WIDTH).astype(BF16)
    ks_ref[...] = _rope(proj(base + 2 * KV_WIDTH, KV_WIDTH), cos[:, :KV_WIDTH], sin[:, :KV_WIDTH]).astype(BF16)
    vs_ref[...] = proj(base + 3 * KV_WIDTH, KV_WIDTH).astype(BF16)
    kw_ref[...] = _rope(proj(base + 4 * KV_WIDTH, KV_WIDTH), cos[:, :KV_WIDTH], sin[:, :KV_WIDTH]).astype(BF16)
    vw_ref[...] = proj(base + 5 * KV_WIDTH, KV_WIDTH).astype(BF16)
    fg_ref[...] = proj(base + 6 * KV_WIDTH, LANES)


def _inproj(x2, g_attn, w_perm, cos_t, sin_t, seq, tm):
    m, d = x2.shape
    n_cols = w_perm.shape[1]
    sblocks = seq // tm
    row = lambda i: (i, 0)
    fixed = lambda i: (0, 0)
    tab = lambda i: (i % sblocks, 0)
    wide = jax.ShapeDtypeStruct((m, WIDTH), BF16)
    kv = jax.ShapeDtypeStruct((m, KV_WIDTH), BF16)
    out_shape = [wide, wide, wide, wide, kv, kv, kv, kv, kv, kv, jax.ShapeDtypeStruct((m, LANES), F32)]
    out_specs = ([pl.BlockSpec((tm, WIDTH), row)] * 4 + [pl.BlockSpec((tm, KV_WIDTH), row)] * 6
                 + [pl.BlockSpec((tm, LANES), row)])
    return pl.pallas_call(
        _inproj_kernel,
        grid=(m // tm,),
        in_specs=[pl.BlockSpec((tm, d), row), pl.BlockSpec((1, d), fixed), pl.BlockSpec((d, n_cols), fixed),
                  pl.BlockSpec((tm, WIDTH), tab), pl.BlockSpec((tm, WIDTH), tab)],
        out_specs=out_specs,
        out_shape=out_shape,
        compiler_params=pltpu.CompilerParams(dimension_semantics=("parallel",), vmem_limit_bytes=VMEM_LIMIT),
        name="inproj",
    )(x2, g_attn, w_perm, cos_t, sin_t)


CUM_BLOCK = 256


def _cumgate_kernel(fg_ref, bf_ref, ccol_ref, crow_ref):
    seq = fg_ref.shape[1]
    r = lax.broadcasted_iota(jnp.int32, (CUM_BLOCK, CUM_BLOCK), 0)
    c = lax.broadcasted_iota(jnp.int32, (CUM_BLOCK, CUM_BLOCK), 1)
    tri = jnp.where(r >= c, 1.0, 0.0).astype(F32)
    er = lax.broadcasted_iota(jnp.int32, (N_HEADS, LANES), 0)
    ec = lax.broadcasted_iota(jnp.int32, (N_HEADS, LANES), 1)
    pick = jnp.where(er == ec, 1.0, 0.0).astype(F32)
    bias = bf_ref[...]

    def body(blk, carry):
        off = pl.multiple_of(blk * CUM_BLOCK, CUM_BLOCK)
        z = fg_ref[0, pl.ds(off, CUM_BLOCK), :] + bias
        logf = jnp.minimum(z, 0.0) - jnp.log(1.0 + jnp.exp(-jnp.abs(z)))
        cs = _dot(tri, logf, precision=lax.Precision.HIGHEST) + carry
        ccol_ref[0, pl.ds(off, CUM_BLOCK), :] = cs
        crow_ref[0, :, pl.ds(off, CUM_BLOCK)] = _dot_nt(pick, cs, precision=lax.Precision.HIGHEST)
        return cs[CUM_BLOCK - 1:CUM_BLOCK, :]

    lax.fori_loop(0, seq // CUM_BLOCK, body, jnp.zeros((1, LANES), F32))


def _cumgate(fg3, bf_pad):
    b, seq, _ = fg3.shape
    return pl.pallas_call(
        _cumgate_kernel,
        grid=(b,),
        in_specs=[pl.BlockSpec((1, seq, LANES), lambda i: (i, 0, 0)), pl.BlockSpec((1, LANES), lambda i: (0, 0))],
        out_specs=[pl.BlockSpec((1, seq, LANES), lambda i: (i, 0, 0)),
                   pl.BlockSpec((1, N_HEADS, seq), lambda i: (i, 0, 0))],
        out_shape=[jax.ShapeDtypeStruct((b, seq, LANES), F32), jax.ShapeDtypeStruct((b, N_HEADS, seq), F32)],
        compiler_params=pltpu.CompilerParams(dimension_semantics=("parallel",), vmem_limit_bytes=VMEM_LIMIT),
        name="cumgate",
    )(fg3, bf_pad)


def _compress_one(x_ref, pea_ref, peb_ref, wa_ref, wb_ref, b1_ref, w2_ref, b2_ref):
    xk = x_ref[0]
    n = xk.shape[0]
    wa = wa_ref[...]
    wb = wb_ref[...]
    first = _dot(xk, wa)
    second = _dot(xk, wb)
    pe_term = _dot(pea_ref[...].astype(BF16), wa) + _dot(peb_ref[...].astype(BF16), wb)
    hidden = first + pltpu.roll(second, n - 1, 0) + pe_term[0:1, :] + b1_ref[...]
    act = jax.nn.gelu(hidden)
    return _dot(act.astype(BF16), w2_ref[...]) + b2_ref[...]


def _compress_kernel(kc_ref, vc_ref, cos_ref, sin_ref,
                     kpea, kpeb, kwa, kwb, kb1, kw2, kb2,
                     vpea, vpeb, vwa, vwb, vb1, vw2, vb2,
                     kcmp_ref, vcmp_ref):
    kcmp = _compress_one(kc_ref, kpea, kpeb, kwa, kwb, kb1, kw2, kb2)
    kcmp_ref[0] = _rope(kcmp, cos_ref[...], sin_ref[...]).astype(BF16)
    vcmp_ref[0] = _compress_one(vc_ref, vpea, vpeb, vwa, vwb, vb1, vw2, vb2).astype(BF16)


def _compress(kc3, vc3, cos_c, sin_c, kparams, vparams):
    b, n, w = kc3.shape
    x_spec = pl.BlockSpec((1, n, w), lambda i: (i, 0, 0))
    full = lambda a: pl.BlockSpec(a.shape, lambda i: (0,) * a.ndim)
    params = list(kparams) + list(vparams)
    out_spec = pl.BlockSpec((1, n, KV_WIDTH), lambda i: (i, 0, 0))
    out = jax.ShapeDtypeStruct((b, n, KV_WIDTH), BF16)
    return pl.pallas_call(
        _compress_kernel,
        grid=(b,),
        in_specs=[x_spec, x_spec, full(cos_c), full(sin_c)] + [full(p) for p in params],
        out_specs=[out_spec, out_spec],
        out_shape=[out, out],
        compiler_params=pltpu.CompilerParams(dimension_semantics=("parallel",), vmem_limit_bytes=VMEM_LIMIT),
        name="compress",
    )(kc3, vc3, cos_c, sin_c, *params)


def _flash_init(rows):
    return (jnp.full((rows, 1), NEG, F32), jnp.zeros((rows, 1), F32), jnp.zeros((rows, LANES), F32))


def _flash_step(carry, s, v):
    m, l, acc = carry
    m_new = jnp.maximum(m, jnp.max(s, axis=-1, keepdims=True))
    alpha = jnp.exp(m - m_new)
    p = jnp.exp(s - m_new)
    l = alpha * l + jnp.sum(p, axis=-1, keepdims=True)
    acc = alpha * acc + _dot(p.astype(BF16), v)
    return m_new, l, acc


def _flash_out(carry):
    _, l, acc = carry
    return acc / l


FOX_TILE = 256


def _fox_kernel(q_ref, k_ref, v_ref, ccol_ref, crow_ref, o_ref):
    t = FOX_TILE
    i = pl.program_id(1)
    lane = lax.broadcasted_iota(jnp.int32, (1, LANES), 1)
    lo = lane < HALF
    rr = lax.broadcasted_iota(jnp.int32, (t, t), 0)
    cc = lax.broadcasted_iota(jnp.int32, (t, t), 1)
    causal = cc <= rr
    diag = pl.multiple_of(i * t, t)

    for col in range(WIDTH // LANES):
        cs = slice(col * LANES, (col + 1) * LANES)
        q2 = q_ref[0, :, cs]
        outs = []
        for half in range(2):
            h = 2 * col + half
            mine = lo if half == 0 else jnp.logical_not(lo)
            c_t = ccol_ref[0, :, h:h + 1]

            def scores(off, h=h, mine=mine, c_t=c_t, q2=q2, cs=cs):
                k2 = jnp.where(mine, k_ref[0, pl.ds(off, t), cs], jnp.zeros((), BF16))
                c_s = crow_ref[0, h:h + 1, pl.ds(off, t)]
                return _dot_nt(q2, k2) + (c_t - c_s)

            def body(j, carry, scores=scores, cs=cs):
                off = pl.multiple_of(j * t, t)
                return _flash_step(carry, scores(off), v_ref[0, pl.ds(off, t), cs])

            carry = lax.fori_loop(0, i, body, _flash_init(t))
            s = jnp.where(causal, scores(diag), MASK_FILL)
            carry = _flash_step(carry, s, v_ref[0, pl.ds(diag, t), cs])
            outs.append(_flash_out(carry))
        o_ref[0, :, cs] = jnp.where(lo, outs[0], outs[1]).astype(BF16)


def _fox(qf, kf, vf, ccol, crow):
    b, seq, w = qf.shape
    t = FOX_TILE
    tile = lambda bi, i: (bi, i, 0)
    whole = lambda bi, i: (bi, 0, 0)
    return pl.pallas_call(
        _fox_kernel,
        grid=(b, seq // t),
        in_specs=[pl.BlockSpec((1, t, w), tile), pl.BlockSpec((1, seq, w), whole), pl.BlockSpec((1, seq, w), whole),
                  pl.BlockSpec((1, t, LANES), tile), pl.BlockSpec((1, N_HEADS, seq), whole)],
        out_specs=pl.BlockSpec((1, t, w), tile),
        out_shape=jax.ShapeDtypeStruct((b, seq, w), BF16),
        compiler_params=pltpu.CompilerParams(dimension_semantics=("parallel", "arbitrary"),
                                             vmem_limit_bytes=VMEM_LIMIT),
        name="fox",
    )(qf, kf, vf, ccol, crow)


NSA_TILE = 128


def _nsa_kernel(q_ref, kcmp_ref, vcmp_ref, ks_ref, vs_ref, kw_ref, vw_ref, fg_ref, bg_ref, ovl_ref, o_ref,
                *, topk):
    t = NSA_TILE
    n_cmp = kcmp_ref.shape[1]
    i = pl.program_id(1)
    start = i * t
    diag = pl.multiple_of(start, t)
    lane = lax.broadcasted_iota(jnp.int32, (1, LANES), 1)
    lo = lane < HALF
    zero_bf = jnp.zeros((), BF16)

    qcols = [q_ref[0, :, n * LANES:(n + 1) * LANES] for n in range(GROUP)]
    gates = jax.nn.sigmoid(fg_ref[0] + bg_ref[...])

    qpos_col = start + lax.broadcasted_iota(jnp.int32, (t, 1), 0)
    qpos_row = start + lax.broadcasted_iota(jnp.int32, (1, t), 1)

    cmp_end = lax.broadcasted_iota(jnp.int32, (1, n_cmp), 1) * CMP_STRIDE + (CMP_LEN - 1)
    vis = cmp_end <= qpos_col
    slot = lax.broadcasted_iota(jnp.int32, (SEL_SLOTS, t), 0)
    cur = qpos_row // SEL_LEN
    valid = slot <= cur
    forced = (slot == 0) | (slot == cur) | (slot == cur - 1)
    vcm = vcmp_ref[0]
    o_cmp = {}
    sel_t = []
    for g in range(N_KV):
        mine = lo if g == 0 else jnp.logical_not(lo)
        kc = jnp.where(mine, kcmp_ref[0], zero_bf)
        psum = jnp.zeros((t, n_cmp), F32)
        for n in range(GROUP):
            s = jnp.where(vis, _dot_nt(qcols[n], kc), NEG)
            e = jnp.where(vis, jnp.exp(s - jnp.max(s, axis=-1, keepdims=True)), 0.0)
            denom = jnp.sum(e, axis=-1, keepdims=True)
            p = e * jnp.where(denom > 0.0, 1.0 / denom, 0.0)
            o_cmp[(g, n)] = _dot(p.astype(BF16), vcm)
            psum = psum + p
        imp_t = _dot_nt(ovl_ref[...], psum, precision=lax.Precision.HIGHEST)
        score = jnp.where(valid, imp_t + jnp.where(forced, FORCE_BONUS, 0.0), -1.0)
        rank = jnp.zeros((SEL_SLOTS, t), F32)
        for jp in range(SEL_SLOTS):
            other = score[jp:jp + 1, :]
            rank = rank + jnp.where(slot > jp, jnp.where(other >= score, 1.0, 0.0),
                                    jnp.where(other > score, 1.0, 0.0))
        sel_t.append(jnp.where(rank < topk, 0.0, MASK_FILL))
    selb = jnp.concatenate([sel_t[1], sel_t[0]], axis=0).T.astype(BF16)

    rr = lax.broadcasted_iota(jnp.int32, (GROUP * t, t), 0) % t
    kk = lax.broadcasted_iota(jnp.int32, (GROUP * t, t), 1)
    causal = kk <= rr
    key_row = lax.broadcasted_iota(jnp.int32, (t, LANES), 0)
    lane_full = lax.broadcasted_iota(jnp.int32, (t, LANES), 1)

    o_slc = {}
    o_win = {}
    for g in range(N_KV):
        mine = lo if g == 0 else jnp.logical_not(lo)
        q_aug = jnp.concatenate(
            [jnp.where(mine, qcols[n], selb) for n in range(GROUP)], axis=0)
        slot_lane = lane_full - HALF if g == 0 else lane_full

        def sel_scores(off, mine=mine, q_aug=q_aug, slot_lane=slot_lane):
            onehot = jnp.where((off + key_row) // SEL_LEN == slot_lane, 1.0, 0.0).astype(BF16)
            k_aug = jnp.where(mine, ks_ref[0, pl.ds(off, t), :], onehot)
            return _dot_nt(q_aug, k_aug)

        def sel_body(j, carry, sel_scores=sel_scores):
            off = pl.multiple_of(j * t, t)
            return _flash_step(carry, sel_scores(off), vs_ref[0, pl.ds(off, t), :])

        carry = lax.fori_loop(0, i, sel_body, _flash_init(GROUP * t))
        carry = _flash_step(carry, jnp.where(causal, sel_scores(diag), MASK_FILL), vs_ref[0, pl.ds(diag, t), :])
        out = _flash_out(carry)
        for n in range(GROUP):
            o_slc[(g, n)] = out[n * t:(n + 1) * t, :]

        q_win = jnp.concatenate(qcols, axis=0)

        def win_scores(off, mine=mine, q_win=q_win):
            return _dot_nt(q_win, jnp.where(mine, kw_ref[0, pl.ds(off, t), :], zero_bf))

        def win_body(j, carry, win_scores=win_scores):
            off = pl.multiple_of(j * t, t)
            s = win_scores(off)
            in_band = (start + rr) - (off + kk) < WINDOW
            return _flash_step(carry, jnp.where(in_band, s, MASK_FILL), vw_ref[0, pl.ds(off, t), :])

        first = jnp.maximum(i - WINDOW // t, 0)
        carry = lax.fori_loop(first, i, win_body, _flash_init(GROUP * t))
        carry = _flash_step(carry, jnp.where(causal, win_scores(diag), MASK_FILL), vw_ref[0, pl.ds(diag, t), :])
        out = _flash_out(carry)
        for n in range(GROUP):
            o_win[(g, n)] = out[n * t:(n + 1) * t, :]

    for n in range(GROUP):
        mixed = []
        for g in range(N_KV):
            base = N_HEADS + (g * GROUP + n) * 3
            mixed.append(gates[:, base:base + 1] * o_cmp[(g, n)]
                         + gates[:, base + 1:base + 2] * o_slc[(g, n)]
                         + gates[:, base + 2:base + 3] * o_win[(g, n)])
        o_ref[0, :, n * LANES:(n + 1) * LANES] = jnp.where(lo, mixed[0], mixed[1]).astype(BF16)


def _nsa(qn, kcmp, vcmp, ks, vs, kw, vw, fg3, bg_pad, ovl_t, topk):
    b, seq, w = qn.shape
    t = NSA_TILE
    n_cmp = kcmp.shape[1]
    tile = lambda bi, i: (bi, i, 0)
    whole = lambda bi, i: (bi, 0, 0)
    fixed = lambda bi, i: (0, 0)
    kv_spec = pl.BlockSpec((1, seq, KV_WIDTH), whole)
    cmp_spec = pl.BlockSpec((1, n_cmp, KV_WIDTH), whole)
    return pl.pallas_call(
        functools.partial(_nsa_kernel, topk=topk),
        grid=(b, seq // t),
        in_specs=[pl.BlockSpec((1, t, w), tile), cmp_spec, cmp_spec, kv_spec, kv_spec, kv_spec, kv_spec,
                  pl.BlockSpec((1, t, LANES), tile), pl.BlockSpec((1, LANES), fixed),
                  pl.BlockSpec(ovl_t.shape, fixed)],
        out_specs=pl.BlockSpec((1, t, w), tile),
        out_shape=jax.ShapeDtypeStruct((b, seq, w), BF16),
        compiler_params=pltpu.CompilerParams(dimension_semantics=("parallel", "arbitrary"),
                                             vmem_limit_bytes=VMEM_LIMIT),
        name="nsa",
    )(qn, kcmp, vcmp, ks, vs, kw, vw, fg3, bg_pad, ovl_t)


def _post_kernel(x_ref, of_ref, on_ref, gf_ref, gn_ref, wo_ref, gm_ref, wu_ref, wd_ref, gl_ref, o_ref):
    yf = _rms(of_ref[...].astype(F32), gf_ref[...]).astype(BF16)
    yn = _rms(on_ref[...].astype(F32), gn_ref[...]).astype(BF16)
    h1 = x_ref[...] + _dot(yf, wo_ref[0:WIDTH, :]) + _dot(yn, wo_ref[WIDTH:2 * WIDTH, :])
    u = _dot(_rms(h1, gm_ref[...]).astype(BF16), wu_ref[...])
    act = jnp.square(jnp.maximum(u, 0.0)).astype(BF16)
    h2 = h1 + _dot(act, wd_ref[...])
    o_ref[...] = _rms(h2, gl_ref[...])


def _post(x2, ofox, onsa, g_fox, g_nsa, w_out, g_mlp, w_up, w_down, g_final, tm):
    m, d = x2.shape
    row = lambda i: (i, 0)
    fixed = lambda i: (0, 0)
    full = lambda a: pl.BlockSpec(a.shape, fixed, pipeline_mode=pl.Buffered(1))
    return pl.pallas_call(
        _post_kernel,
        grid=(m // tm,),
        in_specs=[pl.BlockSpec((tm, d), row), pl.BlockSpec((tm, WIDTH), row), pl.BlockSpec((tm, WIDTH), row),
                  full(g_fox), full(g_nsa), full(w_out), full(g_mlp), full(w_up), full(w_down), full(g_final)],
        out_specs=pl.BlockSpec((tm, d), row),
        out_shape=jax.ShapeDtypeStruct((m, d), F32),
        compiler_params=pltpu.CompilerParams(dimension_semantics=("parallel",), vmem_limit_bytes=VMEM_LIMIT),
        name="post",
    )(x2, ofox, onsa, g_fox, g_nsa, w_out, g_mlp, w_up, w_down, g_final)


_NSA_PERM = np.array([(p % 2) * GROUP + p // 2 for p in range(N_HEADS)])


def _perm_heads(a, axis):
    shape = a.shape
    a = a.reshape(shape[:axis] + (N_HEADS, HEAD_DIM) + shape[axis + 1:])
    a = jnp.take(a, _NSA_PERM, axis=axis)
    return a.reshape(shape)


def _prep_w_in(w):
    scale = HEAD_DIM ** -0.5
    sizes = [WIDTH, WIDTH, WIDTH, N_HEADS, WIDTH] + [KV_WIDTH] * 6 + [3 * N_HEADS]
    offs = np.cumsum([0] + sizes)
    part = [w[:, offs[k]:offs[k + 1]] for k in range(len(sizes))]
    qf, kf, vf, fl, qn = part[0], part[1], part[2], part[3], part[4]
    kvs = part[5:11]
    gate = part[11]
    pad = jnp.zeros((w.shape[0], LANES - N_HEADS - 3 * N_HEADS), w.dtype)
    cols = [qf * scale, kf, vf, _perm_heads(qn * scale, 1)] + kvs + [fl, gate, pad]
    return jnp.concatenate(cols, axis=1).astype(BF16)


def _prep_compress(pe, w1, b1, w2, b2):
    half = CMP_LEN // 2
    eye = jnp.eye(N_KV, dtype=F32)

    def expand_w1(wpart):
        w3 = wpart.reshape(half, HEAD_DIM, CMP_HIDDEN)
        return jnp.einsum('ldh,gk->lgdkh', w3, eye).reshape(half * KV_WIDTH, N_KV * CMP_HIDDEN).astype(BF16)

    def expand_pe(ppart):
        flat = jnp.broadcast_to(ppart[:, None, :], (half, N_KV, HEAD_DIM)).reshape(1, half * KV_WIDTH)
        return jnp.broadcast_to(flat, (8, half * KV_WIDTH))

    w2b = jnp.einsum('hd,gk->ghkd', w2, eye).reshape(N_KV * CMP_HIDDEN, KV_WIDTH).astype(BF16)
    return (expand_pe(pe[:half]), expand_pe(pe[half:]),
            expand_w1(w1[:half * HEAD_DIM]), expand_w1(w1[half * HEAD_DIM:]),
            jnp.tile(b1, N_KV)[None, :], w2b, jnp.tile(b2, N_KV)[None, :])


def _rope_tables(pos, reps):
    half = HEAD_DIM // 2
    inv = ROPE_THETA ** (-jnp.arange(half, dtype=F32) / half)
    ang = pos.astype(F32)[:, None] * inv[None, :]
    cos = jnp.cos(ang)
    sin = jnp.sin(ang)
    return (jnp.tile(jnp.concatenate([cos, cos], axis=1), (1, reps)),
            jnp.tile(jnp.concatenate([-sin, sin], axis=1), (1, reps)))


def _overlap_t(n_cmp_slots, n_cmp, n_blocks):
    c = np.arange(n_cmp_slots)[None, :] * CMP_STRIDE
    j = np.arange(SEL_SLOTS)[:, None]
    s = j * SEL_LEN
    ovl = (c < s + SEL_LEN) & (c + CMP_LEN > s) & (np.arange(n_cmp_slots)[None, :] < n_cmp) & (j < n_blocks)
    return jnp.asarray(ovl.astype(np.float32))


def _row_tile(m, want):
    t = want
    while m % t:
        t //= 2
    return t


def _layer(h, seq, g_attn, w_in, b_f, b_gate, cmpk, cmpv, g_fox, g_nsa, w_out, g_mlp, w_up, w_down, g_out):
    m, d = h.shape
    b = m // seq
    n_chunks = seq // CMP_STRIDE
    n_cmp = (seq - CMP_LEN) // CMP_STRIDE + 1
    n_blocks = seq // SEL_LEN
    assert seq % FOX_TILE == 0 and n_chunks % 8 == 0 and SEL_TOPK <= n_blocks <= SEL_SLOTS
    assert n_cmp == n_chunks - 1

    cos_t, sin_t = _rope_tables(jnp.arange(seq), N_HEADS)
    cos_c, sin_c = _rope_tables(jnp.arange(n_chunks) * CMP_STRIDE + CMP_LEN - 1, N_KV)

    tm = _row_tile(seq, 512)
    qf, kf, vf, qn, kc, vc, ks, vs, kw, vw, fg = _inproj(h, g_attn[None, :], _prep_w_in(w_in), cos_t, sin_t, seq, tm)
    r3 = lambda a: a.reshape(b, seq, a.shape[-1])
    fg3 = r3(fg)

    bf_pad = jnp.zeros((1, LANES), F32).at[0, :N_HEADS].set(b_f)
    bg_pad = jnp.zeros((1, LANES), F32).at[0, N_HEADS:4 * N_HEADS].set(b_gate)
    ccol, crow = _cumgate(fg3, bf_pad)

    chunked = lambda a: a.reshape(b, n_chunks, CMP_STRIDE * KV_WIDTH)
    kcmp, vcmp = _compress(chunked(kc), chunked(vc), cos_c, sin_c, _prep_compress(*cmpk), _prep_compress(*cmpv))

    ofox = _fox(r3(qf), r3(kf), r3(vf), ccol, crow)
    onsa = _nsa(r3(qn), kcmp, vcmp, r3(ks), r3(vs), r3(kw), r3(vw), fg3, bg_pad,
                _overlap_t(n_chunks, n_cmp, n_blocks), min(SEL_TOPK, n_blocks))

    w_out_p = jnp.concatenate([w_out[:WIDTH], _perm_heads(w_out[WIDTH:], 0)], axis=0).astype(BF16)
    return _post(h, ofox.reshape(m, WIDTH), onsa.reshape(m, WIDTH), g_fox[None, :], _perm_heads(g_nsa, 0)[None, :],
                 w_out_p, g_mlp[None, :], w_up.astype(BF16), w_down.astype(BF16), g_out[None, :],
                 _row_tile(seq, 256))


def kernel(x, g_attn, w_in, b_f, b_gate, cmpk_pe, cmpk_w1, cmpk_b1, cmpk_w2, cmpk_b2, cmpv_pe, cmpv_w1, cmpv_b1,
           cmpv_w2, cmpv_b2, g_fox, g_nsa, w_out, g_mlp, w_up, w_down, g_final):
    b, seq, d = x.shape
    depth = g_attn.shape[0]
    assert depth == 1, "the final rmsnorm is fused into the (single) layer's last kernel"
    h = x.reshape(b * seq, d)
    out = _layer(h, seq, g_attn[0], w_in[0], b_f[0], b_gate[0],
                 (cmpk_pe[0], cmpk_w1[0], cmpk_b1[0], cmpk_w2[0], cmpk_b2[0]),
                 (cmpv_pe[0], cmpv_w1[0], cmpv_b1[0], cmpv_w2[0], cmpv_b2[0]),
                 g_fox[0], g_nsa[0], w_out[0], g_mlp[0], w_up[0], w_down[0], g_final)
    return out.reshape(b, seq, d)
```

```python
import functools

import numpy as np
import jax
import jax.numpy as jnp
from jax import lax
from jax.experimental import pallas as pl
from jax.experimental.pallas import tpu as pltpu

F32 = jnp.float32
BF16 = jnp.bfloat16

HEAD_DIM = 64
N_HEADS = 8
N_KV = 2
GROUP = N_HEADS // N_KV
WIDTH = N_HEADS * HEAD_DIM
KV_WIDTH = N_KV * HEAD_DIM
CMP_LEN = 32
CMP_STRIDE = 16
CMP_HIDDEN = 256
SEL_LEN = 64
SEL_TOPK = 16
SEL_SLOTS = 64
WINDOW = 512
ROPE_THETA = 10000.0
EPS = 1e-6
NEG = -1e30
MASK_FILL = -(2.0 ** 100)
FORCE_BONUS = 1e4
LOG2E = 1.4426950408889634
LANES = 128
HALF = LANES // 2
BF16_ROWS = 16

VMEM_LIMIT = 56 * 1024 * 1024


def _dot(a, b, precision=None):
    return jnp.dot(a, b, preferred_element_type=F32, precision=precision)


def _dot_nt(a, b, precision=None):
    return lax.dot_general(a, b, (((1,), (1,)), ((), ())), preferred_element_type=F32, precision=precision)


def _rms(x, g):
    return x * lax.rsqrt(jnp.mean(x * x, axis=-1, keepdims=True) + EPS) * g


def _rope(x, cos, sin_signed):
    w = x.shape[-1]
    lane = lax.broadcasted_iota(jnp.int32, (1, w), 1)
    first = (lane % HEAD_DIM) < (HEAD_DIM // 2)
    partner = jnp.where(first, pltpu.roll(x, w - HEAD_DIM // 2, 1), pltpu.roll(x, HEAD_DIM // 2, 1))
    return x * cos + partner * sin_signed


def _rope_t(x, cos, sin_signed):
    r = x.shape[0]
    row = lax.broadcasted_iota(jnp.int32, (r, 1), 0)
    first = (row % HEAD_DIM) < (HEAD_DIM // 2)
    partner = jnp.where(first, pltpu.roll(x, r - HEAD_DIM // 2, 0), pltpu.roll(x, HEAD_DIM // 2, 0))
    return x * cos + partner * sin_signed


def _split3(c):
    c1 = c.astype(BF16).astype(F32)
    r1 = c - c1
    c2 = r1.astype(BF16).astype(F32)
    return c1, c2, r1 - c2


NAT_KF, NAT_KC, NAT_VC, NAT_KS, NAT_KW, NAT_FG, NAT_COLS = 0, 512, 640, 768, 896, 1024, 1152
TR_QF, TR_VF, TR_QN, TR_VS, TR_VW, TR_FG, TR_ROWS = 0, 512, 1024, 1536, 1664, 1792, 1920


def _inproj_kernel(x_ref, g_ref, wn_ref, wt_ref, cos_ref, sin_ref, cost_ref, sint_ref,
                   kf_ref, kc_ref, vc_ref, ks_ref, kw_ref, fg_ref,
                   qft_ref, vft_ref, qnt_ref, vst_ref, vwt_ref, fgt_ref):
    hb = _rms(x_ref[...], g_ref[...]).astype(BF16)

    def nat(lo, width):
        return _dot(hb, wn_ref[:, lo:lo + width])

    def tr(lo, rows):
        return _dot_nt(wt_ref[lo:lo + rows, :], hb)

    cos = cos_ref[...]
    sin = sin_ref[...]
    kf_ref[...] = nat(NAT_KF, WIDTH).astype(BF16)
    kc_ref[...] = nat(NAT_KC, KV_WIDTH).astype(BF16)
    vc_ref[...] = nat(NAT_VC, KV_WIDTH).astype(BF16)
    ks_ref[...] = _rope(nat(NAT_KS, KV_WIDTH), cos, sin).astype(BF16)
    kw_ref[...] = _rope(nat(NAT_KW, KV_WIDTH), cos, sin).astype(BF16)
    fg_ref[...] = nat(NAT_FG, LANES)
    qft_ref[...] = tr(TR_QF, WIDTH).astype(BF16)
    vft_ref[...] = tr(TR_VF, WIDTH).astype(BF16)
    qnt_ref[...] = _rope_t(tr(TR_QN, WIDTH), cost_ref[...], sint_ref[...]).astype(BF16)
    vst_ref[...] = tr(TR_VS, KV_WIDTH).astype(BF16)
    vwt_ref[...] = tr(TR_VW, KV_WIDTH).astype(BF16)
    fgt_ref[...] = tr(TR_FG, LANES)


def _inproj(x2, g_attn, w_nat, w_tr, cos_n, sin_n, cos_t, sin_t, seq, tm):
    m, d = x2.shape
    sblocks = seq // tm
    row = lambda i: (i, 0)
    col = lambda i: (0, i)
    fixed = lambda i: (0, 0)
    tab = lambda i: (i % sblocks, 0)
    tab_t = lambda i: (0, i % sblocks)
    sds = jax.ShapeDtypeStruct
    out_shape = [sds((m, WIDTH), BF16)] + [sds((m, KV_WIDTH), BF16)] * 4 + [sds((m, LANES), F32),
                 sds((WIDTH, m), BF16), sds((WIDTH, m), BF16), sds((WIDTH, m), BF16),
                 sds((KV_WIDTH, m), BF16), sds((KV_WIDTH, m), BF16), sds((LANES, m), F32)]
    out_specs = ([pl.BlockSpec((tm, WIDTH), row)] + [pl.BlockSpec((tm, KV_WIDTH), row)] * 4
                 + [pl.BlockSpec((tm, LANES), row)]
                 + [pl.BlockSpec((WIDTH, tm), col)] * 3 + [pl.BlockSpec((KV_WIDTH, tm), col)] * 2
                 + [pl.BlockSpec((LANES, tm), col)])
    return pl.pallas_call(
        _inproj_kernel,
        grid=(m // tm,),
        in_specs=[pl.BlockSpec((tm, d), row), pl.BlockSpec((1, d), fixed),
                  pl.BlockSpec(w_nat.shape, fixed), pl.BlockSpec(w_tr.shape, fixed),
                  pl.BlockSpec((tm, KV_WIDTH), tab), pl.BlockSpec((tm, KV_WIDTH), tab),
                  pl.BlockSpec((WIDTH, tm), tab_t), pl.BlockSpec((WIDTH, tm), tab_t)],
        out_specs=out_specs,
        out_shape=out_shape,
        compiler_params=pltpu.CompilerParams(dimension_semantics=("parallel",), vmem_limit_bytes=VMEM_LIMIT),
        name="inproj",
    )(x2, g_attn, w_nat, w_tr, cos_n, sin_n, cos_t, sin_t)


CUM_BLOCK = 256
N_BIAS = 3


def _bias_lane_base(h):
    return HALF if h % 2 == 0 else 0


def _cumgate_kernel(fg_ref, bf_ref, kf_ref, crow_ref, kaug_ref):
    seq = fg_ref.shape[1]
    r = lax.broadcasted_iota(jnp.int32, (CUM_BLOCK, CUM_BLOCK), 0)
    c = lax.broadcasted_iota(jnp.int32, (CUM_BLOCK, CUM_BLOCK), 1)
    tri = jnp.where(r >= c, 1.0, 0.0).astype(F32)
    er = lax.broadcasted_iota(jnp.int32, (N_HEADS, LANES), 0)
    ec = lax.broadcasted_iota(jnp.int32, (N_HEADS, LANES), 1)
    pick = jnp.where(er == ec, 1.0, 0.0).astype(F32)
    bias = bf_ref[...]
    lane = lax.broadcasted_iota(jnp.int32, (CUM_BLOCK, LANES), 1)

    def body(blk, carry):
        off = pl.multiple_of(blk * CUM_BLOCK, CUM_BLOCK)
        z = fg_ref[0, pl.ds(off, CUM_BLOCK), :] + bias
        logf = jnp.minimum(z, 0.0) - jnp.log(1.0 + jnp.exp(-jnp.abs(z)))
        cs = _dot(tri, logf, precision=lax.Precision.HIGHEST) + carry
        c2 = cs * LOG2E
        crow_ref[0, :, pl.ds(off, CUM_BLOCK)] = _dot_nt(pick, c2, precision=lax.Precision.HIGHEST)
        for h in range(N_HEADS):
            c1, cb, cc = _split3(jnp.broadcast_to(c2[:, h:h + 1], (CUM_BLOCK, LANES)))
            d = lane - _bias_lane_base(h)
            feat = jnp.where(d == N_BIAS, -c1, jnp.where(d == N_BIAS + 1, -cb, jnp.where(d == N_BIAS + 2, -cc,
                   jnp.where(d == 0, 1.0, jnp.where(d == 1, 1.0, jnp.where(d == 2, 1.0, 0.0))))))
            mine = (lane < HALF) if h % 2 == 0 else (lane >= HALF)
            kcol = kf_ref[0, pl.ds(off, CUM_BLOCK), (h // 2) * LANES:(h // 2 + 1) * LANES]
            kaug_ref[0, h, pl.ds(off, CUM_BLOCK), :] = jnp.where(mine, kcol, feat.astype(BF16))
        return cs[CUM_BLOCK - 1:CUM_BLOCK, :]

    lax.fori_loop(0, seq // CUM_BLOCK, body, jnp.zeros((1, LANES), F32))


def _cumgate(fg3, bf_pad, kf3):
    b, seq, _ = fg3.shape
    return pl.pallas_call(
        _cumgate_kernel,
        grid=(b,),
        in_specs=[pl.BlockSpec((1, seq, LANES), lambda i: (i, 0, 0)), pl.BlockSpec((1, LANES), lambda i: (0, 0)),
                  pl.BlockSpec((1, seq, WIDTH), lambda i: (i, 0, 0))],
        out_specs=[pl.BlockSpec((1, N_HEADS, seq), lambda i: (i, 0, 0)),
                   pl.BlockSpec((1, N_HEADS, seq, LANES), lambda i: (i, 0, 0, 0))],
        out_shape=[jax.ShapeDtypeStruct((b, N_HEADS, seq), F32),
                   jax.ShapeDtypeStruct((b, N_HEADS, seq, LANES), BF16)],
        compiler_params=pltpu.CompilerParams(dimension_semantics=("parallel",), vmem_limit_bytes=VMEM_LIMIT),
        name="cumgate",
    )(fg3, bf_pad, kf3)


def _compress_one(x_ref, pea_ref, peb_ref, wa_ref, wb_ref, b1_ref, w2_ref, b2_ref):
    xk = x_ref[0]
    n = xk.shape[0]
    wa = wa_ref[...]
    wb = wb_ref[...]
    first = _dot(xk, wa)
    second = _dot(xk, wb)
    pe_term = _dot(pea_ref[...].astype(BF16), wa) + _dot(peb_ref[...].astype(BF16), wb)
    hidden = first + pltpu.roll(second, n - 1, 0) + pe_term[0:1, :] + b1_ref[...]
    act = jax.nn.gelu(hidden)
    return _dot(act.astype(BF16), w2_ref[...]) + b2_ref[...]


def _compress_kernel(kc_ref, vc_ref, cos_ref, sin_ref,
                     kpea, kpeb, kwa, kwb, kb1, kw2, kb2,
                     vpea, vpeb, vwa, vwb, vb1, vw2, vb2,
                     kcmp_ref, vcmpt_ref):
    kcmp = _compress_one(kc_ref, kpea, kpeb, kwa, kwb, kb1, kw2, kb2)
    kcmp_ref[0] = _rope(kcmp, cos_ref[...], sin_ref[...]).astype(BF16)
    vcmpt_ref[0] = _compress_one(vc_ref, vpea, vpeb, vwa, vwb, vb1, vw2, vb2).T.astype(BF16)


def _compress(kc3, vc3, cos_c, sin_c, kparams, vparams):
    b, n, w = kc3.shape
    x_spec = pl.BlockSpec((1, n, w), lambda i: (i, 0, 0))
    full = lambda a: pl.BlockSpec(a.shape, lambda i: (0,) * a.ndim)
    params = list(kparams) + list(vparams)
    return pl.pallas_call(
        _compress_kernel,
        grid=(b,),
        in_specs=[x_spec, x_spec, full(cos_c), full(sin_c)] + [full(p) for p in params],
        out_specs=[pl.BlockSpec((1, n, KV_WIDTH), lambda i: (i, 0, 0)),
                   pl.BlockSpec((1, KV_WIDTH, n), lambda i: (i, 0, 0))],
        out_shape=[jax.ShapeDtypeStruct((b, n, KV_WIDTH), BF16), jax.ShapeDtypeStruct((b, KV_WIDTH, n), BF16)],
        compiler_params=pltpu.CompilerParams(dimension_semantics=("parallel",), vmem_limit_bytes=VMEM_LIMIT),
        name="compress",
    )(kc3, vc3, cos_c, sin_c, *params)


def _flash_step_t(s, vt, m_ref, l_ref, acc_ref, idx):
    row = slice(idx, idx + 1)
    m_old = m_ref[row, :]
    m_new = jnp.maximum(m_old, jnp.max(s, axis=0, keepdims=True))
    alpha = jnp.exp2(m_old - m_new)
    p = jnp.exp2(s - m_new)
    l_ref[row, :] = alpha * l_ref[row, :] + jnp.sum(p, axis=0, keepdims=True)
    m_ref[row, :] = m_new
    acc_ref[idx] = alpha * acc_ref[idx] + _dot(vt, p.astype(BF16))


def _flash_reset(m_ref, l_ref, acc_ref):
    m_ref[...] = jnp.full(m_ref.shape, NEG, F32)
    l_ref[...] = jnp.zeros(l_ref.shape, F32)
    acc_ref[...] = jnp.zeros(acc_ref.shape, F32)


FOX_TILE = 256
FOX_AHEAD = 4


def _fox_kernel(qt_ref, kaug_ref, vt_ref, crow_ref, o_ref, qaug_ref, m_ref, l_ref, acc_ref):
    t = FOX_TILE
    i = pl.program_id(1)
    diag = pl.multiple_of(i * t, t)
    kr = lax.broadcasted_iota(jnp.int32, (t, t), 0)
    qc = lax.broadcasted_iota(jnp.int32, (t, t), 1)
    causal = kr <= qc
    frow = lax.broadcasted_iota(jnp.int32, (BF16_ROWS, t), 0)
    pad = jnp.zeros((HALF - BF16_ROWS, t), BF16)

    _flash_reset(m_ref, l_ref, acc_ref)
    for h in range(N_HEADS):
        c1, c2, c3 = _split3(crow_ref[0, h:h + 1, :])
        feat = jnp.where(frow == 0, c1, jnp.where(frow == 1, c2, jnp.where(frow == 2, c3,
               jnp.where(frow < 2 * N_BIAS, 1.0, 0.0)))).astype(BF16)
        qh = qt_ref[h * HEAD_DIM:(h + 1) * HEAD_DIM, :]
        parts = [qh, feat, pad] if _bias_lane_base(h) == HALF else [feat, pad, qh]
        qaug_ref[h] = jnp.concatenate(parts, axis=0)

    def step(off, masked):
        def scores(h):
            return _dot(kaug_ref[0, h, pl.ds(off, t), :], qaug_ref[h])

        pending = {h: scores(h) for h in range(FOX_AHEAD)}
        for h in range(N_HEADS):
            if h + FOX_AHEAD < N_HEADS:
                pending[h + FOX_AHEAD] = scores(h + FOX_AHEAD)
            s = pending.pop(h)
            if masked:
                s = jnp.where(causal, s, MASK_FILL)
            _flash_step_t(s, vt_ref[h * HEAD_DIM:(h + 1) * HEAD_DIM, pl.ds(off, t)], m_ref, l_ref, acc_ref, h)

    def body(j, carry):
        step(pl.multiple_of(j * t, t), False)
        return carry

    lax.fori_loop(0, i, body, 0)
    step(diag, True)

    for col in range(WIDTH // LANES):
        pair = [acc_ref[h] * (1.0 / l_ref[h:h + 1, :]) for h in (2 * col, 2 * col + 1)]
        o_ref[0, :, col * LANES:(col + 1) * LANES] = jnp.concatenate(pair, axis=0).T.astype(BF16)


def _fox(qft, kaug, vft, crow):
    b, _, seq, _ = kaug.shape
    t = FOX_TILE
    nq = seq // t
    return pl.pallas_call(
        _fox_kernel,
        grid=(b, nq),
        in_specs=[pl.BlockSpec((WIDTH, t), lambda bi, i: (0, bi * nq + i)),
                  pl.BlockSpec((1, N_HEADS, seq, LANES), lambda bi, i: (bi, 0, 0, 0)),
                  pl.BlockSpec((WIDTH, seq), lambda bi, i: (0, bi)),
                  pl.BlockSpec((1, N_HEADS, t), lambda bi, i: (bi, 0, i))],
        out_specs=pl.BlockSpec((1, t, WIDTH), lambda bi, i: (bi, i, 0)),
        out_shape=jax.ShapeDtypeStruct((b, seq, WIDTH), BF16),
        scratch_shapes=[pltpu.VMEM((N_HEADS, LANES, t), BF16), pltpu.VMEM((N_HEADS, t), F32),
                        pltpu.VMEM((N_HEADS, t), F32), pltpu.VMEM((N_HEADS, HEAD_DIM, t), F32)],
        compiler_params=pltpu.CompilerParams(dimension_semantics=("parallel", "arbitrary"),
                                             vmem_limit_bytes=VMEM_LIMIT),
        name="fox",
    )(qft, kaug, vft, crow)


NSA_TILE = 128
STAT_ROWS = 8
NSA_UNROLL = 2


def _nsa_kernel(qt_ref, kcmp_ref, vcmpt_ref, ks_ref, vst_ref, kw_ref, vwt_ref, fgt_ref, bg_ref, ovl_ref, o_ref,
                qaug_ref, m_ref, l_ref, acc_ref, *, topk):
    t = NSA_TILE
    wide = GROUP * t
    n_cmp = kcmp_ref.shape[1]
    i = pl.program_id(1)
    start = i * t
    diag = pl.multiple_of(start, t)
    lane = lax.broadcasted_iota(jnp.int32, (1, LANES), 1)
    mine = [lane < HALF, lane >= HALF]
    zero_bf = jnp.zeros((), BF16)

    q4 = jnp.concatenate([qt_ref[n * LANES:(n + 1) * LANES, :] for n in range(GROUP)], axis=1)
    qpos4 = start + lax.broadcasted_iota(jnp.int32, (1, wide), 1) % t
    qpos = start + lax.broadcasted_iota(jnp.int32, (1, t), 1)

    cmp_end = lax.broadcasted_iota(jnp.int32, (n_cmp, 1), 0) * CMP_STRIDE + (CMP_LEN - 1)
    vis = cmp_end <= qpos4
    slot = lax.broadcasted_iota(jnp.int32, (SEL_SLOTS, t), 0)
    cur = qpos // SEL_LEN
    valid = slot <= cur
    forced = (slot == 0) | (slot == cur) | (slot == cur - 1)
    o_cmp = []
    for g in range(N_KV):
        s = jnp.where(vis, _dot(jnp.where(mine[g], kcmp_ref[0], zero_bf), q4), NEG)
        e = jnp.where(vis, jnp.exp2(s - jnp.max(s, axis=0, keepdims=True)), 0.0)
        denom = jnp.sum(e, axis=0, keepdims=True)
        p = e * jnp.where(denom > 0.0, 1.0 / denom, 0.0)
        o_cmp.append(_dot(vcmpt_ref[0, g * HEAD_DIM:(g + 1) * HEAD_DIM, :], p.astype(BF16)))
        psum = p[:, 0:t]
        for n in range(1, GROUP):
            psum = psum + p[:, n * t:(n + 1) * t]
        imp = _dot(ovl_ref[...], psum, precision=lax.Precision.HIGHEST)
        score = jnp.where(valid, imp + jnp.where(forced, FORCE_BONUS, 0.0), -1.0)
        rank = jnp.zeros((SEL_SLOTS, t), F32)
        for jp in range(SEL_SLOTS):
            other = score[jp:jp + 1, :]
            rank = rank + jnp.where(slot > jp, jnp.where(other >= score, 1.0, 0.0),
                                    jnp.where(other > score, 1.0, 0.0))
        selb = jnp.where(rank < topk, 0.0, MASK_FILL).astype(BF16)
        selb4 = jnp.concatenate([selb] * GROUP, axis=1)
        qaug_ref[g] = jnp.concatenate([q4[0:HALF], selb4] if g == 0 else [selb4, q4[HALF:LANES]], axis=0)

    kr = lax.broadcasted_iota(jnp.int32, (t, wide), 0)
    causal = kr <= lax.broadcasted_iota(jnp.int32, (t, wide), 1) % t
    key_row = lax.broadcasted_iota(jnp.int32, (t, LANES), 0)
    lane_full = lax.broadcasted_iota(jnp.int32, (t, LANES), 1)
    slot_lane = [lane_full - HALF, lane_full]

    def finish(g):
        return acc_ref[g] * (1.0 / l_ref[g:g + 1, :])

    def sweep(offs, score, vt_ref_, mask):
        chains = [(off, g) for off in offs for g in range(N_KV)]
        scores = [score(off, g) for off, g in chains]
        for (off, g), s in zip(chains, scores):
            if mask is not None:
                s = jnp.where(mask(off), s, MASK_FILL)
            _flash_step_t(s, vt_ref_[g * HEAD_DIM:(g + 1) * HEAD_DIM, pl.ds(off, t)], m_ref, l_ref, acc_ref, g)

    def tile_off(j):
        return pl.multiple_of(j * t, t)

    def sel_score(off, g):
        onehot = jnp.where((off + key_row) // SEL_LEN == slot_lane[g], 1.0, 0.0).astype(BF16)
        return _dot(jnp.where(mine[g], ks_ref[0, pl.ds(off, t), :], onehot), qaug_ref[g])

    def sel_body(jj, carry):
        sweep([tile_off(NSA_UNROLL * jj + u) for u in range(NSA_UNROLL)], sel_score, vst_ref, None)
        return carry

    def sel_tail(j, carry):
        sweep([tile_off(j)], sel_score, vst_ref, None)
        return carry

    _flash_reset(m_ref, l_ref, acc_ref)
    lax.fori_loop(0, i // NSA_UNROLL, sel_body, 0)
    lax.fori_loop((i // NSA_UNROLL) * NSA_UNROLL, i, sel_tail, 0)
    sweep([diag], sel_score, vst_ref, lambda off: causal)
    o_slc = [finish(g) for g in range(N_KV)]

    def win_score(off, g):
        return _dot(jnp.where(mine[g], kw_ref[0, pl.ds(off, t), :], zero_bf), q4)

    def win_body(j, carry):
        sweep([tile_off(j)], win_score, vwt_ref, lambda off: qpos4 - (off + kr) < WINDOW)
        return carry

    _flash_reset(m_ref, l_ref, acc_ref)
    lax.fori_loop(jnp.maximum(i - WINDOW // t, 0), i, win_body, 0)
    sweep([diag], win_score, vwt_ref, lambda off: causal)
    o_win = [finish(g) for g in range(N_KV)]

    gates = jax.nn.sigmoid(fgt_ref[...] + bg_ref[...])
    for n in range(GROUP):
        cols = slice(n * t, (n + 1) * t)
        mixed = []
        for g in range(N_KV):
            base = N_HEADS + (g * GROUP + n) * 3
            mixed.append(gates[base:base + 1, :] * o_cmp[g][:, cols]
                         + gates[base + 1:base + 2, :] * o_slc[g][:, cols]
                         + gates[base + 2:base + 3, :] * o_win[g][:, cols])
        o_ref[0, :, n * LANES:(n + 1) * LANES] = jnp.concatenate(mixed, axis=0).T.astype(BF16)


def _nsa(qnt, kcmp, vcmpt, ks, vst, kw, vwt, fgt, bg_t, ovl_t, topk):
    b, seq, _ = ks.shape
    t = NSA_TILE
    nq = seq // t
    n_cmp = kcmp.shape[1]
    whole = lambda bi, i: (bi, 0, 0)
    whole_t = lambda bi, i: (0, bi)
    tile_t = lambda bi, i: (0, bi * nq + i)
    fixed = lambda bi, i: (0, 0)
    return pl.pallas_call(
        functools.partial(_nsa_kernel, topk=topk),
        grid=(b, nq),
        in_specs=[pl.BlockSpec((WIDTH, t), tile_t),
                  pl.BlockSpec((1, n_cmp, KV_WIDTH), whole), pl.BlockSpec((1, KV_WIDTH, n_cmp), whole),
                  pl.BlockSpec((1, seq, KV_WIDTH), whole), pl.BlockSpec((KV_WIDTH, seq), whole_t),
                  pl.BlockSpec((1, seq, KV_WIDTH), whole), pl.BlockSpec((KV_WIDTH, seq), whole_t),
                  pl.BlockSpec((LANES, t), tile_t), pl.BlockSpec((LANES, t), fixed),
                  pl.BlockSpec(ovl_t.shape, fixed)],
        out_specs=pl.BlockSpec((1, t, WIDTH), lambda bi, i: (bi, i, 0)),
        out_shape=jax.ShapeDtypeStruct((b, seq, WIDTH), BF16),
        scratch_shapes=[pltpu.VMEM((N_KV, LANES, GROUP * t), BF16), pltpu.VMEM((STAT_ROWS, GROUP * t), F32),
                        pltpu.VMEM((STAT_ROWS, GROUP * t), F32), pltpu.VMEM((N_KV, HEAD_DIM, GROUP * t), F32)],
        compiler_params=pltpu.CompilerParams(dimension_semantics=("parallel", "arbitrary"),
                                             vmem_limit_bytes=VMEM_LIMIT),
        name="nsa",
    )(qnt, kcmp, vcmpt, ks, vst, kw, vwt, fgt, bg_t, ovl_t)


def _post_kernel(x_ref, of_ref, on_ref, gf_ref, gn_ref, wo_ref, gm_ref, wu_ref, wd_ref, gl_ref, o_ref):
    yf = _rms(of_ref[...].astype(F32), gf_ref[...]).astype(BF16)
    yn = _rms(on_ref[...].astype(F32), gn_ref[...]).astype(BF16)
    h1 = x_ref[...] + _dot(yf, wo_ref[0:WIDTH, :]) + _dot(yn, wo_ref[WIDTH:2 * WIDTH, :])
    u = _dot(_rms(h1, gm_ref[...]).astype(BF16), wu_ref[...])
    act = jnp.square(jnp.maximum(u, 0.0)).astype(BF16)
    h2 = h1 + _dot(act, wd_ref[...])
    o_ref[...] = _rms(h2, gl_ref[...])


def _post(x2, ofox, onsa, g_fox, g_nsa, w_out, g_mlp, w_up, w_down, g_final, tm):
    m, d = x2.shape
    row = lambda i: (i, 0)
    fixed = lambda i: (0, 0)
    full = lambda a: pl.BlockSpec(a.shape, fixed, pipeline_mode=pl.Buffered(1))
    return pl.pallas_call(
        _post_kernel,
        grid=(m // tm,),
        in_specs=[pl.BlockSpec((tm, d), row), pl.BlockSpec((tm, WIDTH), row), pl.BlockSpec((tm, WIDTH), row),
                  full(g_fox), full(g_nsa), full(w_out), full(g_mlp), full(w_up), full(w_down), full(g_final)],
        out_specs=pl.BlockSpec((tm, d), row),
        out_shape=jax.ShapeDtypeStruct((m, d), F32),
        compiler_params=pltpu.CompilerParams(dimension_semantics=("parallel",), vmem_limit_bytes=VMEM_LIMIT),
        name="post",
    )(x2, ofox, onsa, g_fox, g_nsa, w_out, g_mlp, w_up, w_down, g_final)


_NSA_PERM = np.array([(p % 2) * GROUP + p // 2 for p in range(N_HEADS)])


def _perm_heads(a, axis):
    shape = a.shape
    a = a.reshape(shape[:axis] + (N_HEADS, HEAD_DIM) + shape[axis + 1:])
    a = jnp.take(a, _NSA_PERM, axis=axis)
    return a.reshape(shape)


def _prep_w_in(w):
    qscale = HEAD_DIM ** -0.5 * LOG2E
    sizes = [WIDTH, WIDTH, WIDTH, N_HEADS, WIDTH] + [KV_WIDTH] * 6 + [3 * N_HEADS]
    offs = np.cumsum([0] + sizes)
    qf, kf, vf, fl, qn, kc, vc, ks, vs, kw, vw, gate = [w[:, offs[k]:offs[k + 1]] for k in range(len(sizes))]
    fg = jnp.concatenate([fl, gate, jnp.zeros((w.shape[0], LANES - 4 * N_HEADS), w.dtype)], axis=1)
    w_nat = jnp.concatenate([kf, kc, vc, ks, kw, fg], axis=1).astype(BF16)
    w_tr = jnp.concatenate([qf * qscale, vf, _perm_heads(qn * qscale, 1), vs, vw, fg], axis=1).T.astype(BF16)
    assert w_nat.shape[1] == NAT_COLS and w_tr.shape[0] == TR_ROWS
    return w_nat, w_tr


def _prep_compress(pe, w1, b1, w2, b2):
    half = CMP_LEN // 2
    eye = jnp.eye(N_KV, dtype=F32)

    def expand_w1(wpart):
        w3 = wpart.reshape(half, HEAD_DIM, CMP_HIDDEN)
        return jnp.einsum('ldh,gk->lgdkh', w3, eye).reshape(half * KV_WIDTH, N_KV * CMP_HIDDEN).astype(BF16)

    def expand_pe(ppart):
        flat = jnp.broadcast_to(ppart[:, None, :], (half, N_KV, HEAD_DIM)).reshape(1, half * KV_WIDTH)
        return jnp.broadcast_to(flat, (8, half * KV_WIDTH))

    w2b = jnp.einsum('hd,gk->ghkd', w2, eye).reshape(N_KV * CMP_HIDDEN, KV_WIDTH).astype(BF16)
    return (expand_pe(pe[:half]), expand_pe(pe[half:]),
            expand_w1(w1[:half * HEAD_DIM]), expand_w1(w1[half * HEAD_DIM:]),
            jnp.tile(b1, N_KV)[None, :], w2b, jnp.tile(b2, N_KV)[None, :])


def _rope_tables(pos, reps):
    half = HEAD_DIM // 2
    inv = ROPE_THETA ** (-jnp.arange(half, dtype=F32) / half)
    ang = pos.astype(F32)[:, None] * inv[None, :]
    cos = jnp.cos(ang)
    sin = jnp.sin(ang)
    return (jnp.tile(jnp.concatenate([cos, cos], axis=1), (1, reps)),
            jnp.tile(jnp.concatenate([-sin, sin], axis=1), (1, reps)))


def _overlap_t(n_cmp_slots, n_cmp, n_blocks):
    c = np.arange(n_cmp_slots)[None, :] * CMP_STRIDE
    j = np.arange(SEL_SLOTS)[:, None]
    s = j * SEL_LEN
    ovl = (c < s + SEL_LEN) & (c + CMP_LEN > s) & (np.arange(n_cmp_slots)[None, :] < n_cmp) & (j < n_blocks)
    return jnp.asarray(ovl.astype(np.float32))


def _row_tile(m, want):
    t = want
    while m % t:
        t //= 2
    return t


def _layer(h, seq, g_attn, w_in, b_f, b_gate, cmpk, cmpv, g_fox, g_nsa, w_out, g_mlp, w_up, w_down, g_out):
    m, d = h.shape
    b = m // seq
    n_chunks = seq // CMP_STRIDE
    n_cmp = (seq - CMP_LEN) // CMP_STRIDE + 1
    n_blocks = seq // SEL_LEN
    assert seq % FOX_TILE == 0 and n_chunks % LANES == 0 and SEL_TOPK <= n_blocks <= SEL_SLOTS
    assert n_cmp == n_chunks - 1

    cos_n, sin_n = _rope_tables(jnp.arange(seq), N_KV)
    cos_c, sin_c = _rope_tables(jnp.arange(n_chunks) * CMP_STRIDE + CMP_LEN - 1, N_KV)
    cos_t = jnp.tile(cos_n, (1, GROUP)).T
    sin_t = jnp.tile(sin_n, (1, GROUP)).T

    tm = _row_tile(seq, 512)
    w_nat, w_tr = _prep_w_in(w_in)
    kf, kc, vc, ks, kw, fg, qft, vft, qnt, vst, vwt, fgt = _inproj(
        h, g_attn[None, :], w_nat, w_tr, cos_n, sin_n, cos_t, sin_t, seq, tm)
    r3 = lambda a: a.reshape(b, seq, a.shape[-1])

    bf_pad = jnp.zeros((1, LANES), F32).at[0, :N_HEADS].set(b_f)
    bg_col = jnp.zeros((LANES,), F32).at[N_HEADS:4 * N_HEADS].set(b_gate)
    bg_t = jnp.broadcast_to(bg_col[:, None], (LANES, NSA_TILE))
    crow, kaug = _cumgate(r3(fg), bf_pad, r3(kf))

    chunked = lambda a: a.reshape(b, n_chunks, CMP_STRIDE * KV_WIDTH)
    kcmp, vcmpt = _compress(chunked(kc), chunked(vc), cos_c, sin_c, _prep_compress(*cmpk), _prep_compress(*cmpv))

    ofox = _fox(qft, kaug, vft, crow)
    onsa = _nsa(qnt, kcmp, vcmpt, r3(ks), vst, r3(kw), vwt, fgt, bg_t,
                _overlap_t(n_chunks, n_cmp, n_blocks), min(SEL_TOPK, n_blocks))

    w_out_p = jnp.concatenate([w_out[:WIDTH], _perm_heads(w_out[WIDTH:], 0)], axis=0).astype(BF16)
    return _post(h, ofox.reshape(m, WIDTH), onsa.reshape(m, WIDTH), g_fox[None, :], _perm_heads(g_nsa, 0)[None, :],
                 w_out_p, g_mlp[None, :], w_up.astype(BF16), w_down.astype(BF16), g_out[None, :],
                 _row_tile(seq, 256))


def kernel(x, g_attn, w_in, b_f, b_gate, cmpk_pe, cmpk_w1, cmpk_b1, cmpk_w2, cmpk_b2, cmpv_pe, cmpv_w1, cmpv_b1,
           cmpv_w2, cmpv_b2, g_fox, g_nsa, w_out, g_mlp, w_up, w_down, g_final):
    b, seq, d = x.shape
    depth = g_attn.shape[0]
    assert depth == 1, "the final rmsnorm is fused into the (single) layer's last kernel"
    h = x.reshape(b * seq, d)
    out = _layer(h, seq, g_attn[0], w_in[0], b_f[0], b_gate[0],
                 (cmpk_pe[0], cmpk_w1[0], cmpk_b1[0], cmpk_w2[0], cmpk_b2[0]),
                 (cmpv_pe[0], cmpv_w1[0], cmpv_b1[0], cmpv_w2[0], cmpv_b2[0]),
                 g_fox[0], g_nsa[0], w_out[0], g_mlp[0], w_up[0], w_down[0], g_final)
    return out.reshape(b, seq, d)
```

```python
import functools

import numpy as np
import jax
import jax.numpy as jnp
from jax import lax
from jax.experimental import pallas as pl
from jax.experimental.pallas import tpu as pltpu

F32 = jnp.float32
BF16 = jnp.bfloat16

HEAD_DIM = 64
N_HEADS = 8
N_KV = 2
GROUP = N_HEADS // N_KV
WIDTH = N_HEADS * HEAD_DIM
KV_WIDTH = N_KV * HEAD_DIM
CMP_LEN = 32
CMP_STRIDE = 16
CMP_HIDDEN = 256
SEL_LEN = 64
SEL_TOPK = 16
SEL_SLOTS = 64
WINDOW = 512
ROPE_THETA = 10000.0
EPS = 1e-6
NEG = -1e30
MASK_FILL = -(2.0 ** 100)
FORCE_BONUS = 1e4
LOG2E = 1.4426950408889634
LANES = 128
HALF = LANES // 2
BF16_ROWS = 16

VMEM_LIMIT = 56 * 1024 * 1024


def _dot(a, b, precision=None):
    return jnp.dot(a, b, preferred_element_type=F32, precision=precision)


def _dot_nt(a, b, precision=None):
    return lax.dot_general(a, b, (((1,), (1,)), ((), ())), preferred_element_type=F32, precision=precision)


def _rms(x, g):
    return x * lax.rsqrt(jnp.mean(x * x, axis=-1, keepdims=True) + EPS) * g


def _rope(x, cos, sin_signed):
    w = x.shape[-1]
    lane = lax.broadcasted_iota(jnp.int32, (1, w), 1)
    first = (lane % HEAD_DIM) < (HEAD_DIM // 2)
    partner = jnp.where(first, pltpu.roll(x, w - HEAD_DIM // 2, 1), pltpu.roll(x, HEAD_DIM // 2, 1))
    return x * cos + partner * sin_signed


def _rope_t(x, cos, sin_signed):
    r = x.shape[0]
    row = lax.broadcasted_iota(jnp.int32, (r, 1), 0)
    first = (row % HEAD_DIM) < (HEAD_DIM // 2)
    partner = jnp.where(first, pltpu.roll(x, r - HEAD_DIM // 2, 0), pltpu.roll(x, HEAD_DIM // 2, 0))
    return x * cos + partner * sin_signed


def _split3(c):
    c1 = c.astype(BF16).astype(F32)
    r1 = c - c1
    c2 = r1.astype(BF16).astype(F32)
    return c1, c2, r1 - c2


NAT_KF, NAT_KC, NAT_VC, NAT_KS, NAT_KW, NAT_FG, NAT_COLS = 0, 512, 640, 768, 896, 1024, 1152
TR_QF, TR_VF, TR_QN, TR_VS, TR_VW, TR_FG, TR_ROWS = 0, 512, 1024, 1536, 1664, 1792, 1920


def _inproj_kernel(x_ref, g_ref, wn_ref, wt_ref, cos_ref, sin_ref, cost_ref, sint_ref,
                   kf_ref, kc_ref, vc_ref, ks_ref, kw_ref, fg_ref,
                   qft_ref, vft_ref, qnt_ref, vst_ref, vwt_ref, fgt_ref):
    hb = _rms(x_ref[...], g_ref[...]).astype(BF16)

    def nat(lo, width):
        return _dot(hb, wn_ref[:, lo:lo + width])

    def tr(lo, rows):
        return _dot_nt(wt_ref[lo:lo + rows, :], hb)

    cos = cos_ref[...]
    sin = sin_ref[...]
    kf_ref[...] = nat(NAT_KF, WIDTH).astype(BF16)
    kc_ref[...] = nat(NAT_KC, KV_WIDTH).astype(BF16)
    vc_ref[...] = nat(NAT_VC, KV_WIDTH).astype(BF16)
    ks_ref[...] = _rope(nat(NAT_KS, KV_WIDTH), cos, sin).astype(BF16)
    kw_ref[...] = _rope(nat(NAT_KW, KV_WIDTH), cos, sin).astype(BF16)
    fg_ref[...] = nat(NAT_FG, LANES)
    qft_ref[...] = tr(TR_QF, WIDTH).astype(BF16)
    vft_ref[...] = tr(TR_VF, WIDTH).astype(BF16)
    qnt_ref[...] = _rope_t(tr(TR_QN, WIDTH), cost_ref[...], sint_ref[...]).astype(BF16)
    vst_ref[...] = tr(TR_VS, KV_WIDTH).astype(BF16)
    vwt_ref[...] = tr(TR_VW, KV_WIDTH).astype(BF16)
    fgt_ref[...] = tr(TR_FG, LANES)


def _inproj(x2, g_attn, w_nat, w_tr, cos_n, sin_n, cos_t, sin_t, seq, tm):
    m, d = x2.shape
    sblocks = seq // tm
    row = lambda i: (i, 0)
    col = lambda i: (0, i)
    fixed = lambda i: (0, 0)
    tab = lambda i: (i % sblocks, 0)
    tab_t = lambda i: (0, i % sblocks)
    sds = jax.ShapeDtypeStruct
    out_shape = [sds((m, WIDTH), BF16)] + [sds((m, KV_WIDTH), BF16)] * 4 + [sds((m, LANES), F32),
                 sds((WIDTH, m), BF16), sds((WIDTH, m), BF16), sds((WIDTH, m), BF16),
                 sds((KV_WIDTH, m), BF16), sds((KV_WIDTH, m), BF16), sds((LANES, m), F32)]
    out_specs = ([pl.BlockSpec((tm, WIDTH), row)] + [pl.BlockSpec((tm, KV_WIDTH), row)] * 4
                 + [pl.BlockSpec((tm, LANES), row)]
                 + [pl.BlockSpec((WIDTH, tm), col)] * 3 + [pl.BlockSpec((KV_WIDTH, tm), col)] * 2
                 + [pl.BlockSpec((LANES, tm), col)])
    return pl.pallas_call(
        _inproj_kernel,
        grid=(m // tm,),
        in_specs=[pl.BlockSpec((tm, d), row), pl.BlockSpec((1, d), fixed),
                  pl.BlockSpec(w_nat.shape, fixed), pl.BlockSpec(w_tr.shape, fixed),
                  pl.BlockSpec((tm, KV_WIDTH), tab), pl.BlockSpec((tm, KV_WIDTH), tab),
                  pl.BlockSpec((WIDTH, tm), tab_t), pl.BlockSpec((WIDTH, tm), tab_t)],
        out_specs=out_specs,
        out_shape=out_shape,
        compiler_params=pltpu.CompilerParams(dimension_semantics=("parallel",), vmem_limit_bytes=VMEM_LIMIT),
        name="inproj",
    )(x2, g_attn, w_nat, w_tr, cos_n, sin_n, cos_t, sin_t)


CUM_BLOCK = 256
N_BIAS = 3


def _bias_lane_base(h):
    return HALF if h % 2 == 0 else 0


def _cumgate_kernel(fg_ref, bf_ref, kf_ref, crow_ref, kaug_ref):
    seq = fg_ref.shape[1]
    r = lax.broadcasted_iota(jnp.int32, (CUM_BLOCK, CUM_BLOCK), 0)
    c = lax.broadcasted_iota(jnp.int32, (CUM_BLOCK, CUM_BLOCK), 1)
    tri = jnp.where(r >= c, 1.0, 0.0).astype(F32)
    er = lax.broadcasted_iota(jnp.int32, (N_HEADS, LANES), 0)
    ec = lax.broadcasted_iota(jnp.int32, (N_HEADS, LANES), 1)
    pick = jnp.where(er == ec, 1.0, 0.0).astype(F32)
    bias = bf_ref[...]
    lane = lax.broadcasted_iota(jnp.int32, (CUM_BLOCK, LANES), 1)

    def body(blk, carry):
        off = pl.multiple_of(blk * CUM_BLOCK, CUM_BLOCK)
        z = fg_ref[0, pl.ds(off, CUM_BLOCK), :] + bias
        logf = jnp.minimum(z, 0.0) - jnp.log(1.0 + jnp.exp(-jnp.abs(z)))
        cs = _dot(tri, logf, precision=lax.Precision.HIGHEST) + carry
        c2 = cs * LOG2E
        crow_ref[0, :, pl.ds(off, CUM_BLOCK)] = _dot_nt(pick, c2, precision=lax.Precision.HIGHEST)
        for h in range(N_HEADS):
            c1, cb, cc = _split3(jnp.broadcast_to(c2[:, h:h + 1], (CUM_BLOCK, LANES)))
            d = lane - _bias_lane_base(h)
            feat = jnp.where(d == N_BIAS, -c1, jnp.where(d == N_BIAS + 1, -cb, jnp.where(d == N_BIAS + 2, -cc,
                   jnp.where(d == 0, 1.0, jnp.where(d == 1, 1.0, jnp.where(d == 2, 1.0, 0.0))))))
            mine = (lane < HALF) if h % 2 == 0 else (lane >= HALF)
            kcol = kf_ref[0, pl.ds(off, CUM_BLOCK), (h // 2) * LANES:(h // 2 + 1) * LANES]
            kaug_ref[0, h, pl.ds(off, CUM_BLOCK), :] = jnp.where(mine, kcol, feat.astype(BF16))
        return cs[CUM_BLOCK - 1:CUM_BLOCK, :]

    lax.fori_loop(0, seq // CUM_BLOCK, body, jnp.zeros((1, LANES), F32))


def _cumgate(fg3, bf_pad, kf3):
    b, seq, _ = fg3.shape
    return pl.pallas_call(
        _cumgate_kernel,
        grid=(b,),
        in_specs=[pl.BlockSpec((1, seq, LANES), lambda i: (i, 0, 0)), pl.BlockSpec((1, LANES), lambda i: (0, 0)),
                  pl.BlockSpec((1, seq, WIDTH), lambda i: (i, 0, 0))],
        out_specs=[pl.BlockSpec((1, N_HEADS, seq), lambda i: (i, 0, 0)),
                   pl.BlockSpec((1, N_HEADS, seq, LANES), lambda i: (i, 0, 0, 0))],
        out_shape=[jax.ShapeDtypeStruct((b, N_HEADS, seq), F32),
                   jax.ShapeDtypeStruct((b, N_HEADS, seq, LANES), BF16)],
        compiler_params=pltpu.CompilerParams(dimension_semantics=("parallel",), vmem_limit_bytes=VMEM_LIMIT),
        name="cumgate",
    )(fg3, bf_pad, kf3)


def _compress_one(x_ref, pea_ref, peb_ref, wa_ref, wb_ref, b1_ref, w2_ref, b2_ref):
    xk = x_ref[0]
    n = xk.shape[0]
    wa = wa_ref[...]
    wb = wb_ref[...]
    first = _dot(xk, wa)
    second = _dot(xk, wb)
    pe_term = _dot(pea_ref[...].astype(BF16), wa) + _dot(peb_ref[...].astype(BF16), wb)
    hidden = first + pltpu.roll(second, n - 1, 0) + pe_term[0:1, :] + b1_ref[...]
    act = jax.nn.gelu(hidden)
    return _dot(act.astype(BF16), w2_ref[...]) + b2_ref[...]


def _compress_kernel(kc_ref, vc_ref, cos_ref, sin_ref,
                     kpea, kpeb, kwa, kwb, kb1, kw2, kb2,
                     vpea, vpeb, vwa, vwb, vb1, vw2, vb2,
                     kcmp_ref, vcmpt_ref):
    kcmp = _compress_one(kc_ref, kpea, kpeb, kwa, kwb, kb1, kw2, kb2)
    kcmp_ref[0] = _rope(kcmp, cos_ref[...], sin_ref[...]).astype(BF16)
    vcmpt_ref[0] = _compress_one(vc_ref, vpea, vpeb, vwa, vwb, vb1, vw2, vb2).T.astype(BF16)


def _compress(kc3, vc3, cos_c, sin_c, kparams, vparams):
    b, n, w = kc3.shape
    x_spec = pl.BlockSpec((1, n, w), lambda i: (i, 0, 0))
    full = lambda a: pl.BlockSpec(a.shape, lambda i: (0,) * a.ndim)
    params = list(kparams) + list(vparams)
    return pl.pallas_call(
        _compress_kernel,
        grid=(b,),
        in_specs=[x_spec, x_spec, full(cos_c), full(sin_c)] + [full(p) for p in params],
        out_specs=[pl.BlockSpec((1, n, KV_WIDTH), lambda i: (i, 0, 0)),
                   pl.BlockSpec((1, KV_WIDTH, n), lambda i: (i, 0, 0))],
        out_shape=[jax.ShapeDtypeStruct((b, n, KV_WIDTH), BF16), jax.ShapeDtypeStruct((b, KV_WIDTH, n), BF16)],
        compiler_params=pltpu.CompilerParams(dimension_semantics=("parallel",), vmem_limit_bytes=VMEM_LIMIT),
        name="compress",
    )(kc3, vc3, cos_c, sin_c, *params)


ACC_ROWS = HEAD_DIM + BF16_ROWS


def _flash_step_t(s, vt, m_ref, acc_ref, idx):
    row = slice(idx, idx + 1)
    m_old = m_ref[row, :]
    m_new = jnp.maximum(m_old, jnp.max(s, axis=0, keepdims=True))
    alpha = jnp.exp2(m_old - m_new)
    p = jnp.exp2(s - m_new).astype(BF16)
    m_ref[row, :] = m_new
    vt_ones = jnp.concatenate([vt, jnp.ones((BF16_ROWS, vt.shape[1]), BF16)], axis=0)
    acc_ref[idx] = alpha * acc_ref[idx] + _dot(vt_ones, p)


def _flash_reset(m_ref, acc_ref):
    m_ref[...] = jnp.full(m_ref.shape, NEG, F32)
    acc_ref[...] = jnp.zeros(acc_ref.shape, F32)


def _flash_finish(acc_ref, idx):
    acc = acc_ref[idx]
    return acc[0:HEAD_DIM] * (1.0 / acc[HEAD_DIM:HEAD_DIM + 1])


FOX_TILE = 256
FOX_AHEAD = 4
FOX_UNROLL = 2


def _fox_kernel(qt_ref, kaug_ref, vt_ref, crow_ref, o_ref, qaug_ref, m_ref, acc_ref):
    t = FOX_TILE
    i = pl.program_id(1)
    diag = pl.multiple_of(i * t, t)
    kr = lax.broadcasted_iota(jnp.int32, (t, t), 0)
    qc = lax.broadcasted_iota(jnp.int32, (t, t), 1)
    causal = kr <= qc
    frow = lax.broadcasted_iota(jnp.int32, (BF16_ROWS, t), 0)
    pad = jnp.zeros((HALF - BF16_ROWS, t), BF16)

    _flash_reset(m_ref, acc_ref)
    for h in range(N_HEADS):
        c1, c2, c3 = _split3(crow_ref[0, h:h + 1, :])
        feat = jnp.where(frow == 0, c1, jnp.where(frow == 1, c2, jnp.where(frow == 2, c3,
               jnp.where(frow < 2 * N_BIAS, 1.0, 0.0)))).astype(BF16)
        qh = qt_ref[h * HEAD_DIM:(h + 1) * HEAD_DIM, :]
        parts = [qh, feat, pad] if _bias_lane_base(h) == HALF else [feat, pad, qh]
        qaug_ref[h] = jnp.concatenate(parts, axis=0)

    def step(offs, masked):
        chains = [(off, h) for off in offs for h in range(N_HEADS)]

        def scores(c):
            off, h = chains[c]
            return _dot(kaug_ref[0, h, pl.ds(off, t), :], qaug_ref[h])

        pending = {c: scores(c) for c in range(FOX_AHEAD)}
        for c, (off, h) in enumerate(chains):
            if c + FOX_AHEAD < len(chains):
                pending[c + FOX_AHEAD] = scores(c + FOX_AHEAD)
            s = pending.pop(c)
            if masked:
                s = jnp.where(causal, s, MASK_FILL)
            _flash_step_t(s, vt_ref[h * HEAD_DIM:(h + 1) * HEAD_DIM, pl.ds(off, t)], m_ref, acc_ref, h)

    def tile_off(j):
        return pl.multiple_of(j * t, t)

    def body(jj, carry):
        step([tile_off(FOX_UNROLL * jj + u) for u in range(FOX_UNROLL)], False)
        return carry

    def tail(j, carry):
        step([tile_off(j)], False)
        return carry

    lax.fori_loop(0, i // FOX_UNROLL, body, 0)
    lax.fori_loop((i // FOX_UNROLL) * FOX_UNROLL, i, tail, 0)
    step([diag], True)

    for col in range(WIDTH // LANES):
        pair = [_flash_finish(acc_ref, h) for h in (2 * col, 2 * col + 1)]
        o_ref[0, :, col * LANES:(col + 1) * LANES] = jnp.concatenate(pair, axis=0).T.astype(BF16)


def _fox(qft, kaug, vft, crow):
    b, _, seq, _ = kaug.shape
    t = FOX_TILE
    nq = seq // t
    return pl.pallas_call(
        _fox_kernel,
        grid=(b, nq),
        in_specs=[pl.BlockSpec((WIDTH, t), lambda bi, i: (0, bi * nq + i)),
                  pl.BlockSpec((1, N_HEADS, seq, LANES), lambda bi, i: (bi, 0, 0, 0)),
                  pl.BlockSpec((WIDTH, seq), lambda bi, i: (0, bi)),
                  pl.BlockSpec((1, N_HEADS, t), lambda bi, i: (bi, 0, i))],
        out_specs=pl.BlockSpec((1, t, WIDTH), lambda bi, i: (bi, i, 0)),
        out_shape=jax.ShapeDtypeStruct((b, seq, WIDTH), BF16),
        scratch_shapes=[pltpu.VMEM((N_HEADS, LANES, t), BF16), pltpu.VMEM((N_HEADS, t), F32),
                        pltpu.VMEM((N_HEADS, ACC_ROWS, t), F32)],
        compiler_params=pltpu.CompilerParams(dimension_semantics=("parallel", "arbitrary"),
                                             vmem_limit_bytes=VMEM_LIMIT),
        name="fox",
    )(qft, kaug, vft, crow)


NSA_TILE = 128
STAT_ROWS = 8
NSA_UNROLL = 2


def _nsa_kernel(qt_ref, kcmp_ref, vcmpt_ref, ks_ref, vst_ref, kw_ref, vwt_ref, fgt_ref, bg_ref, ovl_ref, o_ref,
                qaug_ref, m_ref, acc_ref, *, topk):
    t = NSA_TILE
    wide = GROUP * t
    n_cmp = kcmp_ref.shape[1]
    i = pl.program_id(1)
    start = i * t
    diag = pl.multiple_of(start, t)
    lane = lax.broadcasted_iota(jnp.int32, (1, LANES), 1)
    mine = [lane < HALF, lane >= HALF]
    zero_bf = jnp.zeros((), BF16)

    q4 = jnp.concatenate([qt_ref[n * LANES:(n + 1) * LANES, :] for n in range(GROUP)], axis=1)
    qpos4 = start + lax.broadcasted_iota(jnp.int32, (1, wide), 1) % t
    qpos = start + lax.broadcasted_iota(jnp.int32, (1, t), 1)

    cmp_end = lax.broadcasted_iota(jnp.int32, (n_cmp, 1), 0) * CMP_STRIDE + (CMP_LEN - 1)
    vis = cmp_end <= qpos4
    slot = lax.broadcasted_iota(jnp.int32, (SEL_SLOTS, t), 0)
    cur = qpos // SEL_LEN
    valid = slot <= cur
    forced = (slot == 0) | (slot == cur) | (slot == cur - 1)
    sub8 = lax.broadcasted_iota(jnp.int32, (STAT_ROWS, t), 0)
    o_cmp = []
    for g in range(N_KV):
        s = jnp.where(vis, _dot(jnp.where(mine[g], kcmp_ref[0], zero_bf), q4), NEG)
        e = jnp.where(vis, jnp.exp2(s - jnp.max(s, axis=0, keepdims=True)), 0.0)
        denom = jnp.sum(e, axis=0, keepdims=True)
        p = e * jnp.where(denom > 0.0, 1.0 / denom, 0.0)
        o_cmp.append(_dot(vcmpt_ref[0, g * HEAD_DIM:(g + 1) * HEAD_DIM, :], p.astype(BF16)))
        psum = p[:, 0:t]
        for n in range(1, GROUP):
            psum = psum + p[:, n * t:(n + 1) * t]
        imp = _dot(ovl_ref[...], psum, precision=lax.Precision.HIGHEST)
        score = jnp.where(valid, imp + jnp.where(forced, FORCE_BONUS, 0.0), -1.0)
        tiles = [score[r:r + STAT_ROWS, :] for r in range(0, SEL_SLOTS, STAT_ROWS)]
        ranks = [jnp.zeros((STAT_ROWS, t), F32) for _ in tiles]
        for jp in range(SEL_SLOTS):
            other = jnp.broadcast_to(score[jp:jp + 1, :], (STAT_ROWS, t))
            for k, tile in enumerate(tiles):
                first = k * STAT_ROWS
                ge = jnp.where(other >= tile, 1.0, 0.0)
                gt = jnp.where(other > tile, 1.0, 0.0)
                if first > jp:
                    ahead = ge
                elif first + STAT_ROWS - 1 <= jp:
                    ahead = gt
                else:
                    ahead = jnp.where(sub8 + first > jp, ge, gt)
                ranks[k] = ranks[k] + ahead
        rank = jnp.concatenate(ranks, axis=0)
        selb = jnp.where(rank < topk, 0.0, MASK_FILL).astype(BF16)
        selb4 = jnp.concatenate([selb] * GROUP, axis=1)
        qaug_ref[g] = jnp.concatenate([q4[0:HALF], selb4] if g == 0 else [selb4, q4[HALF:LANES]], axis=0)

    kr = lax.broadcasted_iota(jnp.int32, (t, wide), 0)
    causal = kr <= lax.broadcasted_iota(jnp.int32, (t, wide), 1) % t
    key_row = lax.broadcasted_iota(jnp.int32, (t, LANES), 0)
    lane_full = lax.broadcasted_iota(jnp.int32, (t, LANES), 1)
    slot_lane = [lane_full - HALF, lane_full]

    def sweep(tiles, score, vt_ref_):
        chains = [(off, mask, g) for off, mask in tiles for g in range(N_KV)]
        scores = [score(off, g) for off, _, g in chains]
        for (off, mask, g), s in zip(chains, scores):
            if mask is not None:
                s = jnp.where(mask, s, MASK_FILL)
            _flash_step_t(s, vt_ref_[g * HEAD_DIM:(g + 1) * HEAD_DIM, pl.ds(off, t)], m_ref, acc_ref, g)

    def tile_off(j):
        return pl.multiple_of(j * t, t)

    def sel_score(off, g):
        onehot = jnp.where((off + key_row) // SEL_LEN == slot_lane[g], 1.0, 0.0).astype(BF16)
        return _dot(jnp.where(mine[g], ks_ref[0, pl.ds(off, t), :], onehot), qaug_ref[g])

    def sel_body(jj, carry):
        sweep([(tile_off(NSA_UNROLL * jj + u), None) for u in range(NSA_UNROLL)], sel_score, vst_ref)
        return carry

    def sel_tail(j, carry):
        sweep([(tile_off(j), None)], sel_score, vst_ref)
        return carry

    _flash_reset(m_ref, acc_ref)
    lax.fori_loop(0, i // NSA_UNROLL, sel_body, 0)
    lax.fori_loop((i // NSA_UNROLL) * NSA_UNROLL, i, sel_tail, 0)
    sweep([(diag, causal)], sel_score, vst_ref)
    o_slc = [_flash_finish(acc_ref, g) for g in range(N_KV)]

    def win_score(off, g):
        return _dot(jnp.where(mine[g], kw_ref[0, pl.ds(off, t), :], zero_bf), q4)

    n_back = WINDOW // t
    _flash_reset(m_ref, acc_ref)

    @pl.when(i >= n_back)
    def _():
        first = tile_off(i - n_back)
        in_band = qpos4 - (first + kr) < WINDOW
        sweep([(first, in_band)] + [(tile_off(i - n_back + u), None) for u in range(1, n_back)] + [(diag, causal)],
              win_score, vwt_ref)

    @pl.when(i < n_back)
    def _():
        def win_body(j, carry):
            sweep([(tile_off(j), None)], win_score, vwt_ref)
            return carry

        lax.fori_loop(0, i, win_body, 0)
        sweep([(diag, causal)], win_score, vwt_ref)

    o_win = [_flash_finish(acc_ref, g) for g in range(N_KV)]

    gates = jax.nn.sigmoid(fgt_ref[...] + bg_ref[...])
    for n in range(GROUP):
        cols = slice(n * t, (n + 1) * t)
        mixed = []
        for g in range(N_KV):
            base = N_HEADS + (g * GROUP + n) * 3
            mixed.append(gates[base:base + 1, :] * o_cmp[g][:, cols]
                         + gates[base + 1:base + 2, :] * o_slc[g][:, cols]
                         + gates[base + 2:base + 3, :] * o_win[g][:, cols])
        o_ref[0, :, n * LANES:(n + 1) * LANES] = jnp.concatenate(mixed, axis=0).T.astype(BF16)


def _nsa(qnt, kcmp, vcmpt, ks, vst, kw, vwt, fgt, bg_t, ovl_t, topk):
    b, seq, _ = ks.shape
    t = NSA_TILE
    nq = seq // t
    n_cmp = kcmp.shape[1]
    whole = lambda bi, i: (bi, 0, 0)
    whole_t = lambda bi, i: (0, bi)
    tile_t = lambda bi, i: (0, bi * nq + i)
    fixed = lambda bi, i: (0, 0)
    return pl.pallas_call(
        functools.partial(_nsa_kernel, topk=topk),
        grid=(b, nq),
        in_specs=[pl.BlockSpec((WIDTH, t), tile_t),
                  pl.BlockSpec((1, n_cmp, KV_WIDTH), whole), pl.BlockSpec((1, KV_WIDTH, n_cmp), whole),
                  pl.BlockSpec((1, seq, KV_WIDTH), whole), pl.BlockSpec((KV_WIDTH, seq), whole_t),
                  pl.BlockSpec((1, seq, KV_WIDTH), whole), pl.BlockSpec((KV_WIDTH, seq), whole_t),
                  pl.BlockSpec((LANES, t), tile_t), pl.BlockSpec((LANES, t), fixed),
                  pl.BlockSpec(ovl_t.shape, fixed)],
        out_specs=pl.BlockSpec((1, t, WIDTH), lambda bi, i: (bi, i, 0)),
        out_shape=jax.ShapeDtypeStruct((b, seq, WIDTH), BF16),
        scratch_shapes=[pltpu.VMEM((N_KV, LANES, GROUP * t), BF16), pltpu.VMEM((STAT_ROWS, GROUP * t), F32),
                        pltpu.VMEM((N_KV, ACC_ROWS, GROUP * t), F32)],
        compiler_params=pltpu.CompilerParams(dimension_semantics=("parallel", "arbitrary"),
                                             vmem_limit_bytes=VMEM_LIMIT),
        name="nsa",
    )(qnt, kcmp, vcmpt, ks, vst, kw, vwt, fgt, bg_t, ovl_t)


def _post_kernel(x_ref, of_ref, on_ref, gf_ref, gn_ref, wo_ref, gm_ref, wu_ref, wd_ref, gl_ref, o_ref):
    yf = _rms(of_ref[...].astype(F32), gf_ref[...]).astype(BF16)
    yn = _rms(on_ref[...].astype(F32), gn_ref[...]).astype(BF16)
    h1 = x_ref[...] + _dot(yf, wo_ref[0:WIDTH, :]) + _dot(yn, wo_ref[WIDTH:2 * WIDTH, :])
    u = _dot(_rms(h1, gm_ref[...]).astype(BF16), wu_ref[...])
    act = jnp.square(jnp.maximum(u, 0.0)).astype(BF16)
    h2 = h1 + _dot(act, wd_ref[...])
    o_ref[...] = _rms(h2, gl_ref[...])


def _post(x2, ofox, onsa, g_fox, g_nsa, w_out, g_mlp, w_up, w_down, g_final, tm):
    m, d = x2.shape
    row = lambda i: (i, 0)
    fixed = lambda i: (0, 0)
    full = lambda a: pl.BlockSpec(a.shape, fixed, pipeline_mode=pl.Buffered(1))
    return pl.pallas_call(
        _post_kernel,
        grid=(m // tm,),
        in_specs=[pl.BlockSpec((tm, d), row), pl.BlockSpec((tm, WIDTH), row), pl.BlockSpec((tm, WIDTH), row),
                  full(g_fox), full(g_nsa), full(w_out), full(g_mlp), full(w_up), full(w_down), full(g_final)],
        out_specs=pl.BlockSpec((tm, d), row),
        out_shape=jax.ShapeDtypeStruct((m, d), F32),
        compiler_params=pltpu.CompilerParams(dimension_semantics=("parallel",), vmem_limit_bytes=VMEM_LIMIT),
        name="post",
    )(x2, ofox, onsa, g_fox, g_nsa, w_out, g_mlp, w_up, w_down, g_final)


_NSA_PERM = np.array([(p % 2) * GROUP + p // 2 for p in range(N_HEADS)])


def _perm_heads(a, axis):
    shape = a.shape
    a = a.reshape(shape[:axis] + (N_HEADS, HEAD_DIM) + shape[axis + 1:])
    a = jnp.take(a, _NSA_PERM, axis=axis)
    return a.reshape(shape)


def _prep_w_in(w):
    qscale = HEAD_DIM ** -0.5 * LOG2E
    sizes = [WIDTH, WIDTH, WIDTH, N_HEADS, WIDTH] + [KV_WIDTH] * 6 + [3 * N_HEADS]
    offs = np.cumsum([0] + sizes)
    qf, kf, vf, fl, qn, kc, vc, ks, vs, kw, vw, gate = [w[:, offs[k]:offs[k + 1]] for k in range(len(sizes))]
    fg = jnp.concatenate([fl, gate, jnp.zeros((w.shape[0], LANES - 4 * N_HEADS), w.dtype)], axis=1)
    w_nat = jnp.concatenate([kf, kc, vc, ks, kw, fg], axis=1).astype(BF16)
    w_tr = jnp.concatenate([qf * qscale, vf, _perm_heads(qn * qscale, 1), vs, vw, fg], axis=1).T.astype(BF16)
    assert w_nat.shape[1] == NAT_COLS and w_tr.shape[0] == TR_ROWS
    return w_nat, w_tr


def _prep_compress(pe, w1, b1, w2, b2):
    half = CMP_LEN // 2
    eye = jnp.eye(N_KV, dtype=F32)

    def expand_w1(wpart):
        w3 = wpart.reshape(half, HEAD_DIM, CMP_HIDDEN)
        return jnp.einsum('ldh,gk->lgdkh', w3, eye).reshape(half * KV_WIDTH, N_KV * CMP_HIDDEN).astype(BF16)

    def expand_pe(ppart):
        flat = jnp.broadcast_to(ppart[:, None, :], (half, N_KV, HEAD_DIM)).reshape(1, half * KV_WIDTH)
        return jnp.broadcast_to(flat, (8, half * KV_WIDTH))

    w2b = jnp.einsum('hd,gk->ghkd', w2, eye).reshape(N_KV * CMP_HIDDEN, KV_WIDTH).astype(BF16)
    return (expand_pe(pe[:half]), expand_pe(pe[half:]),
            expand_w1(w1[:half * HEAD_DIM]), expand_w1(w1[half * HEAD_DIM:]),
            jnp.tile(b1, N_KV)[None, :], w2b, jnp.tile(b2, N_KV)[None, :])


def _rope_tables(pos, reps):
    half = HEAD_DIM // 2
    inv = ROPE_THETA ** (-jnp.arange(half, dtype=F32) / half)
    ang = pos.astype(F32)[:, None] * inv[None, :]
    cos = jnp.cos(ang)
    sin = jnp.sin(ang)
    return (jnp.tile(jnp.concatenate([cos, cos], axis=1), (1, reps)),
            jnp.tile(jnp.concatenate([-sin, sin], axis=1), (1, reps)))


def _overlap_t(n_cmp_slots, n_cmp, n_blocks):
    c = np.arange(n_cmp_slots)[None, :] * CMP_STRIDE
    j = np.arange(SEL_SLOTS)[:, None]
    s = j * SEL_LEN
    ovl = (c < s + SEL_LEN) & (c + CMP_LEN > s) & (np.arange(n_cmp_slots)[None, :] < n_cmp) & (j < n_blocks)
    return jnp.asarray(ovl.astype(np.float32))


def _row_tile(m, want):
    t = want
    while m % t:
        t //= 2
    return t


def _layer(h, seq, g_attn, w_in, b_f, b_gate, cmpk, cmpv, g_fox, g_nsa, w_out, g_mlp, w_up, w_down, g_out):
    m, d = h.shape
    b = m // seq
    n_chunks = seq // CMP_STRIDE
    n_cmp = (seq - CMP_LEN) // CMP_STRIDE + 1
    n_blocks = seq // SEL_LEN
    assert seq % FOX_TILE == 0 and n_chunks % LANES == 0 and SEL_TOPK <= n_blocks <= SEL_SLOTS
    assert n_cmp == n_chunks - 1

    cos_n, sin_n = _rope_tables(jnp.arange(seq), N_KV)
    cos_c, sin_c = _rope_tables(jnp.arange(n_chunks) * CMP_STRIDE + CMP_LEN - 1, N_KV)
    cos_t = jnp.tile(cos_n, (1, GROUP)).T
    sin_t = jnp.tile(sin_n, (1, GROUP)).T

    tm = _row_tile(seq, 512)
    w_nat, w_tr = _prep_w_in(w_in)
    kf, kc, vc, ks, kw, fg, qft, vft, qnt, vst, vwt, fgt = _inproj(
        h, g_attn[None, :], w_nat, w_tr, cos_n, sin_n, cos_t, sin_t, seq, tm)
    r3 = lambda a: a.reshape(b, seq, a.shape[-1])

    bf_pad = jnp.zeros((1, LANES), F32).at[0, :N_HEADS].set(b_f)
    bg_col = jnp.zeros((LANES,), F32).at[N_HEADS:4 * N_HEADS].set(b_gate)
    bg_t = jnp.broadcast_to(bg_col[:, None], (LANES, NSA_TILE))
    crow, kaug = _cumgate(r3(fg), bf_pad, r3(kf))

    chunked = lambda a: a.reshape(b, n_chunks, CMP_STRIDE * KV_WIDTH)
    kcmp, vcmpt = _compress(chunked(kc), chunked(vc), cos_c, sin_c, _prep_compress(*cmpk), _prep_compress(*cmpv))

    ofox = _fox(qft, kaug, vft, crow)
    onsa = _nsa(qnt, kcmp, vcmpt, r3(ks), vst, r3(kw), vwt, fgt, bg_t,
                _overlap_t(n_chunks, n_cmp, n_blocks), min(SEL_TOPK, n_blocks))

    w_out_p = jnp.concatenate([w_out[:WIDTH], _perm_heads(w_out[WIDTH:], 0)], axis=0).astype(BF16)
    return _post(h, ofox.reshape(m, WIDTH), onsa.reshape(m, WIDTH), g_fox[None, :], _perm_heads(g_nsa, 0)[None, :],
                 w_out_p, g_mlp[None, :], w_up.astype(BF16), w_down.astype(BF16), g_out[None, :],
                 _row_tile(seq, 256))


def kernel(x, g_attn, w_in, b_f, b_gate, cmpk_pe, cmpk_w1, cmpk_b1, cmpk_w2, cmpk_b2, cmpv_pe, cmpv_w1, cmpv_b1,
           cmpv_w2, cmpv_b2, g_fox, g_nsa, w_out, g_mlp, w_up, w_down, g_final):
    b, seq, d = x.shape
    depth = g_attn.shape[0]
    assert depth == 1, "the final rmsnorm is fused into the (single) layer's last kernel"
    h = x.reshape(b * seq, d)
    out = _layer(h, seq, g_attn[0], w_in[0], b_f[0], b_gate[0],
                 (cmpk_pe[0], cmpk_w1[0], cmpk_b1[0], cmpk_w2[0], cmpk_b2[0]),
                 (cmpv_pe[0], cmpv_w1[0], cmpv_b1[0], cmpv_w2[0], cmpv_b2[0]),
                 g_fox[0], g_nsa[0], w_out[0], g_mlp[0], w_up[0], w_down[0], g_final)
    return out.reshape(b, seq, d)
```

```python
import functools

import numpy as np
import jax
import jax.numpy as jnp
from jax import lax
from jax.experimental import pallas as pl
from jax.experimental.pallas import tpu as pltpu

F32 = jnp.float32
BF16 = jnp.bfloat16

HEAD_DIM = 64
N_HEADS = 8
N_KV = 2
GROUP = N_HEADS // N_KV
WIDTH = N_HEADS * HEAD_DIM
KV_WIDTH = N_KV * HEAD_DIM
CMP_LEN = 32
CMP_STRIDE = 16
CMP_HIDDEN = 256
SEL_LEN = 64
SEL_SHIFT = 6
SEL_TOPK = 16
SEL_SLOTS = 64
WINDOW = 512
ROPE_THETA = 10000.0
EPS = 1e-6
NEG = -1e30
MASK_FILL = -(2.0 ** 100)
FORCE_BONUS = 1e4
LOG2E = 1.4426950408889634
LANES = 128
HALF = LANES // 2
BF16_ROWS = 16

VMEM_LIMIT = 56 * 1024 * 1024


def _dot(a, b, precision=None):
    return jnp.dot(a, b, preferred_element_type=F32, precision=precision)


def _dot_nt(a, b, precision=None):
    return lax.dot_general(a, b, (((1,), (1,)), ((), ())), preferred_element_type=F32, precision=precision)


def _rms(x, g):
    return x * lax.rsqrt(jnp.mean(x * x, axis=-1, keepdims=True) + EPS) * g


def _rope(x, cos, sin_signed):
    w = x.shape[-1]
    lane = lax.broadcasted_iota(jnp.int32, (1, w), 1)
    first = (lane % HEAD_DIM) < (HEAD_DIM // 2)
    partner = jnp.where(first, pltpu.roll(x, w - HEAD_DIM // 2, 1), pltpu.roll(x, HEAD_DIM // 2, 1))
    return x * cos + partner * sin_signed


def _rope_t(x, cos, sin_signed):
    r = x.shape[0]
    row = lax.broadcasted_iota(jnp.int32, (r, 1), 0)
    first = (row % HEAD_DIM) < (HEAD_DIM // 2)
    partner = jnp.where(first, pltpu.roll(x, r - HEAD_DIM // 2, 0), pltpu.roll(x, HEAD_DIM // 2, 0))
    return x * cos + partner * sin_signed


def _split3(c):
    c1 = c.astype(BF16).astype(F32)
    r1 = c - c1
    c2 = r1.astype(BF16).astype(F32)
    return c1, c2, r1 - c2


NAT_KF, NAT_KC, NAT_VC, NAT_KS, NAT_KW, NAT_FG, NAT_COLS = 0, 512, 640, 768, 896, 1024, 1152
TR_QF, TR_VF, TR_QN, TR_VS, TR_VW, TR_FG, TR_ROWS = 0, 512, 1024, 1536, 1664, 1792, 1920


def _inproj_kernel(x_ref, g_ref, wn_ref, wt_ref, cos_ref, sin_ref, cost_ref, sint_ref,
                   kf_ref, kc_ref, vc_ref, ks_ref, kw_ref, fg_ref,
                   qft_ref, vft_ref, qnt_ref, vst_ref, vwt_ref, fgt_ref):
    hb = _rms(x_ref[...], g_ref[...]).astype(BF16)

    def nat(lo, width):
        return _dot(hb, wn_ref[:, lo:lo + width])

    def tr(lo, rows):
        return _dot_nt(wt_ref[lo:lo + rows, :], hb)

    cos = cos_ref[...]
    sin = sin_ref[...]
    kf_ref[...] = nat(NAT_KF, WIDTH).astype(BF16)
    kc_ref[...] = nat(NAT_KC, KV_WIDTH).astype(BF16)
    vc_ref[...] = nat(NAT_VC, KV_WIDTH).astype(BF16)
    ks_ref[...] = _rope(nat(NAT_KS, KV_WIDTH), cos, sin).astype(BF16)
    kw_ref[...] = _rope(nat(NAT_KW, KV_WIDTH), cos, sin).astype(BF16)
    fg_ref[...] = nat(NAT_FG, LANES)
    qft_ref[...] = tr(TR_QF, WIDTH).astype(BF16)
    vft_ref[...] = tr(TR_VF, WIDTH).astype(BF16)
    qnt_ref[...] = _rope_t(tr(TR_QN, WIDTH), cost_ref[...], sint_ref[...]).astype(BF16)
    vst_ref[...] = tr(TR_VS, KV_WIDTH).astype(BF16)
    vwt_ref[...] = tr(TR_VW, KV_WIDTH).astype(BF16)
    fgt_ref[...] = tr(TR_FG, LANES)


def _inproj(x2, g_attn, w_nat, w_tr, cos_n, sin_n, cos_t, sin_t, seq, tm):
    m, d = x2.shape
    sblocks = seq // tm
    row = lambda i: (i, 0)
    col = lambda i: (0, i)
    fixed = lambda i: (0, 0)
    tab = lambda i: (i % sblocks, 0)
    tab_t = lambda i: (0, i % sblocks)
    sds = jax.ShapeDtypeStruct
    out_shape = [sds((m, WIDTH), BF16)] + [sds((m, KV_WIDTH), BF16)] * 4 + [sds((m, LANES), F32),
                 sds((WIDTH, m), BF16), sds((WIDTH, m), BF16), sds((WIDTH, m), BF16),
                 sds((KV_WIDTH, m), BF16), sds((KV_WIDTH, m), BF16), sds((LANES, m), F32)]
    out_specs = ([pl.BlockSpec((tm, WIDTH), row)] + [pl.BlockSpec((tm, KV_WIDTH), row)] * 4
                 + [pl.BlockSpec((tm, LANES), row)]
                 + [pl.BlockSpec((WIDTH, tm), col)] * 3 + [pl.BlockSpec((KV_WIDTH, tm), col)] * 2
                 + [pl.BlockSpec((LANES, tm), col)])
    return pl.pallas_call(
        _inproj_kernel,
        grid=(m // tm,),
        in_specs=[pl.BlockSpec((tm, d), row), pl.BlockSpec((1, d), fixed),
                  pl.BlockSpec(w_nat.shape, fixed), pl.BlockSpec(w_tr.shape, fixed),
                  pl.BlockSpec((tm, KV_WIDTH), tab), pl.BlockSpec((tm, KV_WIDTH), tab),
                  pl.BlockSpec((WIDTH, tm), tab_t), pl.BlockSpec((WIDTH, tm), tab_t)],
        out_specs=out_specs,
        out_shape=out_shape,
        compiler_params=pltpu.CompilerParams(dimension_semantics=("parallel",), vmem_limit_bytes=VMEM_LIMIT),
        name="inproj",
    )(x2, g_attn, w_nat, w_tr, cos_n, sin_n, cos_t, sin_t)


CUM_BLOCK = 256
N_BIAS = 3


def _bias_lane_base(h):
    return HALF if h % 2 == 0 else 0


def _cumgate_kernel(fg_ref, bf_ref, kf_ref, crow_ref, kaug_ref):
    seq = fg_ref.shape[1]
    r = lax.broadcasted_iota(jnp.int32, (CUM_BLOCK, CUM_BLOCK), 0)
    c = lax.broadcasted_iota(jnp.int32, (CUM_BLOCK, CUM_BLOCK), 1)
    tri = jnp.where(r >= c, 1.0, 0.0).astype(F32)
    er = lax.broadcasted_iota(jnp.int32, (N_HEADS, LANES), 0)
    ec = lax.broadcasted_iota(jnp.int32, (N_HEADS, LANES), 1)
    pick = jnp.where(er == ec, 1.0, 0.0).astype(F32)
    bias = bf_ref[...]
    lane = lax.broadcasted_iota(jnp.int32, (CUM_BLOCK, LANES), 1)

    def body(blk, carry):
        off = pl.multiple_of(blk * CUM_BLOCK, CUM_BLOCK)
        z = fg_ref[0, pl.ds(off, CUM_BLOCK), :] + bias
        logf = jnp.minimum(z, 0.0) - jnp.log(1.0 + jnp.exp(-jnp.abs(z)))
        cs = _dot(tri, logf, precision=lax.Precision.HIGHEST) + carry
        c2 = cs * LOG2E
        crow_ref[0, :, pl.ds(off, CUM_BLOCK)] = _dot_nt(pick, c2, precision=lax.Precision.HIGHEST)
        for h in range(N_HEADS):
            c1, cb, cc = _split3(jnp.broadcast_to(c2[:, h:h + 1], (CUM_BLOCK, LANES)))
            d = lane - _bias_lane_base(h)
            feat = jnp.where(d == N_BIAS, -c1, jnp.where(d == N_BIAS + 1, -cb, jnp.where(d == N_BIAS + 2, -cc,
                   jnp.where(d == 0, 1.0, jnp.where(d == 1, 1.0, jnp.where(d == 2, 1.0, 0.0))))))
            mine = (lane < HALF) if h % 2 == 0 else (lane >= HALF)
            kcol = kf_ref[0, pl.ds(off, CUM_BLOCK), (h // 2) * LANES:(h // 2 + 1) * LANES]
            kaug_ref[0, h, pl.ds(off, CUM_BLOCK), :] = jnp.where(mine, kcol, feat.astype(BF16))
        return cs[CUM_BLOCK - 1:CUM_BLOCK, :]

    lax.fori_loop(0, seq // CUM_BLOCK, body, jnp.zeros((1, LANES), F32))


def _cumgate(fg3, bf_pad, kf3):
    b, seq, _ = fg3.shape
    return pl.pallas_call(
        _cumgate_kernel,
        grid=(b,),
        in_specs=[pl.BlockSpec((1, seq, LANES), lambda i: (i, 0, 0)), pl.BlockSpec((1, LANES), lambda i: (0, 0)),
                  pl.BlockSpec((1, seq, WIDTH), lambda i: (i, 0, 0))],
        out_specs=[pl.BlockSpec((1, N_HEADS, seq), lambda i: (i, 0, 0)),
                   pl.BlockSpec((1, N_HEADS, seq, LANES), lambda i: (i, 0, 0, 0))],
        out_shape=[jax.ShapeDtypeStruct((b, N_HEADS, seq), F32),
                   jax.ShapeDtypeStruct((b, N_HEADS, seq, LANES), BF16)],
        compiler_params=pltpu.CompilerParams(dimension_semantics=("parallel",), vmem_limit_bytes=VMEM_LIMIT),
        name="cumgate",
    )(fg3, bf_pad, kf3)


def _compress_one(x_ref, pea_ref, peb_ref, wa_ref, wb_ref, b1_ref, w2_ref, b2_ref):
    xk = x_ref[0]
    n = xk.shape[0]
    wa = wa_ref[...]
    wb = wb_ref[...]
    first = _dot(xk, wa)
    second = _dot(xk, wb)
    pe_term = _dot(pea_ref[...].astype(BF16), wa) + _dot(peb_ref[...].astype(BF16), wb)
    hidden = first + pltpu.roll(second, n - 1, 0) + pe_term[0:1, :] + b1_ref[...]
    act = jax.nn.gelu(hidden)
    return _dot(act.astype(BF16), w2_ref[...]) + b2_ref[...]


def _compress_kernel(kc_ref, vc_ref, cos_ref, sin_ref,
                     kpea, kpeb, kwa, kwb, kb1, kw2, kb2,
                     vpea, vpeb, vwa, vwb, vb1, vw2, vb2,
                     kcmp_ref, vcmpt_ref):
    kcmp = _compress_one(kc_ref, kpea, kpeb, kwa, kwb, kb1, kw2, kb2)
    kcmp_ref[0] = _rope(kcmp, cos_ref[...], sin_ref[...]).astype(BF16)
    vcmpt_ref[0] = _compress_one(vc_ref, vpea, vpeb, vwa, vwb, vb1, vw2, vb2).T.astype(BF16)


def _compress(kc3, vc3, cos_c, sin_c, kparams, vparams):
    b, n, w = kc3.shape
    x_spec = pl.BlockSpec((1, n, w), lambda i: (i, 0, 0))
    full = lambda a: pl.BlockSpec(a.shape, lambda i: (0,) * a.ndim)
    params = list(kparams) + list(vparams)
    return pl.pallas_call(
        _compress_kernel,
        grid=(b,),
        in_specs=[x_spec, x_spec, full(cos_c), full(sin_c)] + [full(p) for p in params],
        out_specs=[pl.BlockSpec((1, n, KV_WIDTH), lambda i: (i, 0, 0)),
                   pl.BlockSpec((1, KV_WIDTH, n), lambda i: (i, 0, 0))],
        out_shape=[jax.ShapeDtypeStruct((b, n, KV_WIDTH), BF16), jax.ShapeDtypeStruct((b, KV_WIDTH, n), BF16)],
        compiler_params=pltpu.CompilerParams(dimension_semantics=("parallel",), vmem_limit_bytes=VMEM_LIMIT),
        name="compress",
    )(kc3, vc3, cos_c, sin_c, *params)


ACC_ROWS = HEAD_DIM + BF16_ROWS


def _flash_step_t(s, vt, m_ref, acc_ref, idx):
    row = slice(idx, idx + 1)
    m_old = m_ref[row, :]
    m_new = jnp.maximum(m_old, jnp.max(s, axis=0, keepdims=True))
    alpha = jnp.exp2(m_old - m_new)
    p = jnp.exp2((s - m_new).astype(BF16))
    m_ref[row, :] = m_new
    vt_ones = jnp.concatenate([vt, jnp.ones((BF16_ROWS, vt.shape[1]), BF16)], axis=0)
    acc_ref[idx] = alpha * acc_ref[idx] + _dot(vt_ones, p)


def _flash_reset(m_ref, acc_ref):
    m_ref[...] = jnp.full(m_ref.shape, NEG, F32)
    acc_ref[...] = jnp.zeros(acc_ref.shape, F32)


def _flash_finish(acc_ref, idx):
    acc = acc_ref[idx]
    return acc[0:HEAD_DIM] * (1.0 / acc[HEAD_DIM:HEAD_DIM + 1])


def _for_tiles(n, unroll, visit):
    groups = n // unroll

    def body(jj, carry):
        visit(jj * unroll, unroll)
        return carry

    lax.fori_loop(0, groups, body, 0)
    done = groups * unroll
    rem = n - done
    p = unroll // 2
    while p >= 1:
        first = done + (rem & ~(2 * p - 1))

        @pl.when((rem & p) != 0)
        def _(first=first, p=p):
            visit(first, p)

        p //= 2


FOX_TILE = 256
FOX_AHEAD = 4
FOX_UNROLL = 4


def _fox_kernel(qt_ref, kaug_ref, vt_ref, crow_ref, o_ref, qaug_ref, m_ref, acc_ref):
    t = FOX_TILE
    i = pl.program_id(1)
    diag = pl.multiple_of(i * t, t)
    kr = lax.broadcasted_iota(jnp.int32, (t, t), 0)
    qc = lax.broadcasted_iota(jnp.int32, (t, t), 1)
    causal = kr <= qc
    frow = lax.broadcasted_iota(jnp.int32, (BF16_ROWS, t), 0)
    pad = jnp.zeros((HALF - BF16_ROWS, t), BF16)

    _flash_reset(m_ref, acc_ref)
    for h in range(N_HEADS):
        c1, c2, c3 = _split3(crow_ref[0, h:h + 1, :])
        feat = jnp.where(frow == 0, c1, jnp.where(frow == 1, c2, jnp.where(frow == 2, c3,
               jnp.where(frow < 2 * N_BIAS, 1.0, 0.0)))).astype(BF16)
        qh = qt_ref[h * HEAD_DIM:(h + 1) * HEAD_DIM, :]
        parts = [qh, feat, pad] if _bias_lane_base(h) == HALF else [feat, pad, qh]
        qaug_ref[h] = jnp.concatenate(parts, axis=0)

    def step(offs, masked):
        chains = [(off, h) for off in offs for h in range(N_HEADS)]

        def scores(c):
            off, h = chains[c]
            return _dot(kaug_ref[0, h, pl.ds(off, t), :], qaug_ref[h])

        pending = {c: scores(c) for c in range(FOX_AHEAD)}
        for c, (off, h) in enumerate(chains):
            if c + FOX_AHEAD < len(chains):
                pending[c + FOX_AHEAD] = scores(c + FOX_AHEAD)
            s = pending.pop(c)
            if masked:
                s = jnp.where(causal, s, MASK_FILL)
            _flash_step_t(s, vt_ref[h * HEAD_DIM:(h + 1) * HEAD_DIM, pl.ds(off, t)], m_ref, acc_ref, h)

    def visit(first, count):
        step([pl.multiple_of((first + u) * t, t) for u in range(count)], False)

    _for_tiles(i, FOX_UNROLL, visit)
    step([diag], True)

    for col in range(WIDTH // LANES):
        pair = [_flash_finish(acc_ref, h) for h in (2 * col, 2 * col + 1)]
        o_ref[0, :, col * LANES:(col + 1) * LANES] = jnp.concatenate(pair, axis=0).T.astype(BF16)


def _fox(qft, kaug, vft, crow):
    b, _, seq, _ = kaug.shape
    t = FOX_TILE
    nq = seq // t
    return pl.pallas_call(
        _fox_kernel,
        grid=(b, nq),
        in_specs=[pl.BlockSpec((WIDTH, t), lambda bi, i: (0, bi * nq + i)),
                  pl.BlockSpec((1, N_HEADS, seq, LANES), lambda bi, i: (bi, 0, 0, 0)),
                  pl.BlockSpec((WIDTH, seq), lambda bi, i: (0, bi)),
                  pl.BlockSpec((1, N_HEADS, t), lambda bi, i: (bi, 0, i))],
        out_specs=pl.BlockSpec((1, t, WIDTH), lambda bi, i: (bi, i, 0)),
        out_shape=jax.ShapeDtypeStruct((b, seq, WIDTH), BF16),
        scratch_shapes=[pltpu.VMEM((N_HEADS, LANES, t), BF16), pltpu.VMEM((N_HEADS, t), F32),
                        pltpu.VMEM((N_HEADS, ACC_ROWS, t), F32)],
        compiler_params=pltpu.CompilerParams(dimension_semantics=("parallel", "arbitrary"),
                                             vmem_limit_bytes=VMEM_LIMIT),
        name="fox",
    )(qft, kaug, vft, crow)


NSA_TILE = 128
STAT_ROWS = 8
NSA_UNROLL = 8
NSA_AHEAD = 4


def _nsa_kernel(qt_ref, kcmp_ref, vcmpt_ref, ks_ref, vst_ref, kw_ref, vwt_ref, fgt_ref, bg_ref, ovl_ref, o_ref,
                qaug_ref, m_ref, acc_ref, *, topk):
    t = NSA_TILE
    wide = GROUP * t
    n_cmp = kcmp_ref.shape[1]
    i = pl.program_id(1)
    start = i * t
    diag = pl.multiple_of(start, t)
    lane = lax.broadcasted_iota(jnp.int32, (1, LANES), 1)
    mine = [lane < HALF, lane >= HALF]
    zero_bf = jnp.zeros((), BF16)

    q4 = jnp.concatenate([qt_ref[n * LANES:(n + 1) * LANES, :] for n in range(GROUP)], axis=1)
    qpos4 = start + (lax.broadcasted_iota(jnp.int32, (1, wide), 1) & (t - 1))
    qpos = start + lax.broadcasted_iota(jnp.int32, (1, t), 1)

    cmp_end = lax.broadcasted_iota(jnp.int32, (n_cmp, 1), 0) * CMP_STRIDE + (CMP_LEN - 1)
    vis = cmp_end <= qpos4
    slot = lax.broadcasted_iota(jnp.int32, (SEL_SLOTS, t), 0)
    cur = qpos >> SEL_SHIFT
    valid = slot <= cur
    forced = (slot == 0) | (slot == cur) | (slot == cur - 1)
    sub8 = lax.broadcasted_iota(jnp.int32, (STAT_ROWS, t), 0)
    o_cmp = []
    for g in range(N_KV):
        s = jnp.where(vis, _dot(jnp.where(mine[g], kcmp_ref[0], zero_bf), q4), NEG)
        e = jnp.where(vis, jnp.exp2(s - jnp.max(s, axis=0, keepdims=True)), 0.0)
        denom = jnp.sum(e, axis=0, keepdims=True)
        p = e * jnp.where(denom > 0.0, 1.0 / denom, 0.0)
        o_cmp.append(_dot(vcmpt_ref[0, g * HEAD_DIM:(g + 1) * HEAD_DIM, :], p.astype(BF16)))
        psum = p[:, 0:t]
        for n in range(1, GROUP):
            psum = psum + p[:, n * t:(n + 1) * t]
        imp = _dot(ovl_ref[...], psum, precision=lax.Precision.HIGHEST)
        score = jnp.where(valid, imp + jnp.where(forced, FORCE_BONUS, 0.0), -1.0)
        tiles = [score[r:r + STAT_ROWS, :] for r in range(0, SEL_SLOTS, STAT_ROWS)]
        ranks = [jnp.zeros((STAT_ROWS, t), F32) for _ in tiles]
        for jp in range(SEL_SLOTS):
            other = jnp.broadcast_to(score[jp:jp + 1, :], (STAT_ROWS, t))
            for k, tile in enumerate(tiles):
                first = k * STAT_ROWS
                ge = jnp.where(other >= tile, 1.0, 0.0)
                gt = jnp.where(other > tile, 1.0, 0.0)
                if first > jp:
                    ahead = ge
                elif first + STAT_ROWS - 1 <= jp:
                    ahead = gt
                else:
                    ahead = jnp.where(sub8 + first > jp, ge, gt)
                ranks[k] = ranks[k] + ahead
        rank = jnp.concatenate(ranks, axis=0)
        selb = jnp.where(rank < topk, 0.0, MASK_FILL).astype(BF16)
        selb4 = jnp.concatenate([selb] * GROUP, axis=1)
        qaug_ref[g] = jnp.concatenate([q4[0:HALF], selb4] if g == 0 else [selb4, q4[HALF:LANES]], axis=0)

    kr = lax.broadcasted_iota(jnp.int32, (t, wide), 0)
    causal = kr <= (lax.broadcasted_iota(jnp.int32, (t, wide), 1) & (t - 1))
    key_row = lax.broadcasted_iota(jnp.int32, (t, LANES), 0)
    lane_full = lax.broadcasted_iota(jnp.int32, (t, LANES), 1)
    slot_minus_row = [lane_full - HALF - (key_row >> SEL_SHIFT), lane_full - (key_row >> SEL_SHIFT)]

    def sweep(tiles, score, vt_ref_):
        chains = [(off, mask, g) for off, mask in tiles for g in range(N_KV)]
        pending = {c: score(chains[c][0], chains[c][2]) for c in range(min(NSA_AHEAD, len(chains)))}
        for c, (off, mask, g) in enumerate(chains):
            if c + NSA_AHEAD < len(chains):
                nxt = chains[c + NSA_AHEAD]
                pending[c + NSA_AHEAD] = score(nxt[0], nxt[2])
            s = pending.pop(c)
            if mask is not None:
                s = jnp.where(mask, s, MASK_FILL)
            _flash_step_t(s, vt_ref_[g * HEAD_DIM:(g + 1) * HEAD_DIM, pl.ds(off, t)], m_ref, acc_ref, g)

    def tile_off(j):
        return pl.multiple_of(j * t, t)

    def sel_score(off, g):
        onehot = jnp.where(slot_minus_row[g] == (off >> SEL_SHIFT), 1.0, 0.0).astype(BF16)
        return _dot(jnp.where(mine[g], ks_ref[0, pl.ds(off, t), :], onehot), qaug_ref[g])

    def sel_visit(first, count):
        sweep([(tile_off(first + u), None) for u in range(count)], sel_score, vst_ref)

    _flash_reset(m_ref, acc_ref)
    _for_tiles(i, NSA_UNROLL, sel_visit)
    sweep([(diag, causal)], sel_score, vst_ref)
    o_slc = [_flash_finish(acc_ref, g) for g in range(N_KV)]

    def win_score(off, g):
        return _dot(jnp.where(mine[g], kw_ref[0, pl.ds(off, t), :], zero_bf), q4)

    n_back = WINDOW // t
    _flash_reset(m_ref, acc_ref)

    @pl.when(i >= n_back)
    def _():
        first = tile_off(i - n_back)
        in_band = qpos4 - (first + kr) < WINDOW
        sweep([(first, in_band)] + [(tile_off(i - n_back + u), None) for u in range(1, n_back)] + [(diag, causal)],
              win_score, vwt_ref)

    @pl.when(i < n_back)
    def _():
        def win_body(j, carry):
            sweep([(tile_off(j), None)], win_score, vwt_ref)
            return carry

        lax.fori_loop(0, i, win_body, 0)
        sweep([(diag, causal)], win_score, vwt_ref)

    o_win = [_flash_finish(acc_ref, g) for g in range(N_KV)]

    gates = jax.nn.sigmoid(fgt_ref[...] + bg_ref[...])
    for n in range(GROUP):
        cols = slice(n * t, (n + 1) * t)
        mixed = []
        for g in range(N_KV):
            base = N_HEADS + (g * GROUP + n) * 3
            mixed.append(gates[base:base + 1, :] * o_cmp[g][:, cols]
                         + gates[base + 1:base + 2, :] * o_slc[g][:, cols]
                         + gates[base + 2:base + 3, :] * o_win[g][:, cols])
        o_ref[0, :, n * LANES:(n + 1) * LANES] = jnp.concatenate(mixed, axis=0).T.astype(BF16)


def _nsa(qnt, kcmp, vcmpt, ks, vst, kw, vwt, fgt, bg_t, ovl_t, topk):
    b, seq, _ = ks.shape
    t = NSA_TILE
    nq = seq // t
    n_cmp = kcmp.shape[1]
    whole = lambda bi, i: (bi, 0, 0)
    whole_t = lambda bi, i: (0, bi)
    tile_t = lambda bi, i: (0, bi * nq + i)
    fixed = lambda bi, i: (0, 0)
    return pl.pallas_call(
        functools.partial(_nsa_kernel, topk=topk),
        grid=(b, nq),
        in_specs=[pl.BlockSpec((WIDTH, t), tile_t),
                  pl.BlockSpec((1, n_cmp, KV_WIDTH), whole), pl.BlockSpec((1, KV_WIDTH, n_cmp), whole),
                  pl.BlockSpec((1, seq, KV_WIDTH), whole), pl.BlockSpec((KV_WIDTH, seq), whole_t),
                  pl.BlockSpec((1, seq, KV_WIDTH), whole), pl.BlockSpec((KV_WIDTH, seq), whole_t),
                  pl.BlockSpec((LANES, t), tile_t), pl.BlockSpec((LANES, t), fixed),
                  pl.BlockSpec(ovl_t.shape, fixed)],
        out_specs=pl.BlockSpec((1, t, WIDTH), lambda bi, i: (bi, i, 0)),
        out_shape=jax.ShapeDtypeStruct((b, seq, WIDTH), BF16),
        scratch_shapes=[pltpu.VMEM((N_KV, LANES, GROUP * t), BF16), pltpu.VMEM((STAT_ROWS, GROUP * t), F32),
                        pltpu.VMEM((N_KV, ACC_ROWS, GROUP * t), F32)],
        compiler_params=pltpu.CompilerParams(dimension_semantics=("parallel", "arbitrary"),
                                             vmem_limit_bytes=VMEM_LIMIT),
        name="nsa",
    )(qnt, kcmp, vcmpt, ks, vst, kw, vwt, fgt, bg_t, ovl_t)


def _post_kernel(x_ref, of_ref, on_ref, gf_ref, gn_ref, wo_ref, gm_ref, wu_ref, wd_ref, gl_ref, o_ref):
    yf = _rms(of_ref[...].astype(F32), gf_ref[...]).astype(BF16)
    yn = _rms(on_ref[...].astype(F32), gn_ref[...]).astype(BF16)
    h1 = x_ref[...] + _dot(yf, wo_ref[0:WIDTH, :]) + _dot(yn, wo_ref[WIDTH:2 * WIDTH, :])
    u = _dot(_rms(h1, gm_ref[...]).astype(BF16), wu_ref[...])
    act = jnp.square(jnp.maximum(u, 0.0)).astype(BF16)
    h2 = h1 + _dot(act, wd_ref[...])
    o_ref[...] = _rms(h2, gl_ref[...])


def _post(x2, ofox, onsa, g_fox, g_nsa, w_out, g_mlp, w_up, w_down, g_final, tm):
    m, d = x2.shape
    row = lambda i: (i, 0)
    fixed = lambda i: (0, 0)
    full = lambda a: pl.BlockSpec(a.shape, fixed, pipeline_mode=pl.Buffered(1))
    return pl.pallas_call(
        _post_kernel,
        grid=(m // tm,),
        in_specs=[pl.BlockSpec((tm, d), row), pl.BlockSpec((tm, WIDTH), row), pl.BlockSpec((tm, WIDTH), row),
                  full(g_fox), full(g_nsa), full(w_out), full(g_mlp), full(w_up), full(w_down), full(g_final)],
        out_specs=pl.BlockSpec((tm, d), row),
        out_shape=jax.ShapeDtypeStruct((m, d), F32),
        compiler_params=pltpu.CompilerParams(dimension_semantics=("parallel",), vmem_limit_bytes=VMEM_LIMIT),
        name="post",
    )(x2, ofox, onsa, g_fox, g_nsa, w_out, g_mlp, w_up, w_down, g_final)


_NSA_PERM = np.array([(p % 2) * GROUP + p // 2 for p in range(N_HEADS)])


def _perm_heads(a, axis):
    shape = a.shape
    a = a.reshape(shape[:axis] + (N_HEADS, HEAD_DIM) + shape[axis + 1:])
    a = jnp.take(a, _NSA_PERM, axis=axis)
    return a.reshape(shape)


def _prep_w_in(w):
    qscale = HEAD_DIM ** -0.5 * LOG2E
    sizes = [WIDTH, WIDTH, WIDTH, N_HEADS, WIDTH] + [KV_WIDTH] * 6 + [3 * N_HEADS]
    offs = np.cumsum([0] + sizes)
    qf, kf, vf, fl, qn, kc, vc, ks, vs, kw, vw, gate = [w[:, offs[k]:offs[k + 1]] for k in range(len(sizes))]
    fg = jnp.concatenate([fl, gate, jnp.zeros((w.shape[0], LANES - 4 * N_HEADS), w.dtype)], axis=1)
    w_nat = jnp.concatenate([kf, kc, vc, ks, kw, fg], axis=1).astype(BF16)
    w_tr = jnp.concatenate([qf * qscale, vf, _perm_heads(qn * qscale, 1), vs, vw, fg], axis=1).T.astype(BF16)
    assert w_nat.shape[1] == NAT_COLS and w_tr.shape[0] == TR_ROWS
    return w_nat, w_tr


def _prep_compress(pe, w1, b1, w2, b2):
    half = CMP_LEN // 2
    eye = jnp.eye(N_KV, dtype=F32)

    def expand_w1(wpart):
        w3 = wpart.reshape(half, HEAD_DIM, CMP_HIDDEN)
        return jnp.einsum('ldh,gk->lgdkh', w3, eye).reshape(half * KV_WIDTH, N_KV * CMP_HIDDEN).astype(BF16)

    def expand_pe(ppart):
        flat = jnp.broadcast_to(ppart[:, None, :], (half, N_KV, HEAD_DIM)).reshape(1, half * KV_WIDTH)
        return jnp.broadcast_to(flat, (8, half * KV_WIDTH))

    w2b = jnp.einsum('hd,gk->ghkd', w2, eye).reshape(N_KV * CMP_HIDDEN, KV_WIDTH).astype(BF16)
    return (expand_pe(pe[:half]), expand_pe(pe[half:]),
            expand_w1(w1[:half * HEAD_DIM]), expand_w1(w1[half * HEAD_DIM:]),
            jnp.tile(b1, N_KV)[None, :], w2b, jnp.tile(b2, N_KV)[None, :])


def _rope_tables(pos, reps):
    half = HEAD_DIM // 2
    inv = ROPE_THETA ** (-jnp.arange(half, dtype=F32) / half)
    ang = pos.astype(F32)[:, None] * inv[None, :]
    cos = jnp.cos(ang)
    sin = jnp.sin(ang)
    return (jnp.tile(jnp.concatenate([cos, cos], axis=1), (1, reps)),
            jnp.tile(jnp.concatenate([-sin, sin], axis=1), (1, reps)))


def _overlap_t(n_cmp_slots, n_cmp, n_blocks):
    c = np.arange(n_cmp_slots)[None, :] * CMP_STRIDE
    j = np.arange(SEL_SLOTS)[:, None]
    s = j * SEL_LEN
    ovl = (c < s + SEL_LEN) & (c + CMP_LEN > s) & (np.arange(n_cmp_slots)[None, :] < n_cmp) & (j < n_blocks)
    return jnp.asarray(ovl.astype(np.float32))


def _row_tile(m, want):
    t = want
    while m % t:
        t //= 2
    return t


def _layer(h, seq, g_attn, w_in, b_f, b_gate, cmpk, cmpv, g_fox, g_nsa, w_out, g_mlp, w_up, w_down, g_out):
    m, d = h.shape
    b = m // seq
    n_chunks = seq // CMP_STRIDE
    n_cmp = (seq - CMP_LEN) // CMP_STRIDE + 1
    n_blocks = seq // SEL_LEN
    assert seq % FOX_TILE == 0 and n_chunks % LANES == 0 and SEL_TOPK <= n_blocks <= SEL_SLOTS
    assert n_cmp == n_chunks - 1

    cos_n, sin_n = _rope_tables(jnp.arange(seq), N_KV)
    cos_c, sin_c = _rope_tables(jnp.arange(n_chunks) * CMP_STRIDE + CMP_LEN - 1, N_KV)
    cos_t = jnp.tile(cos_n, (1, GROUP)).T
    sin_t = jnp.tile(sin_n, (1, GROUP)).T

    tm = _row_tile(seq, 512)
    w_nat, w_tr = _prep_w_in(w_in)
    kf, kc, vc, ks, kw, fg, qft, vft, qnt, vst, vwt, fgt = _inproj(
        h, g_attn[None, :], w_nat, w_tr, cos_n, sin_n, cos_t, sin_t, seq, tm)
    r3 = lambda a: a.reshape(b, seq, a.shape[-1])

    bf_pad = jnp.zeros((1, LANES), F32).at[0, :N_HEADS].set(b_f)
    bg_col = jnp.zeros((LANES,), F32).at[N_HEADS:4 * N_HEADS].set(b_gate)
    bg_t = jnp.broadcast_to(bg_col[:, None], (LANES, NSA_TILE))
    crow, kaug = _cumgate(r3(fg), bf_pad, r3(kf))

    chunked = lambda a: a.reshape(b, n_chunks, CMP_STRIDE * KV_WIDTH)
    kcmp, vcmpt = _compress(chunked(kc), chunked(vc), cos_c, sin_c, _prep_compress(*cmpk), _prep_compress(*cmpv))

    ofox = _fox(qft, kaug, vft, crow)
    onsa = _nsa(qnt, kcmp, vcmpt, r3(ks), vst, r3(kw), vwt, fgt, bg_t,
                _overlap_t(n_chunks, n_cmp, n_blocks), min(SEL_TOPK, n_blocks))

    w_out_p = jnp.concatenate([w_out[:WIDTH], _perm_heads(w_out[WIDTH:], 0)], axis=0).astype(BF16)
    return _post(h, ofox.reshape(m, WIDTH), onsa.reshape(m, WIDTH), g_fox[None, :], _perm_heads(g_nsa, 0)[None, :],
                 w_out_p, g_mlp[None, :], w_up.astype(BF16), w_down.astype(BF16), g_out[None, :],
                 _row_tile(seq, 256))


def kernel(x, g_attn, w_in, b_f, b_gate, cmpk_pe, cmpk_w1, cmpk_b1, cmpk_w2, cmpk_b2, cmpv_pe, cmpv_w1, cmpv_b1,
           cmpv_w2, cmpv_b2, g_fox, g_nsa, w_out, g_mlp, w_up, w_down, g_final):
    b, seq, d = x.shape
    depth = g_attn.shape[0]
    assert depth == 1, "the final rmsnorm is fused into the (single) layer's last kernel"
    h = x.reshape(b * seq, d)
    out = _layer(h, seq, g_attn[0], w_in[0], b_f[0], b_gate[0],
                 (cmpk_pe[0], cmpk_w1[0], cmpk_b1[0], cmpk_w2[0], cmpk_b2[0]),
                 (cmpv_pe[0], cmpv_w1[0], cmpv_b1[0], cmpv_w2[0], cmpv_b2[0]),
                 g_fox[0], g_nsa[0], w_out[0], g_mlp[0], w_up[0], w_down[0], g_final)
    return out.reshape(b, seq, d)
```

```python
import functools

import numpy as np
import jax
import jax.numpy as jnp
from jax import lax
from jax.experimental import pallas as pl
from jax.experimental.pallas import tpu as pltpu

F32 = jnp.float32
BF16 = jnp.bfloat16

HEAD_DIM = 64
N_HEADS = 8
N_KV = 2
GROUP = N_HEADS // N_KV
WIDTH = N_HEADS * HEAD_DIM
KV_WIDTH = N_KV * HEAD_DIM
CMP_LEN = 32
CMP_STRIDE = 16
CMP_HIDDEN = 256
SEL_LEN = 64
SEL_SHIFT = 6
SEL_TOPK = 16
SEL_SLOTS = 64
WINDOW = 512
ROPE_THETA = 10000.0
EPS = 1e-6
NEG = -1e30
MASK_FILL = -(2.0 ** 100)
FORCE_BONUS = 1e4
LOG2E = 1.4426950408889634
LANES = 128
HALF = LANES // 2
BF16_ROWS = 16

VMEM_LIMIT = 56 * 1024 * 1024


def _dot(a, b, precision=None):
    return jnp.dot(a, b, preferred_element_type=F32, precision=precision)


def _dot_nt(a, b, precision=None):
    return lax.dot_general(a, b, (((1,), (1,)), ((), ())), preferred_element_type=F32, precision=precision)


def _rms(x, g):
    return x * lax.rsqrt(jnp.mean(x * x, axis=-1, keepdims=True) + EPS) * g


def _rope(x, cos, sin_signed):
    w = x.shape[-1]
    lane = lax.broadcasted_iota(jnp.int32, (1, w), 1)
    first = (lane % HEAD_DIM) < (HEAD_DIM // 2)
    partner = jnp.where(first, pltpu.roll(x, w - HEAD_DIM // 2, 1), pltpu.roll(x, HEAD_DIM // 2, 1))
    return x * cos + partner * sin_signed


def _rope_t(x, cos, sin_signed):
    r = x.shape[0]
    row = lax.broadcasted_iota(jnp.int32, (r, 1), 0)
    first = (row % HEAD_DIM) < (HEAD_DIM // 2)
    partner = jnp.where(first, pltpu.roll(x, r - HEAD_DIM // 2, 0), pltpu.roll(x, HEAD_DIM // 2, 0))
    return x * cos + partner * sin_signed


def _split3(c):
    c1 = c.astype(BF16).astype(F32)
    r1 = c - c1
    c2 = r1.astype(BF16).astype(F32)
    return c1, c2, r1 - c2


NAT_KF, NAT_KC, NAT_VC, NAT_KS, NAT_KW, NAT_FG, NAT_COLS = 0, 512, 640, 768, 896, 1024, 1152
TR_QF, TR_VF, TR_QN, TR_VS, TR_VW, TR_FG, TR_ROWS = 0, 512, 1024, 1536, 1664, 1792, 1920


def _inproj_kernel(x_ref, g_ref, wn_ref, wt_ref, cos_ref, sin_ref, cost_ref, sint_ref,
                   kf_ref, kc_ref, vc_ref, ks_ref, kw_ref, fg_ref,
                   qft_ref, vft_ref, qnt_ref, vst_ref, vwt_ref, fgt_ref):
    hb = _rms(x_ref[...], g_ref[...]).astype(BF16)

    def nat(lo, width):
        return _dot(hb, wn_ref[:, lo:lo + width])

    def tr(lo, rows):
        return _dot_nt(wt_ref[lo:lo + rows, :], hb)

    cos = cos_ref[...]
    sin = sin_ref[...]
    kf_ref[...] = nat(NAT_KF, WIDTH).astype(BF16)
    kc_ref[...] = nat(NAT_KC, KV_WIDTH).astype(BF16)
    vc_ref[...] = nat(NAT_VC, KV_WIDTH).astype(BF16)
    ks_ref[...] = _rope(nat(NAT_KS, KV_WIDTH), cos, sin).astype(BF16)
    kw_ref[...] = _rope(nat(NAT_KW, KV_WIDTH), cos, sin).astype(BF16)
    fg_ref[...] = nat(NAT_FG, LANES)
    qft_ref[...] = tr(TR_QF, WIDTH).astype(BF16)
    vft_ref[...] = tr(TR_VF, WIDTH).astype(BF16)
    qnt_ref[...] = _rope_t(tr(TR_QN, WIDTH), cost_ref[...], sint_ref[...]).astype(BF16)
    vst_ref[...] = tr(TR_VS, KV_WIDTH).astype(BF16)
    vwt_ref[...] = tr(TR_VW, KV_WIDTH).astype(BF16)
    fgt_ref[...] = tr(TR_FG, LANES)


def _inproj(x2, g_attn, w_nat, w_tr, cos_n, sin_n, cos_t, sin_t, seq, tm):
    m, d = x2.shape
    sblocks = seq // tm
    row = lambda i: (i, 0)
    col = lambda i: (0, i)
    fixed = lambda i: (0, 0)
    tab = lambda i: (i % sblocks, 0)
    tab_t = lambda i: (0, i % sblocks)
    sds = jax.ShapeDtypeStruct
    out_shape = [sds((m, WIDTH), BF16)] + [sds((m, KV_WIDTH), BF16)] * 4 + [sds((m, LANES), F32),
                 sds((WIDTH, m), BF16), sds((WIDTH, m), BF16), sds((WIDTH, m), BF16),
                 sds((KV_WIDTH, m), BF16), sds((KV_WIDTH, m), BF16), sds((LANES, m), F32)]
    out_specs = ([pl.BlockSpec((tm, WIDTH), row)] + [pl.BlockSpec((tm, KV_WIDTH), row)] * 4
                 + [pl.BlockSpec((tm, LANES), row)]
                 + [pl.BlockSpec((WIDTH, tm), col)] * 3 + [pl.BlockSpec((KV_WIDTH, tm), col)] * 2
                 + [pl.BlockSpec((LANES, tm), col)])
    return pl.pallas_call(
        _inproj_kernel,
        grid=(m // tm,),
        in_specs=[pl.BlockSpec((tm, d), row), pl.BlockSpec((1, d), fixed),
                  pl.BlockSpec(w_nat.shape, fixed), pl.BlockSpec(w_tr.shape, fixed),
                  pl.BlockSpec((tm, KV_WIDTH), tab), pl.BlockSpec((tm, KV_WIDTH), tab),
                  pl.BlockSpec((WIDTH, tm), tab_t), pl.BlockSpec((WIDTH, tm), tab_t)],
        out_specs=out_specs,
        out_shape=out_shape,
        compiler_params=pltpu.CompilerParams(dimension_semantics=("parallel",), vmem_limit_bytes=VMEM_LIMIT),
        name="inproj",
    )(x2, g_attn, w_nat, w_tr, cos_n, sin_n, cos_t, sin_t)


CUM_BLOCK = 256
N_BIAS = 3


def _bias_lane_base(h):
    return HALF if h % 2 == 0 else 0


def _cumgate_kernel(fg_ref, bf_ref, kf_ref, crow_ref, kaug_ref):
    seq = fg_ref.shape[1]
    r = lax.broadcasted_iota(jnp.int32, (CUM_BLOCK, CUM_BLOCK), 0)
    c = lax.broadcasted_iota(jnp.int32, (CUM_BLOCK, CUM_BLOCK), 1)
    tri = jnp.where(r >= c, 1.0, 0.0).astype(F32)
    er = lax.broadcasted_iota(jnp.int32, (N_HEADS, LANES), 0)
    ec = lax.broadcasted_iota(jnp.int32, (N_HEADS, LANES), 1)
    pick = jnp.where(er == ec, 1.0, 0.0).astype(F32)
    bias = bf_ref[...]
    lane = lax.broadcasted_iota(jnp.int32, (CUM_BLOCK, LANES), 1)

    def body(blk, carry):
        off = pl.multiple_of(blk * CUM_BLOCK, CUM_BLOCK)
        z = fg_ref[0, pl.ds(off, CUM_BLOCK), :] + bias
        logf = jnp.minimum(z, 0.0) - jnp.log(1.0 + jnp.exp(-jnp.abs(z)))
        cs = _dot(tri, logf, precision=lax.Precision.HIGHEST) + carry
        c2 = cs * LOG2E
        crow_ref[0, :, pl.ds(off, CUM_BLOCK)] = _dot_nt(pick, c2, precision=lax.Precision.HIGHEST)
        for h in range(N_HEADS):
            c1, cb, cc = _split3(jnp.broadcast_to(c2[:, h:h + 1], (CUM_BLOCK, LANES)))
            d = lane - _bias_lane_base(h)
            feat = jnp.where(d == N_BIAS, -c1, jnp.where(d == N_BIAS + 1, -cb, jnp.where(d == N_BIAS + 2, -cc,
                   jnp.where(d == 0, 1.0, jnp.where(d == 1, 1.0, jnp.where(d == 2, 1.0, 0.0))))))
            mine = (lane < HALF) if h % 2 == 0 else (lane >= HALF)
            kcol = kf_ref[0, pl.ds(off, CUM_BLOCK), (h // 2) * LANES:(h // 2 + 1) * LANES]
            kaug_ref[0, h, pl.ds(off, CUM_BLOCK), :] = jnp.where(mine, kcol, feat.astype(BF16))
        return cs[CUM_BLOCK - 1:CUM_BLOCK, :]

    lax.fori_loop(0, seq // CUM_BLOCK, body, jnp.zeros((1, LANES), F32))


def _cumgate(fg3, bf_pad, kf3):
    b, seq, _ = fg3.shape
    return pl.pallas_call(
        _cumgate_kernel,
        grid=(b,),
        in_specs=[pl.BlockSpec((1, seq, LANES), lambda i: (i, 0, 0)), pl.BlockSpec((1, LANES), lambda i: (0, 0)),
                  pl.BlockSpec((1, seq, WIDTH), lambda i: (i, 0, 0))],
        out_specs=[pl.BlockSpec((1, N_HEADS, seq), lambda i: (i, 0, 0)),
                   pl.BlockSpec((1, N_HEADS, seq, LANES), lambda i: (i, 0, 0, 0))],
        out_shape=[jax.ShapeDtypeStruct((b, N_HEADS, seq), F32),
                   jax.ShapeDtypeStruct((b, N_HEADS, seq, LANES), BF16)],
        compiler_params=pltpu.CompilerParams(dimension_semantics=("parallel",), vmem_limit_bytes=VMEM_LIMIT),
        name="cumgate",
    )(fg3, bf_pad, kf3)


def _compress_one(x_ref, pea_ref, peb_ref, wa_ref, wb_ref, b1_ref, w2_ref, b2_ref):
    xk = x_ref[0]
    n = xk.shape[0]
    wa = wa_ref[...]
    wb = wb_ref[...]
    first = _dot(xk, wa)
    second = _dot(xk, wb)
    pe_term = _dot(pea_ref[...].astype(BF16), wa) + _dot(peb_ref[...].astype(BF16), wb)
    hidden = first + pltpu.roll(second, n - 1, 0) + pe_term[0:1, :] + b1_ref[...]
    act = jax.nn.gelu(hidden)
    return _dot(act.astype(BF16), w2_ref[...]) + b2_ref[...]


def _compress_kernel(kc_ref, vc_ref, cos_ref, sin_ref,
                     kpea, kpeb, kwa, kwb, kb1, kw2, kb2,
                     vpea, vpeb, vwa, vwb, vb1, vw2, vb2,
                     kcmp_ref, vcmpt_ref):
    kcmp = _compress_one(kc_ref, kpea, kpeb, kwa, kwb, kb1, kw2, kb2)
    kcmp_ref[0] = _rope(kcmp, cos_ref[...], sin_ref[...]).astype(BF16)
    vcmpt_ref[0] = _compress_one(vc_ref, vpea, vpeb, vwa, vwb, vb1, vw2, vb2).T.astype(BF16)


def _compress(kc3, vc3, cos_c, sin_c, kparams, vparams):
    b, n, w = kc3.shape
    x_spec = pl.BlockSpec((1, n, w), lambda i: (i, 0, 0))
    full = lambda a: pl.BlockSpec(a.shape, lambda i: (0,) * a.ndim)
    params = list(kparams) + list(vparams)
    return pl.pallas_call(
        _compress_kernel,
        grid=(b,),
        in_specs=[x_spec, x_spec, full(cos_c), full(sin_c)] + [full(p) for p in params],
        out_specs=[pl.BlockSpec((1, n, KV_WIDTH), lambda i: (i, 0, 0)),
                   pl.BlockSpec((1, KV_WIDTH, n), lambda i: (i, 0, 0))],
        out_shape=[jax.ShapeDtypeStruct((b, n, KV_WIDTH), BF16), jax.ShapeDtypeStruct((b, KV_WIDTH, n), BF16)],
        compiler_params=pltpu.CompilerParams(dimension_semantics=("parallel",), vmem_limit_bytes=VMEM_LIMIT),
        name="compress",
    )(kc3, vc3, cos_c, sin_c, *params)


ACC_ROWS = HEAD_DIM + BF16_ROWS


def _flash_step_t(s, vt, m_ref, acc_ref, idx):
    row = slice(idx, idx + 1)
    m_old = m_ref[row, :]
    m_new = jnp.maximum(m_old, jnp.max(s, axis=0, keepdims=True))
    alpha = jnp.exp2(m_old - m_new)
    p = jnp.exp2((s - m_new).astype(BF16))
    m_ref[row, :] = m_new
    vt_ones = jnp.concatenate([vt, jnp.ones((BF16_ROWS, vt.shape[1]), BF16)], axis=0)
    acc_ref[idx] = alpha * acc_ref[idx] + _dot(vt_ones, p)


def _flash_reset(m_ref, acc_ref):
    m_ref[...] = jnp.full(m_ref.shape, NEG, F32)
    acc_ref[...] = jnp.zeros(acc_ref.shape, F32)


def _flash_finish(acc_ref, idx):
    acc = acc_ref[idx]
    return acc[0:HEAD_DIM] * (1.0 / acc[HEAD_DIM:HEAD_DIM + 1])


def _for_tiles(n, unroll, visit):
    groups = n // unroll

    def body(jj, carry):
        visit(jj * unroll, unroll)
        return carry

    lax.fori_loop(0, groups, body, 0)
    done = groups * unroll
    rem = n - done
    p = unroll // 2
    while p >= 1:
        first = done + (rem & ~(2 * p - 1))

        @pl.when((rem & p) != 0)
        def _(first=first, p=p):
            visit(first, p)

        p //= 2


FOX_TILE = 256
FOX_KEYS = 256
FOX_AHEAD = 4
FOX_UNROLL = 4


def _fox_kernel(qt_ref, kaug_ref, vt_ref, crow_ref, o_ref, qaug_ref, m_ref, acc_ref):
    t = FOX_TILE
    tk = FOX_KEYS
    per_q = t // tk
    i = pl.program_id(1)
    kr = lax.broadcasted_iota(jnp.int32, (tk, t), 0)
    qc = lax.broadcasted_iota(jnp.int32, (tk, t), 1)
    frow = lax.broadcasted_iota(jnp.int32, (BF16_ROWS, t), 0)
    pad = jnp.zeros((HALF - BF16_ROWS, t), BF16)

    _flash_reset(m_ref, acc_ref)
    for h in range(N_HEADS):
        c1, c2, c3 = _split3(crow_ref[0, h:h + 1, :])
        feat = jnp.where(frow == 0, c1, jnp.where(frow == 1, c2, jnp.where(frow == 2, c3,
               jnp.where(frow < 2 * N_BIAS, 1.0, 0.0)))).astype(BF16)
        qh = qt_ref[h * HEAD_DIM:(h + 1) * HEAD_DIM, :]
        parts = [qh, feat, pad] if _bias_lane_base(h) == HALF else [feat, pad, qh]
        qaug_ref[h] = jnp.concatenate(parts, axis=0)

    def step(tiles):
        chains = [(off, mask, h) for off, mask in tiles for h in range(N_HEADS)]

        def scores(c):
            off, _, h = chains[c]
            return _dot(kaug_ref[0, h, pl.ds(off, tk), :], qaug_ref[h])

        pending = {c: scores(c) for c in range(FOX_AHEAD)}
        for c, (off, mask, h) in enumerate(chains):
            if c + FOX_AHEAD < len(chains):
                pending[c + FOX_AHEAD] = scores(c + FOX_AHEAD)
            s = pending.pop(c)
            if mask is not None:
                s = jnp.where(mask, s, MASK_FILL)
            _flash_step_t(s, vt_ref[h * HEAD_DIM:(h + 1) * HEAD_DIM, pl.ds(off, tk)], m_ref, acc_ref, h)

    def tile_off(j):
        return pl.multiple_of(j * tk, tk)

    n_before = i * per_q
    _for_tiles(n_before, FOX_UNROLL, lambda first, count: step([(tile_off(first + u), None) for u in range(count)]))
    step([(tile_off(n_before + u), kr + u * tk <= qc) for u in range(per_q)])

    for col in range(WIDTH // LANES):
        pair = [_flash_finish(acc_ref, h) for h in (2 * col, 2 * col + 1)]
        o_ref[0, :, col * LANES:(col + 1) * LANES] = jnp.concatenate(pair, axis=0).T.astype(BF16)


def _fox(qft, kaug, vft, crow):
    b, _, seq, _ = kaug.shape
    t = FOX_TILE
    nq = seq // t
    return pl.pallas_call(
        _fox_kernel,
        grid=(b, nq),
        in_specs=[pl.BlockSpec((WIDTH, t), lambda bi, i: (0, bi * nq + i)),
                  pl.BlockSpec((1, N_HEADS, seq, LANES), lambda bi, i: (bi, 0, 0, 0)),
                  pl.BlockSpec((WIDTH, seq), lambda bi, i: (0, bi)),
                  pl.BlockSpec((1, N_HEADS, t), lambda bi, i: (bi, 0, i))],
        out_specs=pl.BlockSpec((1, t, WIDTH), lambda bi, i: (bi, i, 0)),
        out_shape=jax.ShapeDtypeStruct((b, seq, WIDTH), BF16),
        scratch_shapes=[pltpu.VMEM((N_HEADS, LANES, t), BF16), pltpu.VMEM((N_HEADS, t), F32),
                        pltpu.VMEM((N_HEADS, ACC_ROWS, t), F32)],
        compiler_params=pltpu.CompilerParams(dimension_semantics=("parallel", "arbitrary"),
                                             vmem_limit_bytes=VMEM_LIMIT),
        name="fox",
    )(qft, kaug, vft, crow)


NSA_TILE = 256
NSA_KEYS = 128
STAT_ROWS = 8
NSA_UNROLL = 8
NSA_AHEAD = 4


def _nsa_kernel(qt_ref, kcmp_ref, vcmpt_ref, ks_ref, vst_ref, kw_ref, vwt_ref, fgt_ref, bg_ref, ovl_ref, o_ref,
                qaug_ref, m_ref, acc_ref, *, topk):
    t = NSA_TILE
    tk = NSA_KEYS
    per_q = t // tk
    wide = GROUP * t
    n_cmp = kcmp_ref.shape[1]
    i = pl.program_id(1)
    start = i * t
    lane = lax.broadcasted_iota(jnp.int32, (1, LANES), 1)
    mine = [lane < HALF, lane >= HALF]
    zero_bf = jnp.zeros((), BF16)

    q4 = jnp.concatenate([qt_ref[n * LANES:(n + 1) * LANES, :] for n in range(GROUP)], axis=1)
    qpos4 = start + (lax.broadcasted_iota(jnp.int32, (1, wide), 1) & (t - 1))
    qpos = start + lax.broadcasted_iota(jnp.int32, (1, t), 1)

    cmp_end = lax.broadcasted_iota(jnp.int32, (n_cmp, 1), 0) * CMP_STRIDE + (CMP_LEN - 1)
    vis = cmp_end <= qpos4
    slot = lax.broadcasted_iota(jnp.int32, (SEL_SLOTS, t), 0)
    cur = qpos >> SEL_SHIFT
    valid = slot <= cur
    forced = (slot == 0) | (slot == cur) | (slot == cur - 1)
    sub8 = lax.broadcasted_iota(jnp.int32, (STAT_ROWS, t), 0)
    o_cmp = []
    for g in range(N_KV):
        s = jnp.where(vis, _dot(jnp.where(mine[g], kcmp_ref[0], zero_bf), q4), NEG)
        e = jnp.where(vis, jnp.exp2(s - jnp.max(s, axis=0, keepdims=True)), 0.0)
        denom = jnp.sum(e, axis=0, keepdims=True)
        p = e * jnp.where(denom > 0.0, 1.0 / denom, 0.0)
        o_cmp.append(_dot(vcmpt_ref[0, g * HEAD_DIM:(g + 1) * HEAD_DIM, :], p.astype(BF16)))
        psum = p[:, 0:t]
        for n in range(1, GROUP):
            psum = psum + p[:, n * t:(n + 1) * t]
        imp = _dot(ovl_ref[...], psum, precision=lax.Precision.HIGHEST)
        score = jnp.where(valid, imp + jnp.where(forced, FORCE_BONUS, 0.0), -1.0)
        tiles = [score[r:r + STAT_ROWS, :] for r in range(0, SEL_SLOTS, STAT_ROWS)]
        ranks = [jnp.zeros((STAT_ROWS, t), F32) for _ in tiles]
        for jp in range(SEL_SLOTS):
            other = jnp.broadcast_to(score[jp:jp + 1, :], (STAT_ROWS, t))
            for k, tile in enumerate(tiles):
                first = k * STAT_ROWS
                ge = jnp.where(other >= tile, 1.0, 0.0)
                gt = jnp.where(other > tile, 1.0, 0.0)
                if first > jp:
                    ahead = ge
                elif first + STAT_ROWS - 1 <= jp:
                    ahead = gt
                else:
                    ahead = jnp.where(sub8 + first > jp, ge, gt)
                ranks[k] = ranks[k] + ahead
        rank = jnp.concatenate(ranks, axis=0)
        selb = jnp.where(rank < topk, 0.0, MASK_FILL).astype(BF16)
        selb4 = jnp.concatenate([selb] * GROUP, axis=1)
        qaug_ref[g] = jnp.concatenate([q4[0:HALF], selb4] if g == 0 else [selb4, q4[HALF:LANES]], axis=0)

    kr = lax.broadcasted_iota(jnp.int32, (tk, wide), 0)
    q_local = lax.broadcasted_iota(jnp.int32, (tk, wide), 1) & (t - 1)
    key_row = lax.broadcasted_iota(jnp.int32, (tk, LANES), 0)
    lane_full = lax.broadcasted_iota(jnp.int32, (tk, LANES), 1)
    slot_minus_row = [lane_full - HALF - (key_row >> SEL_SHIFT), lane_full - (key_row >> SEL_SHIFT)]

    def sel_score(off, g):
        onehot = jnp.where(slot_minus_row[g] == (off >> SEL_SHIFT), 1.0, 0.0).astype(BF16)
        return _dot(jnp.where(mine[g], ks_ref[0, pl.ds(off, tk), :], onehot), qaug_ref[g])

    def win_score(off, g):
        return _dot(jnp.where(mine[g], kw_ref[0, pl.ds(off, tk), :], zero_bf), q4)

    def sel_chains(tiles):
        return [(sel_score, vst_ref, g, g, off, mask) for off, mask in tiles for g in range(N_KV)]

    def win_chains(tiles):
        return [(win_score, vwt_ref, g, N_KV + g, off, mask) for off, mask in tiles for g in range(N_KV)]

    def sweep(chains):
        def issue(c):
            score, _, g, _, off, _ = chains[c]
            return score(off, g)

        pending = {c: issue(c) for c in range(min(NSA_AHEAD, len(chains)))}
        for c, (_, vt_ref_, g, slot_id, off, mask) in enumerate(chains):
            if c + NSA_AHEAD < len(chains):
                pending[c + NSA_AHEAD] = issue(c + NSA_AHEAD)
            s = pending.pop(c)
            if mask is not None:
                s = jnp.where(mask, s, MASK_FILL)
            _flash_step_t(s, vt_ref_[g * HEAD_DIM:(g + 1) * HEAD_DIM, pl.ds(off, tk)], m_ref, acc_ref, slot_id)

    def tile_off(j):
        return pl.multiple_of(j * tk, tk)

    n_before = i * per_q
    _flash_reset(m_ref, acc_ref)
    _for_tiles(n_before, NSA_UNROLL, lambda first, count: sweep(
        sel_chains([(tile_off(first + u), None) for u in range(count)])))

    own = [(tile_off(n_before + u), kr + u * tk <= q_local) for u in range(per_q)]
    n_back = WINDOW // tk

    @pl.when(n_before >= n_back)
    def _():
        back = [(tile_off(n_before - n_back + u), (q_local - kr < u * tk) if u < per_q else None)
                for u in range(n_back)]
        sweep(sel_chains(own) + win_chains(back + own))

    @pl.when(n_before < n_back)
    def _():
        sweep(sel_chains(own))

        def win_body(j, carry):
            sweep(win_chains([(tile_off(j), None)]))
            return carry

        lax.fori_loop(0, n_before, win_body, 0)
        sweep(win_chains(own))

    o_slc = [_flash_finish(acc_ref, g) for g in range(N_KV)]
    o_win = [_flash_finish(acc_ref, N_KV + g) for g in range(N_KV)]

    gates = jax.nn.sigmoid(fgt_ref[...] + bg_ref[...])
    for n in range(GROUP):
        cols = slice(n * t, (n + 1) * t)
        mixed = []
        for g in range(N_KV):
            base = N_HEADS + (g * GROUP + n) * 3
            mixed.append(gates[base:base + 1, :] * o_cmp[g][:, cols]
                         + gates[base + 1:base + 2, :] * o_slc[g][:, cols]
                         + gates[base + 2:base + 3, :] * o_win[g][:, cols])
        o_ref[0, :, n * LANES:(n + 1) * LANES] = jnp.concatenate(mixed, axis=0).T.astype(BF16)


def _nsa(qnt, kcmp, vcmpt, ks, vst, kw, vwt, fgt, bg_t, ovl_t, topk):
    b, seq, _ = ks.shape
    t = NSA_TILE
    nq = seq // t
    n_cmp = kcmp.shape[1]
    whole = lambda bi, i: (bi, 0, 0)
    whole_t = lambda bi, i: (0, bi)
    tile_t = lambda bi, i: (0, bi * nq + i)
    fixed = lambda bi, i: (0, 0)
    return pl.pallas_call(
        functools.partial(_nsa_kernel, topk=topk),
        grid=(b, nq),
        in_specs=[pl.BlockSpec((WIDTH, t), tile_t),
                  pl.BlockSpec((1, n_cmp, KV_WIDTH), whole), pl.BlockSpec((1, KV_WIDTH, n_cmp), whole),
                  pl.BlockSpec((1, seq, KV_WIDTH), whole), pl.BlockSpec((KV_WIDTH, seq), whole_t),
                  pl.BlockSpec((1, seq, KV_WIDTH), whole), pl.BlockSpec((KV_WIDTH, seq), whole_t),
                  pl.BlockSpec((LANES, t), tile_t), pl.BlockSpec((LANES, t), fixed),
                  pl.BlockSpec(ovl_t.shape, fixed)],
        out_specs=pl.BlockSpec((1, t, WIDTH), lambda bi, i: (bi, i, 0)),
        out_shape=jax.ShapeDtypeStruct((b, seq, WIDTH), BF16),
        scratch_shapes=[pltpu.VMEM((N_KV, LANES, GROUP * t), BF16), pltpu.VMEM((STAT_ROWS, GROUP * t), F32),
                        pltpu.VMEM((2 * N_KV, ACC_ROWS, GROUP * t), F32)],
        compiler_params=pltpu.CompilerParams(dimension_semantics=("parallel", "arbitrary"),
                                             vmem_limit_bytes=VMEM_LIMIT),
        name="nsa",
    )(qnt, kcmp, vcmpt, ks, vst, kw, vwt, fgt, bg_t, ovl_t)


def _post_kernel(x_ref, of_ref, on_ref, gf_ref, gn_ref, wo_ref, gm_ref, wu_ref, wd_ref, gl_ref, o_ref):
    yf = _rms(of_ref[...].astype(F32), gf_ref[...]).astype(BF16)
    yn = _rms(on_ref[...].astype(F32), gn_ref[...]).astype(BF16)
    h1 = x_ref[...] + _dot(yf, wo_ref[0:WIDTH, :]) + _dot(yn, wo_ref[WIDTH:2 * WIDTH, :])
    u = _dot(_rms(h1, gm_ref[...]).astype(BF16), wu_ref[...])
    act = jnp.square(jnp.maximum(u, 0.0)).astype(BF16)
    h2 = h1 + _dot(act, wd_ref[...])
    o_ref[...] = _rms(h2, gl_ref[...])


def _post(x2, ofox, onsa, g_fox, g_nsa, w_out, g_mlp, w_up, w_down, g_final, tm):
    m, d = x2.shape
    row = lambda i: (i, 0)
    fixed = lambda i: (0, 0)
    full = lambda a: pl.BlockSpec(a.shape, fixed, pipeline_mode=pl.Buffered(1))
    return pl.pallas_call(
        _post_kernel,
        grid=(m // tm,),
        in_specs=[pl.BlockSpec((tm, d), row), pl.BlockSpec((tm, WIDTH), row), pl.BlockSpec((tm, WIDTH), row),
                  full(g_fox), full(g_nsa), full(w_out), full(g_mlp), full(w_up), full(w_down), full(g_final)],
        out_specs=pl.BlockSpec((tm, d), row),
        out_shape=jax.ShapeDtypeStruct((m, d), F32),
        compiler_params=pltpu.CompilerParams(dimension_semantics=("parallel",), vmem_limit_bytes=VMEM_LIMIT),
        name="post",
    )(x2, ofox, onsa, g_fox, g_nsa, w_out, g_mlp, w_up, w_down, g_final)


_NSA_PERM = np.array([(p % 2) * GROUP + p // 2 for p in range(N_HEADS)])


def _perm_heads(a, axis):
    shape = a.shape
    a = a.reshape(shape[:axis] + (N_HEADS, HEAD_DIM) + shape[axis + 1:])
    a = jnp.take(a, _NSA_PERM, axis=axis)
    return a.reshape(shape)


def _prep_w_in(w):
    qscale = HEAD_DIM ** -0.5 * LOG2E
    sizes = [WIDTH, WIDTH, WIDTH, N_HEADS, WIDTH] + [KV_WIDTH] * 6 + [3 * N_HEADS]
    offs = np.cumsum([0] + sizes)
    qf, kf, vf, fl, qn, kc, vc, ks, vs, kw, vw, gate = [w[:, offs[k]:offs[k + 1]] for k in range(len(sizes))]
    fg = jnp.concatenate([fl, gate, jnp.zeros((w.shape[0], LANES - 4 * N_HEADS), w.dtype)], axis=1)
    w_nat = jnp.concatenate([kf, kc, vc, ks, kw, fg], axis=1).astype(BF16)
    w_tr = jnp.concatenate([qf * qscale, vf, _perm_heads(qn * qscale, 1), vs, vw, fg], axis=1).T.astype(BF16)
    assert w_nat.shape[1] == NAT_COLS and w_tr.shape[0] == TR_ROWS
    return w_nat, w_tr


def _prep_compress(pe, w1, b1, w2, b2):
    half = CMP_LEN // 2
    eye = jnp.eye(N_KV, dtype=F32)

    def expand_w1(wpart):
        w3 = wpart.reshape(half, HEAD_DIM, CMP_HIDDEN)
        return jnp.einsum('ldh,gk->lgdkh', w3, eye).reshape(half * KV_WIDTH, N_KV * CMP_HIDDEN).astype(BF16)

    def expand_pe(ppart):
        flat = jnp.broadcast_to(ppart[:, None, :], (half, N_KV, HEAD_DIM)).reshape(1, half * KV_WIDTH)
        return jnp.broadcast_to(flat, (8, half * KV_WIDTH))

    w2b = jnp.einsum('hd,gk->ghkd', w2, eye).reshape(N_KV * CMP_HIDDEN, KV_WIDTH).astype(BF16)
    return (expand_pe(pe[:half]), expand_pe(pe[half:]),
            expand_w1(w1[:half * HEAD_DIM]), expand_w1(w1[half * HEAD_DIM:]),
            jnp.tile(b1, N_KV)[None, :], w2b, jnp.tile(b2, N_KV)[None, :])


def _rope_tables(pos, reps):
    half = HEAD_DIM // 2
    inv = ROPE_THETA ** (-jnp.arange(half, dtype=F32) / half)
    ang = pos.astype(F32)[:, None] * inv[None, :]
    cos = jnp.cos(ang)
    sin = jnp.sin(ang)
    return (jnp.tile(jnp.concatenate([cos, cos], axis=1), (1, reps)),
            jnp.tile(jnp.concatenate([-sin, sin], axis=1), (1, reps)))


def _overlap_t(n_cmp_slots, n_cmp, n_blocks):
    c = np.arange(n_cmp_slots)[None, :] * CMP_STRIDE
    j = np.arange(SEL_SLOTS)[:, None]
    s = j * SEL_LEN
    ovl = (c < s + SEL_LEN) & (c + CMP_LEN > s) & (np.arange(n_cmp_slots)[None, :] < n_cmp) & (j < n_blocks)
    return jnp.asarray(ovl.astype(np.float32))


def _row_tile(m, want):
    t = want
    while m % t:
        t //= 2
    return t


def _layer(h, seq, g_attn, w_in, b_f, b_gate, cmpk, cmpv, g_fox, g_nsa, w_out, g_mlp, w_up, w_down, g_out):
    m, d = h.shape
    b = m // seq
    n_chunks = seq // CMP_STRIDE
    n_cmp = (seq - CMP_LEN) // CMP_STRIDE + 1
    n_blocks = seq // SEL_LEN
    assert seq % FOX_TILE == 0 and seq % NSA_TILE == 0 and n_chunks % LANES == 0
    assert SEL_TOPK <= n_blocks <= SEL_SLOTS
    assert n_cmp == n_chunks - 1

    cos_n, sin_n = _rope_tables(jnp.arange(seq), N_KV)
    cos_c, sin_c = _rope_tables(jnp.arange(n_chunks) * CMP_STRIDE + CMP_LEN - 1, N_KV)
    cos_t = jnp.tile(cos_n, (1, GROUP)).T
    sin_t = jnp.tile(sin_n, (1, GROUP)).T

    tm = _row_tile(seq, 512)
    w_nat, w_tr = _prep_w_in(w_in)
    kf, kc, vc, ks, kw, fg, qft, vft, qnt, vst, vwt, fgt = _inproj(
        h, g_attn[None, :], w_nat, w_tr, cos_n, sin_n, cos_t, sin_t, seq, tm)
    r3 = lambda a: a.reshape(b, seq, a.shape[-1])

    bf_pad = jnp.zeros((1, LANES), F32).at[0, :N_HEADS].set(b_f)
    bg_col = jnp.zeros((LANES,), F32).at[N_HEADS:4 * N_HEADS].set(b_gate)
    bg_t = jnp.broadcast_to(bg_col[:, None], (LANES, NSA_TILE))
    crow, kaug = _cumgate(r3(fg), bf_pad, r3(kf))

    chunked = lambda a: a.reshape(b, n_chunks, CMP_STRIDE * KV_WIDTH)
    kcmp, vcmpt = _compress(chunked(kc), chunked(vc), cos_c, sin_c, _prep_compress(*cmpk), _prep_compress(*cmpv))

    ofox = _fox(qft, kaug, vft, crow)
    onsa = _nsa(qnt, kcmp, vcmpt, r3(ks), vst, r3(kw), vwt, fgt, bg_t,
                _overlap_t(n_chunks, n_cmp, n_blocks), min(SEL_TOPK, n_blocks))

    w_out_p = jnp.concatenate([w_out[:WIDTH], _perm_heads(w_out[WIDTH:], 0)], axis=0).astype(BF16)
    return _post(h, ofox.reshape(m, WIDTH), onsa.reshape(m, WIDTH), g_fox[None, :], _perm_heads(g_nsa, 0)[None, :],
                 w_out_p, g_mlp[None, :], w_up.astype(BF16), w_down.astype(BF16), g_out[None, :],
                 _row_tile(seq, 256))


def kernel(x, g_attn, w_in, b_f, b_gate, cmpk_pe, cmpk_w1, cmpk_b1, cmpk_w2, cmpk_b2, cmpv_pe, cmpv_w1, cmpv_b1,
           cmpv_w2, cmpv_b2, g_fox, g_nsa, w_out, g_mlp, w_up, w_down, g_final):
    b, seq, d = x.shape
    depth = g_attn.shape[0]
    assert depth == 1, "the final rmsnorm is fused into the (single) layer's last kernel"
    h = x.reshape(b * seq, d)
    out = _layer(h, seq, g_attn[0], w_in[0], b_f[0], b_gate[0],
                 (cmpk_pe[0], cmpk_w1[0], cmpk_b1[0], cmpk_w2[0], cmpk_b2[0]),
                 (cmpv_pe[0], cmpv_w1[0], cmpv_b1[0], cmpv_w2[0], cmpv_b2[0]),
                 g_fox[0], g_nsa[0], w_out[0], g_mlp[0], w_up[0], w_down[0], g_final)
    return out.reshape(b, seq, d)
```

```python
import functools

import numpy as np
import jax
import jax.numpy as jnp
from jax import lax
from jax.experimental import pallas as pl
from jax.experimental.pallas import tpu as pltpu

F32 = jnp.float32
BF16 = jnp.bfloat16

HEAD_DIM = 64
N_HEADS = 8
N_KV = 2
GROUP = N_HEADS // N_KV
WIDTH = N_HEADS * HEAD_DIM
KV_WIDTH = N_KV * HEAD_DIM
CMP_LEN = 32
CMP_STRIDE = 16
CMP_HIDDEN = 256
SEL_LEN = 64
SEL_SHIFT = 6
SEL_TOPK = 16
SEL_SLOTS = 64
WINDOW = 512
ROPE_THETA = 10000.0
EPS = 1e-6
NEG = -1e30
MASK_FILL = -(2.0 ** 100)
FORCE_BONUS = 1e4
LOG2E = 1.4426950408889634
LANES = 128
HALF = LANES // 2
BF16_ROWS = 16

VMEM_LIMIT = 56 * 1024 * 1024


def _dot(a, b, precision=None):
    return jnp.dot(a, b, preferred_element_type=F32, precision=precision)


def _dot_nt(a, b, precision=None):
    return lax.dot_general(a, b, (((1,), (1,)), ((), ())), preferred_element_type=F32, precision=precision)


def _rms(x, g):
    return x * lax.rsqrt(jnp.mean(x * x, axis=-1, keepdims=True) + EPS) * g


def _rope(x, cos, sin_signed):
    w = x.shape[-1]
    lane = lax.broadcasted_iota(jnp.int32, (1, w), 1)
    first = (lane % HEAD_DIM) < (HEAD_DIM // 2)
    partner = jnp.where(first, pltpu.roll(x, w - HEAD_DIM // 2, 1), pltpu.roll(x, HEAD_DIM // 2, 1))
    return x * cos + partner * sin_signed


def _rope_t(x, cos, sin_signed):
    r = x.shape[0]
    row = lax.broadcasted_iota(jnp.int32, (r, 1), 0)
    first = (row % HEAD_DIM) < (HEAD_DIM // 2)
    partner = jnp.where(first, pltpu.roll(x, r - HEAD_DIM // 2, 0), pltpu.roll(x, HEAD_DIM // 2, 0))
    return x * cos + partner * sin_signed


def _split3(c):
    c1 = c.astype(BF16).astype(F32)
    r1 = c - c1
    c2 = r1.astype(BF16).astype(F32)
    return c1, c2, r1 - c2


NAT_KF, NAT_KC, NAT_VC, NAT_KS, NAT_KW, NAT_FG, NAT_COLS = 0, 512, 640, 768, 896, 1024, 1152
TR_QF, TR_VF, TR_QN, TR_VS, TR_VW, TR_FG, TR_ROWS = 0, 512, 1024, 1536, 1664, 1792, 1920


def _inproj_kernel(x_ref, g_ref, wn_ref, wt_ref, cos_ref, sin_ref, cost_ref, sint_ref,
                   kf_ref, kc_ref, vc_ref, ks_ref, kw_ref, fg_ref,
                   qft_ref, vft_ref, qnt_ref, vst_ref, vwt_ref, fgt_ref):
    hb = _rms(x_ref[...], g_ref[...]).astype(BF16)

    def nat(lo, width):
        return _dot(hb, wn_ref[:, lo:lo + width])

    def tr(lo, rows):
        return _dot_nt(wt_ref[lo:lo + rows, :], hb)

    cos = cos_ref[...]
    sin = sin_ref[...]
    kf_ref[...] = nat(NAT_KF, WIDTH).astype(BF16)
    kc_ref[...] = nat(NAT_KC, KV_WIDTH).astype(BF16)
    vc_ref[...] = nat(NAT_VC, KV_WIDTH).astype(BF16)
    ks_ref[...] = _rope(nat(NAT_KS, KV_WIDTH), cos, sin).astype(BF16)
    kw_ref[...] = _rope(nat(NAT_KW, KV_WIDTH), cos, sin).astype(BF16)
    fg_ref[...] = nat(NAT_FG, LANES)
    qft_ref[...] = tr(TR_QF, WIDTH).astype(BF16)
    vft_ref[...] = tr(TR_VF, WIDTH).astype(BF16)
    qnt_ref[...] = _rope_t(tr(TR_QN, WIDTH), cost_ref[...], sint_ref[...]).astype(BF16)
    vst_ref[...] = tr(TR_VS, KV_WIDTH).astype(BF16)
    vwt_ref[...] = tr(TR_VW, KV_WIDTH).astype(BF16)
    fgt_ref[...] = tr(TR_FG, LANES)


def _inproj(x2, g_attn, w_nat, w_tr, cos_n, sin_n, cos_t, sin_t, seq, tm):
    m, d = x2.shape
    sblocks = seq // tm
    row = lambda i: (i, 0)
    col = lambda i: (0, i)
    fixed = lambda i: (0, 0)
    tab = lambda i: (i % sblocks, 0)
    tab_t = lambda i: (0, i % sblocks)
    sds = jax.ShapeDtypeStruct
    out_shape = [sds((m, WIDTH), BF16)] + [sds((m, KV_WIDTH), BF16)] * 4 + [sds((m, LANES), F32),
                 sds((WIDTH, m), BF16), sds((WIDTH, m), BF16), sds((WIDTH, m), BF16),
                 sds((KV_WIDTH, m), BF16), sds((KV_WIDTH, m), BF16), sds((LANES, m), F32)]
    out_specs = ([pl.BlockSpec((tm, WIDTH), row)] + [pl.BlockSpec((tm, KV_WIDTH), row)] * 4
                 + [pl.BlockSpec((tm, LANES), row)]
                 + [pl.BlockSpec((WIDTH, tm), col)] * 3 + [pl.BlockSpec((KV_WIDTH, tm), col)] * 2
                 + [pl.BlockSpec((LANES, tm), col)])
    return pl.pallas_call(
        _inproj_kernel,
        grid=(m // tm,),
        in_specs=[pl.BlockSpec((tm, d), row), pl.BlockSpec((1, d), fixed),
                  pl.BlockSpec(w_nat.shape, fixed), pl.BlockSpec(w_tr.shape, fixed),
                  pl.BlockSpec((tm, KV_WIDTH), tab), pl.BlockSpec((tm, KV_WIDTH), tab),
                  pl.BlockSpec((WIDTH, tm), tab_t), pl.BlockSpec((WIDTH, tm), tab_t)],
        out_specs=out_specs,
        out_shape=out_shape,
        compiler_params=pltpu.CompilerParams(dimension_semantics=("parallel",), vmem_limit_bytes=VMEM_LIMIT),
        name="inproj",
    )(x2, g_attn, w_nat, w_tr, cos_n, sin_n, cos_t, sin_t)


CUM_BLOCK = 256
CUM_UNROLL = 4
N_BIAS = 3


def _bias_lane_base(h):
    return HALF if h % 2 == 0 else 0


def _bias_placement():
    p = np.zeros((LANES, N_HEADS * LANES), np.float32)
    for h in range(N_HEADS):
        base = h * LANES + _bias_lane_base(h)
        for part in range(N_BIAS):
            p[part * N_HEADS + h, base + N_BIAS + part] = -1.0
            p[N_BIAS * N_HEADS, base + part] = 1.0
    return jnp.asarray(p, BF16)


def _cumgate_kernel(fg_ref, bf_ref, kf_ref, place_ref, crow_ref, kaug_ref):
    seq = fg_ref.shape[1]
    r = lax.broadcasted_iota(jnp.int32, (CUM_BLOCK, CUM_BLOCK), 0)
    c = lax.broadcasted_iota(jnp.int32, (CUM_BLOCK, CUM_BLOCK), 1)
    tri = jnp.where(r >= c, 1.0, 0.0).astype(F32)
    er = lax.broadcasted_iota(jnp.int32, (N_HEADS, LANES), 0)
    ec = lax.broadcasted_iota(jnp.int32, (N_HEADS, LANES), 1)
    pick = jnp.where(er == ec, 1.0, 0.0).astype(F32)
    bias = bf_ref[...]
    lane = lax.broadcasted_iota(jnp.int32, (CUM_BLOCK, LANES), 1)

    def body(blk, carry):
        off = pl.multiple_of(blk * CUM_BLOCK, CUM_BLOCK)
        z = fg_ref[0, pl.ds(off, CUM_BLOCK), :] + bias
        logf = jnp.minimum(z, 0.0) - jnp.log(1.0 + jnp.exp(-jnp.abs(z)))
        cs = _dot(tri, logf, precision=lax.Precision.HIGHEST) + carry
        c2 = cs * LOG2E
        crow_ref[0, :, pl.ds(off, CUM_BLOCK)] = _dot_nt(pick, c2, precision=lax.Precision.HIGHEST)
        c1, cb, cc = _split3(jnp.where(lane < N_HEADS, c2, 0.0))
        parts = (c1 + pltpu.roll(cb, N_HEADS, 1) + pltpu.roll(cc, 2 * N_HEADS, 1)
                 + jnp.where(lane == N_BIAS * N_HEADS, 1.0, 0.0))
        feat = _dot(parts.astype(BF16), place_ref[...]).astype(BF16)
        for h in range(N_HEADS):
            mine = (lane < HALF) if h % 2 == 0 else (lane >= HALF)
            kcol = kf_ref[0, pl.ds(off, CUM_BLOCK), (h // 2) * LANES:(h // 2 + 1) * LANES]
            kaug_ref[0, h, pl.ds(off, CUM_BLOCK), :] = jnp.where(mine, kcol, feat[:, h * LANES:(h + 1) * LANES])
        return cs[CUM_BLOCK - 1:CUM_BLOCK, :]

    lax.fori_loop(0, seq // CUM_BLOCK, body, jnp.zeros((1, LANES), F32), unroll=CUM_UNROLL)


def _cumgate(fg3, bf_pad, kf3):
    b, seq, _ = fg3.shape
    place = _bias_placement()
    return pl.pallas_call(
        _cumgate_kernel,
        grid=(b,),
        in_specs=[pl.BlockSpec((1, seq, LANES), lambda i: (i, 0, 0)), pl.BlockSpec((1, LANES), lambda i: (0, 0)),
                  pl.BlockSpec((1, seq, WIDTH), lambda i: (i, 0, 0)), pl.BlockSpec(place.shape, lambda i: (0, 0))],
        out_specs=[pl.BlockSpec((1, N_HEADS, seq), lambda i: (i, 0, 0)),
                   pl.BlockSpec((1, N_HEADS, seq, LANES), lambda i: (i, 0, 0, 0))],
        out_shape=[jax.ShapeDtypeStruct((b, N_HEADS, seq), F32),
                   jax.ShapeDtypeStruct((b, N_HEADS, seq, LANES), BF16)],
        compiler_params=pltpu.CompilerParams(dimension_semantics=("parallel",), vmem_limit_bytes=VMEM_LIMIT),
        name="cumgate",
    )(fg3, bf_pad, kf3, place)


def _compress_one(x_ref, pea_ref, peb_ref, wa_ref, wb_ref, b1_ref, w2_ref, b2_ref):
    xk = x_ref[0]
    n = xk.shape[0]
    wa = wa_ref[...]
    wb = wb_ref[...]
    first = _dot(xk, wa)
    second = _dot(xk, wb)
    pe_term = _dot(pea_ref[...].astype(BF16), wa) + _dot(peb_ref[...].astype(BF16), wb)
    hidden = first + pltpu.roll(second, n - 1, 0) + pe_term[0:1, :] + b1_ref[...]
    act = jax.nn.gelu(hidden)
    return _dot(act.astype(BF16), w2_ref[...]) + b2_ref[...]


def _compress_kernel(kc_ref, vc_ref, cos_ref, sin_ref,
                     kpea, kpeb, kwa, kwb, kb1, kw2, kb2,
                     vpea, vpeb, vwa, vwb, vb1, vw2, vb2,
                     kcmp_ref, vcmpt_ref):
    kcmp = _compress_one(kc_ref, kpea, kpeb, kwa, kwb, kb1, kw2, kb2)
    kcmp_ref[0] = _rope(kcmp, cos_ref[...], sin_ref[...]).astype(BF16)
    vcmpt_ref[0] = _compress_one(vc_ref, vpea, vpeb, vwa, vwb, vb1, vw2, vb2).T.astype(BF16)


def _compress(kc3, vc3, cos_c, sin_c, kparams, vparams):
    b, n, w = kc3.shape
    x_spec = pl.BlockSpec((1, n, w), lambda i: (i, 0, 0))
    full = lambda a: pl.BlockSpec(a.shape, lambda i: (0,) * a.ndim)
    params = list(kparams) + list(vparams)
    return pl.pallas_call(
        _compress_kernel,
        grid=(b,),
        in_specs=[x_spec, x_spec, full(cos_c), full(sin_c)] + [full(p) for p in params],
        out_specs=[pl.BlockSpec((1, n, KV_WIDTH), lambda i: (i, 0, 0)),
                   pl.BlockSpec((1, KV_WIDTH, n), lambda i: (i, 0, 0))],
        out_shape=[jax.ShapeDtypeStruct((b, n, KV_WIDTH), BF16), jax.ShapeDtypeStruct((b, KV_WIDTH, n), BF16)],
        compiler_params=pltpu.CompilerParams(dimension_semantics=("parallel",), vmem_limit_bytes=VMEM_LIMIT),
        name="compress",
    )(kc3, vc3, cos_c, sin_c, *params)


ACC_ROWS = HEAD_DIM + BF16_ROWS


def _flash_step_t(s, vt, m_ref, acc_ref, idx):
    row = slice(idx, idx + 1)
    m_old = m_ref[row, :]
    m_new = jnp.maximum(m_old, jnp.max(s, axis=0, keepdims=True))
    alpha = jnp.exp2(m_old - m_new)
    p = jnp.exp2((s - m_new).astype(BF16))
    m_ref[row, :] = m_new
    vt_ones = jnp.concatenate([vt, jnp.ones((BF16_ROWS, vt.shape[1]), BF16)], axis=0)
    acc_ref[idx] = alpha * acc_ref[idx] + _dot(vt_ones, p)


def _flash_reset(m_ref, acc_ref):
    m_ref[...] = jnp.full(m_ref.shape, NEG, F32)
    acc_ref[...] = jnp.zeros(acc_ref.shape, F32)


def _flash_finish(acc_ref, idx):
    acc = acc_ref[idx]
    return acc[0:HEAD_DIM] * (1.0 / acc[HEAD_DIM:HEAD_DIM + 1])


def _for_tiles(n, unroll, visit):
    groups = n // unroll

    def body(jj, carry):
        visit(jj * unroll, unroll)
        return carry

    lax.fori_loop(0, groups, body, 0)
    done = groups * unroll
    rem = n - done
    p = unroll // 2
    while p >= 1:
        first = done + (rem & ~(2 * p - 1))

        @pl.when((rem & p) != 0)
        def _(first=first, p=p):
            visit(first, p)

        p //= 2


FOX_TILE = 256
FOX_KEYS = 256
FOX_RUN = 1
FOX_AHEAD = 4
FOX_UNROLL = 4


def _fox_kernel(qt_ref, kaug_ref, vt_ref, crow_ref, o_ref, qaug_ref, m_ref, acc_ref):
    t = FOX_TILE
    tk = FOX_KEYS
    assert t == tk
    i = pl.program_id(1)
    kr = lax.broadcasted_iota(jnp.int32, (tk, t), 0)
    qc = lax.broadcasted_iota(jnp.int32, (tk, t), 1)
    frow = lax.broadcasted_iota(jnp.int32, (BF16_ROWS, t), 0)
    pad = jnp.zeros((HALF - BF16_ROWS, t), BF16)

    _flash_reset(m_ref, acc_ref)
    for h in range(N_HEADS):
        c1, c2, c3 = _split3(crow_ref[0, h:h + 1, :])
        feat = jnp.where(frow == 0, c1, jnp.where(frow == 1, c2, jnp.where(frow == 2, c3,
               jnp.where(frow < 2 * N_BIAS, 1.0, 0.0)))).astype(BF16)
        qh = qt_ref[h * HEAD_DIM:(h + 1) * HEAD_DIM, :]
        parts = [qh, feat, pad] if _bias_lane_base(h) == HALF else [feat, pad, qh]
        qaug_ref[h] = jnp.concatenate(parts, axis=0)

    def step(runs, mask):
        chains = [(off, nkeys, h) for off, nkeys in runs for h in range(N_HEADS)]

        def scores(c):
            off, nkeys, h = chains[c]
            return _dot(kaug_ref[0, h, pl.ds(off, nkeys), :], qaug_ref[h])

        pending = {c: scores(c) for c in range(FOX_AHEAD)}
        for c, (off, nkeys, h) in enumerate(chains):
            if c + FOX_AHEAD < len(chains):
                pending[c + FOX_AHEAD] = scores(c + FOX_AHEAD)
            s = pending.pop(c)
            if mask is not None:
                s = jnp.where(mask, s, MASK_FILL)
            _flash_step_t(s, vt_ref[h * HEAD_DIM:(h + 1) * HEAD_DIM, pl.ds(off, nkeys)], m_ref, acc_ref, h)

    def tile_off(j):
        return pl.multiple_of(j * tk, tk)

    def visit(first, count):
        run = min(count, FOX_RUN)
        step([(tile_off(first + u), run * tk) for u in range(0, count, run)], None)

    _for_tiles(i, FOX_UNROLL, visit)
    step([(tile_off(i), tk)], kr <= qc)

    for col in range(WIDTH // LANES):
        pair = [_flash_finish(acc_ref, h) for h in (2 * col, 2 * col + 1)]
        o_ref[0, :, col * LANES:(col + 1) * LANES] = jnp.concatenate(pair, axis=0).T.astype(BF16)


def _fox(qft, kaug, vft, crow):
    b, _, seq, _ = kaug.shape
    t = FOX_TILE
    nq = seq // t
    return pl.pallas_call(
        _fox_kernel,
        grid=(b, nq),
        in_specs=[pl.BlockSpec((WIDTH, t), lambda bi, i: (0, bi * nq + i)),
                  pl.BlockSpec((1, N_HEADS, seq, LANES), lambda bi, i: (bi, 0, 0, 0)),
                  pl.BlockSpec((WIDTH, seq), lambda bi, i: (0, bi)),
                  pl.BlockSpec((1, N_HEADS, t), lambda bi, i: (bi, 0, i))],
        out_specs=pl.BlockSpec((1, t, WIDTH), lambda bi, i: (bi, i, 0)),
        out_shape=jax.ShapeDtypeStruct((b, seq, WIDTH), BF16),
        scratch_shapes=[pltpu.VMEM((N_HEADS, LANES, t), BF16), pltpu.VMEM((N_HEADS, t), F32),
                        pltpu.VMEM((N_HEADS, ACC_ROWS, t), F32)],
        compiler_params=pltpu.CompilerParams(dimension_semantics=("parallel", "arbitrary"),
                                             vmem_limit_bytes=VMEM_LIMIT),
        name="fox",
    )(qft, kaug, vft, crow)


NSA_TILE = 256
NSA_KEYS = 128
STAT_ROWS = 8
NSA_UNROLL = 8
NSA_RUN = 1
NSA_AHEAD = 4


def _nsa_kernel(qt_ref, kcmp_ref, vcmpt_ref, ks_ref, vst_ref, kw_ref, vwt_ref, fgt_ref, bg_ref, ovl_ref, o_ref,
                qaug_ref, m_ref, acc_ref, ocmp_ref, *, topk):
    t = NSA_TILE
    tk = NSA_KEYS
    per_q = t // tk
    wide = GROUP * t
    n_cmp = kcmp_ref.shape[1]
    i = pl.program_id(1)
    start = i * t
    lane = lax.broadcasted_iota(jnp.int32, (1, LANES), 1)
    mine = [lane < HALF, lane >= HALF]
    zero_bf = jnp.zeros((), BF16)

    q4 = jnp.concatenate([qt_ref[n * LANES:(n + 1) * LANES, :] for n in range(GROUP)], axis=1)
    qpos4 = start + (lax.broadcasted_iota(jnp.int32, (1, wide), 1) & (t - 1))
    qpos = start + lax.broadcasted_iota(jnp.int32, (1, t), 1)

    cur = qpos >> SEL_SHIFT
    sub8 = lax.broadcasted_iota(jnp.int32, (STAT_ROWS, t), 0)

    def cmp_rank(n_rows, n_slots):
        cmp_end = lax.broadcasted_iota(jnp.int32, (n_rows, 1), 0) * CMP_STRIDE + (CMP_LEN - 1)
        vis = cmp_end <= qpos4
        slot = lax.broadcasted_iota(jnp.int32, (n_slots, t), 0)
        valid = slot <= cur
        forced = (slot == 0) | (slot == cur) | (slot == cur - 1)
        for g in range(N_KV):
            s = jnp.where(vis, _dot(jnp.where(mine[g], kcmp_ref[0, 0:n_rows, :], zero_bf), q4), NEG)
            e = jnp.where(vis, jnp.exp2(s - jnp.max(s, axis=0, keepdims=True)), 0.0)
            denom = jnp.sum(e, axis=0, keepdims=True)
            p = e * jnp.where(denom > 0.0, 1.0 / denom, 0.0)
            ocmp_ref[g] = _dot(vcmpt_ref[0, g * HEAD_DIM:(g + 1) * HEAD_DIM, 0:n_rows], p.astype(BF16))
            psum = p[:, 0:t]
            for n in range(1, GROUP):
                psum = psum + p[:, n * t:(n + 1) * t]
            imp = _dot(ovl_ref[0:n_slots, 0:n_rows], psum, precision=lax.Precision.HIGHEST)
            score = jnp.where(valid, imp + jnp.where(forced, FORCE_BONUS, 0.0), -1.0)
            tiles = [score[r:r + STAT_ROWS, :] for r in range(0, n_slots, STAT_ROWS)]
            ranks = [jnp.zeros((STAT_ROWS, t), F32) for _ in tiles]
            for jp in range(n_slots):
                other = jnp.broadcast_to(score[jp:jp + 1, :], (STAT_ROWS, t))
                for k, tile in enumerate(tiles):
                    first = k * STAT_ROWS
                    ge = jnp.where(other >= tile, 1.0, 0.0)
                    gt = jnp.where(other > tile, 1.0, 0.0)
                    if first > jp:
                        ahead = ge
                    elif first + STAT_ROWS - 1 <= jp:
                        ahead = gt
                    else:
                        ahead = jnp.where(sub8 + first > jp, ge, gt)
                    ranks[k] = ranks[k] + ahead
            rank = jnp.concatenate(ranks, axis=0)
            selb = jnp.where(rank < topk, 0.0, MASK_FILL).astype(BF16)
            if n_slots < SEL_SLOTS:
                selb = jnp.concatenate([selb, jnp.full((SEL_SLOTS - n_slots, t), MASK_FILL, BF16)], axis=0)
            selb4 = jnp.concatenate([selb] * GROUP, axis=1)
            qaug_ref[g] = jnp.concatenate([q4[0:HALF], selb4] if g == 0 else [selb4, q4[HALF:LANES]], axis=0)

    slots_per_tile = t // SEL_LEN
    lo = 0
    for n_slots in range(SEL_SLOTS // 4, SEL_SLOTS + 1, SEL_SLOTS // 4):
        hi = n_slots // slots_per_tile
        n_rows = min(n_cmp, -(-(hi * t // CMP_STRIDE) // LANES) * LANES)
        last = n_slots == SEL_SLOTS

        @pl.when((i >= lo) if last else ((i >= lo) & (i < hi)))
        def _(n_rows=n_rows, n_slots=n_slots):
            cmp_rank(n_rows, n_slots)

        lo = hi
    o_cmp = [ocmp_ref[g] for g in range(N_KV)]

    kr = lax.broadcasted_iota(jnp.int32, (tk, wide), 0)
    q_local = lax.broadcasted_iota(jnp.int32, (tk, wide), 1) & (t - 1)
    run_rows = NSA_RUN * tk
    key_row = lax.broadcasted_iota(jnp.int32, (run_rows, LANES), 0)
    lane_full = lax.broadcasted_iota(jnp.int32, (run_rows, LANES), 1)
    slot_minus_row = [lane_full - HALF - (key_row >> SEL_SHIFT), lane_full - (key_row >> SEL_SHIFT)]

    def sel_score(off, nkeys, g):
        onehot = jnp.where(slot_minus_row[g][0:nkeys] == (off >> SEL_SHIFT), 1.0, 0.0).astype(BF16)
        return _dot(jnp.where(mine[g], ks_ref[0, pl.ds(off, nkeys), :], onehot), qaug_ref[g])

    def win_score(off, nkeys, g):
        return _dot(jnp.where(mine[g], kw_ref[0, pl.ds(off, nkeys), :], zero_bf), q4)

    def sel_chains(tiles, nkeys=tk):
        return [(sel_score, vst_ref, g, g, off, nkeys, mask) for off, mask in tiles for g in range(N_KV)]

    def win_chains(tiles):
        return [(win_score, vwt_ref, g, N_KV + g, off, tk, mask) for off, mask in tiles for g in range(N_KV)]

    def sweep(chains):
        def issue(c):
            score, _, g, _, off, nkeys, _ = chains[c]
            return score(off, nkeys, g)

        pending = {c: issue(c) for c in range(min(NSA_AHEAD, len(chains)))}
        for c, (_, vt_ref_, g, slot_id, off, nkeys, mask) in enumerate(chains):
            if c + NSA_AHEAD < len(chains):
                pending[c + NSA_AHEAD] = issue(c + NSA_AHEAD)
            s = pending.pop(c)
            if mask is not None:
                s = jnp.where(mask, s, MASK_FILL)
            _flash_step_t(s, vt_ref_[g * HEAD_DIM:(g + 1) * HEAD_DIM, pl.ds(off, nkeys)], m_ref, acc_ref, slot_id)

    def tile_off(j):
        return pl.multiple_of(j * tk, tk)

    def sel_visit(first, count):
        run = min(count, NSA_RUN)
        sweep(sel_chains([(tile_off(first + u), None) for u in range(0, count, run)], run * tk))

    n_before = i * per_q
    _flash_reset(m_ref, acc_ref)
    _for_tiles(n_before, NSA_UNROLL, sel_visit)

    own = [(tile_off(n_before + u), kr + u * tk <= q_local) for u in range(per_q)]
    n_back = WINDOW // tk

    @pl.when(n_before >= n_back)
    def _():
        back = [(tile_off(n_before - n_back + u), (q_local - kr < u * tk) if u < per_q else None)
                for u in range(n_back)]
        sweep(sel_chains(own) + win_chains(back + own))

    @pl.when(n_before < n_back)
    def _():
        sweep(sel_chains(own))

        def win_body(j, carry):
            sweep(win_chains([(tile_off(j), None)]))
            return carry

        lax.fori_loop(0, n_before, win_body, 0)
        sweep(win_chains(own))

    o_slc = [_flash_finish(acc_ref, g) for g in range(N_KV)]
    o_win = [_flash_finish(acc_ref, N_KV + g) for g in range(N_KV)]

    gates = jax.nn.sigmoid(fgt_ref[...] + bg_ref[...])
    for n in range(GROUP):
        cols = slice(n * t, (n + 1) * t)
        mixed = []
        for g in range(N_KV):
            base = N_HEADS + (g * GROUP + n) * 3
            mixed.append(gates[base:base + 1, :] * o_cmp[g][:, cols]
                         + gates[base + 1:base + 2, :] * o_slc[g][:, cols]
                         + gates[base + 2:base + 3, :] * o_win[g][:, cols])
        o_ref[0, :, n * LANES:(n + 1) * LANES] = jnp.concatenate(mixed, axis=0).T.astype(BF16)


def _nsa(qnt, kcmp, vcmpt, ks, vst, kw, vwt, fgt, bg_t, ovl_t, topk):
    b, seq, _ = ks.shape
    t = NSA_TILE
    nq = seq // t
    n_cmp = kcmp.shape[1]
    whole = lambda bi, i: (bi, 0, 0)
    whole_t = lambda bi, i: (0, bi)
    tile_t = lambda bi, i: (0, bi * nq + i)
    fixed = lambda bi, i: (0, 0)
    return pl.pallas_call(
        functools.partial(_nsa_kernel, topk=topk),
        grid=(b, nq),
        in_specs=[pl.BlockSpec((WIDTH, t), tile_t),
                  pl.BlockSpec((1, n_cmp, KV_WIDTH), whole), pl.BlockSpec((1, KV_WIDTH, n_cmp), whole),
                  pl.BlockSpec((1, seq, KV_WIDTH), whole), pl.BlockSpec((KV_WIDTH, seq), whole_t),
                  pl.BlockSpec((1, seq, KV_WIDTH), whole), pl.BlockSpec((KV_WIDTH, seq), whole_t),
                  pl.BlockSpec((LANES, t), tile_t), pl.BlockSpec((LANES, t), fixed),
                  pl.BlockSpec(ovl_t.shape, fixed)],
        out_specs=pl.BlockSpec((1, t, WIDTH), lambda bi, i: (bi, i, 0)),
        out_shape=jax.ShapeDtypeStruct((b, seq, WIDTH), BF16),
        scratch_shapes=[pltpu.VMEM((N_KV, LANES, GROUP * t), BF16), pltpu.VMEM((STAT_ROWS, GROUP * t), F32),
                        pltpu.VMEM((2 * N_KV, ACC_ROWS, GROUP * t), F32),
                        pltpu.VMEM((N_KV, HEAD_DIM, GROUP * t), F32)],
        compiler_params=pltpu.CompilerParams(dimension_semantics=("parallel", "arbitrary"),
                                             vmem_limit_bytes=VMEM_LIMIT),
        name="nsa",
    )(qnt, kcmp, vcmpt, ks, vst, kw, vwt, fgt, bg_t, ovl_t)


def _post_kernel(x_ref, of_ref, on_ref, gf_ref, gn_ref, wo_ref, gm_ref, wu_ref, wd_ref, gl_ref, o_ref):
    yf = _rms(of_ref[...].astype(F32), gf_ref[...]).astype(BF16)
    yn = _rms(on_ref[...].astype(F32), gn_ref[...]).astype(BF16)
    h1 = x_ref[...] + _dot(yf, wo_ref[0:WIDTH, :]) + _dot(yn, wo_ref[WIDTH:2 * WIDTH, :])
    u = _dot(_rms(h1, gm_ref[...]).astype(BF16), wu_ref[...])
    act = jnp.square(jnp.maximum(u, 0.0)).astype(BF16)
    h2 = h1 + _dot(act, wd_ref[...])
    o_ref[...] = _rms(h2, gl_ref[...])


def _post(x2, ofox, onsa, g_fox, g_nsa, w_out, g_mlp, w_up, w_down, g_final, tm):
    m, d = x2.shape
    row = lambda i: (i, 0)
    fixed = lambda i: (0, 0)
    full = lambda a: pl.BlockSpec(a.shape, fixed, pipeline_mode=pl.Buffered(1))
    return pl.pallas_call(
        _post_kernel,
        grid=(m // tm,),
        in_specs=[pl.BlockSpec((tm, d), row), pl.BlockSpec((tm, WIDTH), row), pl.BlockSpec((tm, WIDTH), row),
                  full(g_fox), full(g_nsa), full(w_out), full(g_mlp), full(w_up), full(w_down), full(g_final)],
        out_specs=pl.BlockSpec((tm, d), row),
        out_shape=jax.ShapeDtypeStruct((m, d), F32),
        compiler_params=pltpu.CompilerParams(dimension_semantics=("parallel",), vmem_limit_bytes=VMEM_LIMIT),
        name="post",
    )(x2, ofox, onsa, g_fox, g_nsa, w_out, g_mlp, w_up, w_down, g_final)


_NSA_PERM = np.array([(p % 2) * GROUP + p // 2 for p in range(N_HEADS)])


def _perm_heads(a, axis):
    shape = a.shape
    a = a.reshape(shape[:axis] + (N_HEADS, HEAD_DIM) + shape[axis + 1:])
    a = jnp.take(a, _NSA_PERM, axis=axis)
    return a.reshape(shape)


def _prep_w_in(w):
    qscale = HEAD_DIM ** -0.5 * LOG2E
    sizes = [WIDTH, WIDTH, WIDTH, N_HEADS, WIDTH] + [KV_WIDTH] * 6 + [3 * N_HEADS]
    offs = np.cumsum([0] + sizes)
    qf, kf, vf, fl, qn, kc, vc, ks, vs, kw, vw, gate = [w[:, offs[k]:offs[k + 1]] for k in range(len(sizes))]
    fg = jnp.concatenate([fl, gate, jnp.zeros((w.shape[0], LANES - 4 * N_HEADS), w.dtype)], axis=1)
    w_nat = jnp.concatenate([kf, kc, vc, ks, kw, fg], axis=1).astype(BF16)
    w_tr = jnp.concatenate([qf * qscale, vf, _perm_heads(qn * qscale, 1), vs, vw, fg], axis=1).T.astype(BF16)
    assert w_nat.shape[1] == NAT_COLS and w_tr.shape[0] == TR_ROWS
    return w_nat, w_tr


def _prep_compress(pe, w1, b1, w2, b2):
    half = CMP_LEN // 2
    eye = jnp.eye(N_KV, dtype=F32)

    def expand_w1(wpart):
        w3 = wpart.reshape(half, HEAD_DIM, CMP_HIDDEN)
        return jnp.einsum('ldh,gk->lgdkh', w3, eye).reshape(half * KV_WIDTH, N_KV * CMP_HIDDEN).astype(BF16)

    def expand_pe(ppart):
        flat = jnp.broadcast_to(ppart[:, None, :], (half, N_KV, HEAD_DIM)).reshape(1, half * KV_WIDTH)
        return jnp.broadcast_to(flat, (8, half * KV_WIDTH))

    w2b = jnp.einsum('hd,gk->ghkd', w2, eye).reshape(N_KV * CMP_HIDDEN, KV_WIDTH).astype(BF16)
    return (expand_pe(pe[:half]), expand_pe(pe[half:]),
            expand_w1(w1[:half * HEAD_DIM]), expand_w1(w1[half * HEAD_DIM:]),
            jnp.tile(b1, N_KV)[None, :], w2b, jnp.tile(b2, N_KV)[None, :])


def _rope_tables(pos, reps):
    half = HEAD_DIM // 2
    inv = ROPE_THETA ** (-jnp.arange(half, dtype=F32) / half)
    ang = pos.astype(F32)[:, None] * inv[None, :]
    cos = jnp.cos(ang)
    sin = jnp.sin(ang)
    return (jnp.tile(jnp.concatenate([cos, cos], axis=1), (1, reps)),
            jnp.tile(jnp.concatenate([-sin, sin], axis=1), (1, reps)))


def _overlap_t(n_cmp_slots, n_cmp, n_blocks):
    c = np.arange(n_cmp_slots)[None, :] * CMP_STRIDE
    j = np.arange(SEL_SLOTS)[:, None]
    s = j * SEL_LEN
    ovl = (c < s + SEL_LEN) & (c + CMP_LEN > s) & (np.arange(n_cmp_slots)[None, :] < n_cmp) & (j < n_blocks)
    return jnp.asarray(ovl.astype(np.float32))


def _row_tile(m, want):
    t = want
    while m % t:
        t //= 2
    return t


def _layer(h, seq, g_attn, w_in, b_f, b_gate, cmpk, cmpv, g_fox, g_nsa, w_out, g_mlp, w_up, w_down, g_out):
    m, d = h.shape
    b = m // seq
    n_chunks = seq // CMP_STRIDE
    n_cmp = (seq - CMP_LEN) // CMP_STRIDE + 1
    n_blocks = seq // SEL_LEN
    assert seq % FOX_TILE == 0 and seq % NSA_TILE == 0 and n_chunks % LANES == 0
    assert SEL_TOPK <= n_blocks <= SEL_SLOTS
    assert n_cmp == n_chunks - 1

    cos_n, sin_n = _rope_tables(jnp.arange(seq), N_KV)
    cos_c, sin_c = _rope_tables(jnp.arange(n_chunks) * CMP_STRIDE + CMP_LEN - 1, N_KV)
    cos_t = jnp.tile(cos_n, (1, GROUP)).T
    sin_t = jnp.tile(sin_n, (1, GROUP)).T

    tm = _row_tile(seq, 512)
    w_nat, w_tr = _prep_w_in(w_in)
    kf, kc, vc, ks, kw, fg, qft, vft, qnt, vst, vwt, fgt = _inproj(
        h, g_attn[None, :], w_nat, w_tr, cos_n, sin_n, cos_t, sin_t, seq, tm)
    r3 = lambda a: a.reshape(b, seq, a.shape[-1])

    bf_pad = jnp.zeros((1, LANES), F32).at[0, :N_HEADS].set(b_f)
    bg_col = jnp.zeros((LANES,), F32).at[N_HEADS:4 * N_HEADS].set(b_gate)
    bg_t = jnp.broadcast_to(bg_col[:, None], (LANES, NSA_TILE))
    crow, kaug = _cumgate(r3(fg), bf_pad, r3(kf))

    chunked = lambda a: a.reshape(b, n_chunks, CMP_STRIDE * KV_WIDTH)
    kcmp, vcmpt = _compress(chunked(kc), chunked(vc), cos_c, sin_c, _prep_compress(*cmpk), _prep_compress(*cmpv))

    ofox = _fox(qft, kaug, vft, crow)
    onsa = _nsa(qnt, kcmp, vcmpt, r3(ks), vst, r3(kw), vwt, fgt, bg_t,
                _overlap_t(n_chunks, n_cmp, n_blocks), min(SEL_TOPK, n_blocks))

    w_out_p = jnp.concatenate([w_out[:WIDTH], _perm_heads(w_out[WIDTH:], 0)], axis=0).astype(BF16)
    return _post(h, ofox.reshape(m, WIDTH), onsa.reshape(m, WIDTH), g_fox[None, :], _perm_heads(g_nsa, 0)[None, :],
                 w_out_p, g_mlp[None, :], w_up.astype(BF16), w_down.astype(BF16), g_out[None, :],
                 _row_tile(seq, 256))


def kernel(x, g_attn, w_in, b_f, b_gate, cmpk_pe, cmpk_w1, cmpk_b1, cmpk_w2, cmpk_b2, cmpv_pe, cmpv_w1, cmpv_b1,
           cmpv_w2, cmpv_b2, g_fox, g_nsa, w_out, g_mlp, w_up, w_down, g_final):
    b, seq, d = x.shape
    depth = g_attn.shape[0]
    assert depth == 1, "the final rmsnorm is fused into the (single) layer's last kernel"
    h = x.reshape(b * seq, d)
    out = _layer(h, seq, g_attn[0], w_in[0], b_f[0], b_gate[0],
                 (cmpk_pe[0], cmpk_w1[0], cmpk_b1[0], cmpk_w2[0], cmpk_b2[0]),
                 (cmpv_pe[0], cmpv_w1[0], cmpv_b1[0], cmpv_w2[0], cmpv_b2[0]),
                 g_fox[0], g_nsa[0], w_out[0], g_mlp[0], w_up[0], w_down[0], g_final)
    return out.reshape(b, seq, d)
```

```python
import functools

import numpy as np
import jax
import jax.numpy as jnp
from jax import lax
from jax.experimental import pallas as pl
from jax.experimental.pallas import tpu as pltpu

F32 = jnp.float32
BF16 = jnp.bfloat16

HEAD_DIM = 64
N_HEADS = 8
N_KV = 2
GROUP = N_HEADS // N_KV
WIDTH = N_HEADS * HEAD_DIM
KV_WIDTH = N_KV * HEAD_DIM
CMP_LEN = 32
CMP_STRIDE = 16
CMP_HIDDEN = 256
SEL_LEN = 64
SEL_SHIFT = 6
SEL_TOPK = 16
SEL_SLOTS = 64
WINDOW = 512
ROPE_THETA = 10000.0
EPS = 1e-6
NEG = -1e30
MASK_FILL = -(2.0 ** 100)
FORCE_BONUS = 1e4
LOG2E = 1.4426950408889634
LANES = 128
HALF = LANES // 2
BF16_ROWS = 16

VMEM_LIMIT = 56 * 1024 * 1024


def _dot(a, b, precision=None):
    return jnp.dot(a, b, preferred_element_type=F32, precision=precision)


def _dot_nt(a, b, precision=None):
    return lax.dot_general(a, b, (((1,), (1,)), ((), ())), preferred_element_type=F32, precision=precision)


def _rms(x, g):
    return x * lax.rsqrt(jnp.mean(x * x, axis=-1, keepdims=True) + EPS) * g


def _rope(x, cos, sin_signed):
    w = x.shape[-1]
    lane = lax.broadcasted_iota(jnp.int32, (1, w), 1)
    first = (lane % HEAD_DIM) < (HEAD_DIM // 2)
    partner = jnp.where(first, pltpu.roll(x, w - HEAD_DIM // 2, 1), pltpu.roll(x, HEAD_DIM // 2, 1))
    return x * cos + partner * sin_signed


def _rope_t(x, cos, sin_signed):
    r = x.shape[0]
    row = lax.broadcasted_iota(jnp.int32, (r, 1), 0)
    first = (row % HEAD_DIM) < (HEAD_DIM // 2)
    partner = jnp.where(first, pltpu.roll(x, r - HEAD_DIM // 2, 0), pltpu.roll(x, HEAD_DIM // 2, 0))
    return x * cos + partner * sin_signed


def _split3(c):
    c1 = c.astype(BF16).astype(F32)
    r1 = c - c1
    c2 = r1.astype(BF16).astype(F32)
    return c1, c2, r1 - c2


NAT_KF, NAT_KC, NAT_VC, NAT_KS, NAT_KW, NAT_FG, NAT_COLS = 0, 512, 640, 768, 896, 1024, 1152
TR_QF, TR_VF, TR_QN, TR_VS, TR_VW, TR_FG, TR_ROWS = 0, 512, 1024, 1536, 1664, 1792, 1920


def _inproj_kernel(x_ref, g_ref, wn_ref, wt_ref, cos_ref, sin_ref, cost_ref, sint_ref,
                   kf_ref, kc_ref, vc_ref, ks_ref, kw_ref, fg_ref,
                   qft_ref, vft_ref, qnt_ref, vst_ref, vwt_ref, fgt_ref):
    hb = _rms(x_ref[...], g_ref[...]).astype(BF16)

    def nat(lo, width):
        return _dot(hb, wn_ref[:, lo:lo + width])

    def tr(lo, rows):
        return _dot_nt(wt_ref[lo:lo + rows, :], hb)

    cos = cos_ref[...]
    sin = sin_ref[...]
    kf_ref[...] = nat(NAT_KF, WIDTH).astype(BF16)
    kc_ref[...] = nat(NAT_KC, KV_WIDTH).astype(BF16)
    vc_ref[...] = nat(NAT_VC, KV_WIDTH).astype(BF16)
    ks_ref[...] = _rope(nat(NAT_KS, KV_WIDTH), cos, sin).astype(BF16)
    kw_ref[...] = _rope(nat(NAT_KW, KV_WIDTH), cos, sin).astype(BF16)
    fg_ref[...] = nat(NAT_FG, LANES)
    qft_ref[...] = tr(TR_QF, WIDTH).astype(BF16)
    vft_ref[...] = tr(TR_VF, WIDTH).astype(BF16)
    qnt_ref[...] = _rope_t(tr(TR_QN, WIDTH), cost_ref[...], sint_ref[...]).astype(BF16)
    vst_ref[...] = tr(TR_VS, KV_WIDTH).astype(BF16)
    vwt_ref[...] = tr(TR_VW, KV_WIDTH).astype(BF16)
    fgt_ref[...] = tr(TR_FG, LANES)


def _inproj(x2, g_attn, w_nat, w_tr, cos_n, sin_n, cos_t, sin_t, seq, tm):
    m, d = x2.shape
    sblocks = seq // tm
    row = lambda i: (i, 0)
    col = lambda i: (0, i)
    fixed = lambda i: (0, 0)
    tab = lambda i: (i % sblocks, 0)
    tab_t = lambda i: (0, i % sblocks)
    sds = jax.ShapeDtypeStruct
    out_shape = [sds((m, WIDTH), BF16)] + [sds((m, KV_WIDTH), BF16)] * 4 + [sds((m, LANES), F32),
                 sds((WIDTH, m), BF16), sds((WIDTH, m), BF16), sds((WIDTH, m), BF16),
                 sds((KV_WIDTH, m), BF16), sds((KV_WIDTH, m), BF16), sds((LANES, m), F32)]
    out_specs = ([pl.BlockSpec((tm, WIDTH), row)] + [pl.BlockSpec((tm, KV_WIDTH), row)] * 4
                 + [pl.BlockSpec((tm, LANES), row)]
                 + [pl.BlockSpec((WIDTH, tm), col)] * 3 + [pl.BlockSpec((KV_WIDTH, tm), col)] * 2
                 + [pl.BlockSpec((LANES, tm), col)])
    return pl.pallas_call(
        _inproj_kernel,
        grid=(m // tm,),
        in_specs=[pl.BlockSpec((tm, d), row), pl.BlockSpec((1, d), fixed),
                  pl.BlockSpec(w_nat.shape, fixed), pl.BlockSpec(w_tr.shape, fixed),
                  pl.BlockSpec((tm, KV_WIDTH), tab), pl.BlockSpec((tm, KV_WIDTH), tab),
                  pl.BlockSpec((WIDTH, tm), tab_t), pl.BlockSpec((WIDTH, tm), tab_t)],
        out_specs=out_specs,
        out_shape=out_shape,
        compiler_params=pltpu.CompilerParams(dimension_semantics=("parallel",), vmem_limit_bytes=VMEM_LIMIT),
        name="inproj",
    )(x2, g_attn, w_nat, w_tr, cos_n, sin_n, cos_t, sin_t)


CUM_BLOCK = 256
CUM_UNROLL = 4
N_BIAS = 3


def _bias_lane_base(h):
    return HALF if h % 2 == 0 else 0


def _bias_placement():
    p = np.zeros((LANES, N_HEADS * LANES), np.float32)
    for h in range(N_HEADS):
        base = h * LANES + _bias_lane_base(h)
        for part in range(N_BIAS):
            p[part * N_HEADS + h, base + N_BIAS + part] = -1.0
            p[N_BIAS * N_HEADS, base + part] = 1.0
    return jnp.asarray(p, BF16)


def _cumgate_kernel(fg_ref, bf_ref, kf_ref, place_ref, crow_ref, kaug_ref):
    seq = fg_ref.shape[1]
    r = lax.broadcasted_iota(jnp.int32, (CUM_BLOCK, CUM_BLOCK), 0)
    c = lax.broadcasted_iota(jnp.int32, (CUM_BLOCK, CUM_BLOCK), 1)
    tri = jnp.where(r >= c, 1.0, 0.0).astype(BF16)
    er = lax.broadcasted_iota(jnp.int32, (N_HEADS, LANES), 0)
    ec = lax.broadcasted_iota(jnp.int32, (N_HEADS, LANES), 1)
    pick = jnp.where(er == ec, 1.0, 0.0).astype(BF16)
    bias = bf_ref[...]
    lane = lax.broadcasted_iota(jnp.int32, (CUM_BLOCK, LANES), 1)

    def body(blk, carry):
        off = pl.multiple_of(blk * CUM_BLOCK, CUM_BLOCK)
        z = fg_ref[0, pl.ds(off, CUM_BLOCK), :] + bias
        logf = jnp.minimum(z, 0.0) - jnp.log(1.0 + jnp.exp(-jnp.abs(z)))
        cs = sum(_dot(tri, part.astype(BF16)) for part in _split3(logf)) + carry
        c2 = cs * LOG2E
        c1, cb, cc = _split3(jnp.where(lane < N_HEADS, c2, 0.0))
        crow_ref[0, :, pl.ds(off, CUM_BLOCK)] = sum(_dot_nt(pick, part.astype(BF16)) for part in (c1, cb, cc))
        parts = (c1 + pltpu.roll(cb, N_HEADS, 1) + pltpu.roll(cc, 2 * N_HEADS, 1)
                 + jnp.where(lane == N_BIAS * N_HEADS, 1.0, 0.0))
        feat = _dot(parts.astype(BF16), place_ref[...]).astype(BF16)
        for h in range(N_HEADS):
            mine = (lane < HALF) if h % 2 == 0 else (lane >= HALF)
            kcol = kf_ref[0, pl.ds(off, CUM_BLOCK), (h // 2) * LANES:(h // 2 + 1) * LANES]
            kaug_ref[0, h, pl.ds(off, CUM_BLOCK), :] = jnp.where(mine, kcol, feat[:, h * LANES:(h + 1) * LANES])
        return cs[CUM_BLOCK - 1:CUM_BLOCK, :]

    lax.fori_loop(0, seq // CUM_BLOCK, body, jnp.zeros((1, LANES), F32), unroll=CUM_UNROLL)


def _cumgate(fg3, bf_pad, kf3):
    b, seq, _ = fg3.shape
    place = _bias_placement()
    return pl.pallas_call(
        _cumgate_kernel,
        grid=(b,),
        in_specs=[pl.BlockSpec((1, seq, LANES), lambda i: (i, 0, 0)), pl.BlockSpec((1, LANES), lambda i: (0, 0)),
                  pl.BlockSpec((1, seq, WIDTH), lambda i: (i, 0, 0)), pl.BlockSpec(place.shape, lambda i: (0, 0))],
        out_specs=[pl.BlockSpec((1, N_HEADS, seq), lambda i: (i, 0, 0)),
                   pl.BlockSpec((1, N_HEADS, seq, LANES), lambda i: (i, 0, 0, 0))],
        out_shape=[jax.ShapeDtypeStruct((b, N_HEADS, seq), F32),
                   jax.ShapeDtypeStruct((b, N_HEADS, seq, LANES), BF16)],
        compiler_params=pltpu.CompilerParams(dimension_semantics=("parallel",), vmem_limit_bytes=VMEM_LIMIT),
        name="cumgate",
    )(fg3, bf_pad, kf3, place)


def _compress_one(x_ref, pea_ref, peb_ref, wa_ref, wb_ref, b1_ref, w2_ref, b2_ref):
    xk = x_ref[0]
    n = xk.shape[0]
    wa = wa_ref[...]
    wb = wb_ref[...]
    first = _dot(xk, wa)
    second = _dot(xk, wb)
    pe_term = _dot(pea_ref[...].astype(BF16), wa) + _dot(peb_ref[...].astype(BF16), wb)
    hidden = first + pltpu.roll(second, n - 1, 0) + pe_term[0:1, :] + b1_ref[...]
    act = jax.nn.gelu(hidden)
    return _dot(act.astype(BF16), w2_ref[...]) + b2_ref[...]


def _compress_kernel(kc_ref, vc_ref, cos_ref, sin_ref,
                     kpea, kpeb, kwa, kwb, kb1, kw2, kb2,
                     vpea, vpeb, vwa, vwb, vb1, vw2, vb2,
                     kcmp_ref, vcmpt_ref):
    kcmp = _compress_one(kc_ref, kpea, kpeb, kwa, kwb, kb1, kw2, kb2)
    kcmp_ref[0] = _rope(kcmp, cos_ref[...], sin_ref[...]).astype(BF16)
    vcmpt_ref[0] = _compress_one(vc_ref, vpea, vpeb, vwa, vwb, vb1, vw2, vb2).T.astype(BF16)


def _compress(kc3, vc3, cos_c, sin_c, kparams, vparams):
    b, n, w = kc3.shape
    x_spec = pl.BlockSpec((1, n, w), lambda i: (i, 0, 0))
    full = lambda a: pl.BlockSpec(a.shape, lambda i: (0,) * a.ndim)
    params = list(kparams) + list(vparams)
    return pl.pallas_call(
        _compress_kernel,
        grid=(b,),
        in_specs=[x_spec, x_spec, full(cos_c), full(sin_c)] + [full(p) for p in params],
        out_specs=[pl.BlockSpec((1, n, KV_WIDTH), lambda i: (i, 0, 0)),
                   pl.BlockSpec((1, KV_WIDTH, n), lambda i: (i, 0, 0))],
        out_shape=[jax.ShapeDtypeStruct((b, n, KV_WIDTH), BF16), jax.ShapeDtypeStruct((b, KV_WIDTH, n), BF16)],
        compiler_params=pltpu.CompilerParams(dimension_semantics=("parallel",), vmem_limit_bytes=VMEM_LIMIT),
        name="compress",
    )(kc3, vc3, cos_c, sin_c, *params)


ACC_ROWS = HEAD_DIM + BF16_ROWS


def _flash_step_t(s, vt, m_ref, acc_ref, idx):
    row = slice(idx, idx + 1)
    m_old = m_ref[row, :]
    m_new = jnp.maximum(m_old, jnp.max(s, axis=0, keepdims=True))
    alpha = jnp.exp2(m_old - m_new)
    p = jnp.exp2((s - m_new).astype(BF16))
    m_ref[row, :] = m_new
    vt_ones = jnp.concatenate([vt, jnp.ones((BF16_ROWS, vt.shape[1]), BF16)], axis=0)
    acc_ref[idx] = alpha * acc_ref[idx] + _dot(vt_ones, p)


def _flash_reset(m_ref, acc_ref):
    m_ref[...] = jnp.full(m_ref.shape, NEG, F32)
    acc_ref[...] = jnp.zeros(acc_ref.shape, F32)


def _flash_finish(acc_ref, idx):
    acc = acc_ref[idx]
    return acc[0:HEAD_DIM] * (1.0 / acc[HEAD_DIM:HEAD_DIM + 1])


def _for_tiles(n, unroll, visit):
    groups = n // unroll

    def body(jj, carry):
        visit(jj * unroll, unroll)
        return carry

    lax.fori_loop(0, groups, body, 0)
    done = groups * unroll
    rem = n - done
    p = unroll // 2
    while p >= 1:
        first = done + (rem & ~(2 * p - 1))

        @pl.when((rem & p) != 0)
        def _(first=first, p=p):
            visit(first, p)

        p //= 2


FOX_TILE = 256
FOX_KEYS = 256
FOX_RUN = 1
FOX_AHEAD = 4
FOX_UNROLL = 4


def _fox_kernel(qt_ref, kaug_ref, vt_ref, crow_ref, o_ref, qaug_ref, m_ref, acc_ref):
    t = FOX_TILE
    tk = FOX_KEYS
    assert t == tk
    i = pl.program_id(1)
    kr = lax.broadcasted_iota(jnp.int32, (tk, t), 0)
    qc = lax.broadcasted_iota(jnp.int32, (tk, t), 1)
    frow = lax.broadcasted_iota(jnp.int32, (BF16_ROWS, t), 0)
    pad = jnp.zeros((HALF - BF16_ROWS, t), BF16)

    _flash_reset(m_ref, acc_ref)
    for h in range(N_HEADS):
        c1, c2, c3 = _split3(crow_ref[0, h:h + 1, :])
        feat = jnp.where(frow == 0, c1, jnp.where(frow == 1, c2, jnp.where(frow == 2, c3,
               jnp.where(frow < 2 * N_BIAS, 1.0, 0.0)))).astype(BF16)
        qh = qt_ref[h * HEAD_DIM:(h + 1) * HEAD_DIM, :]
        parts = [qh, feat, pad] if _bias_lane_base(h) == HALF else [feat, pad, qh]
        qaug_ref[h] = jnp.concatenate(parts, axis=0)

    def step(runs, mask):
        chains = [(off, nkeys, h) for off, nkeys in runs for h in range(N_HEADS)]

        def scores(c):
            off, nkeys, h = chains[c]
            return _dot(kaug_ref[0, h, pl.ds(off, nkeys), :], qaug_ref[h])

        pending = {c: scores(c) for c in range(FOX_AHEAD)}
        for c, (off, nkeys, h) in enumerate(chains):
            if c + FOX_AHEAD < len(chains):
                pending[c + FOX_AHEAD] = scores(c + FOX_AHEAD)
            s = pending.pop(c)
            if mask is not None:
                s = jnp.where(mask, s, MASK_FILL)
            _flash_step_t(s, vt_ref[h * HEAD_DIM:(h + 1) * HEAD_DIM, pl.ds(off, nkeys)], m_ref, acc_ref, h)

    def tile_off(j):
        return pl.multiple_of(j * tk, tk)

    def visit(first, count):
        run = min(count, FOX_RUN)
        step([(tile_off(first + u), run * tk) for u in range(0, count, run)], None)

    _for_tiles(i, FOX_UNROLL, visit)
    step([(tile_off(i), tk)], kr <= qc)

    for col in range(WIDTH // LANES):
        pair = [_flash_finish(acc_ref, h) for h in (2 * col, 2 * col + 1)]
        o_ref[0, :, col * LANES:(col + 1) * LANES] = jnp.concatenate(pair, axis=0).T.astype(BF16)


def _fox(qft, kaug, vft, crow):
    b, _, seq, _ = kaug.shape
    t = FOX_TILE
    nq = seq // t
    return pl.pallas_call(
        _fox_kernel,
        grid=(b, nq),
        in_specs=[pl.BlockSpec((WIDTH, t), lambda bi, i: (0, bi * nq + i)),
                  pl.BlockSpec((1, N_HEADS, seq, LANES), lambda bi, i: (bi, 0, 0, 0)),
                  pl.BlockSpec((WIDTH, seq), lambda bi, i: (0, bi)),
                  pl.BlockSpec((1, N_HEADS, t), lambda bi, i: (bi, 0, i))],
        out_specs=pl.BlockSpec((1, t, WIDTH), lambda bi, i: (bi, i, 0)),
        out_shape=jax.ShapeDtypeStruct((b, seq, WIDTH), BF16),
        scratch_shapes=[pltpu.VMEM((N_HEADS, LANES, t), BF16), pltpu.VMEM((N_HEADS, t), F32),
                        pltpu.VMEM((N_HEADS, ACC_ROWS, t), F32)],
        compiler_params=pltpu.CompilerParams(dimension_semantics=("parallel", "arbitrary"),
                                             vmem_limit_bytes=VMEM_LIMIT),
        name="fox",
    )(qft, kaug, vft, crow)


NSA_TILE = 256
NSA_KEYS = 128
STAT_ROWS = 8
NSA_UNROLL = 8
NSA_AHEAD = 4


def _nsa_kernel(qt_ref, kcmp_ref, vcmpt_ref, ks_ref, vst_ref, kw_ref, vwt_ref, fgt_ref, bg_ref, ovl_ref, o_ref,
                qaug_ref, m_ref, acc_ref, ocmp_ref, *, topk):
    t = NSA_TILE
    tk = NSA_KEYS
    per_q = t // tk
    wide = GROUP * t
    n_cmp = kcmp_ref.shape[1]
    i = pl.program_id(1)
    start = i * t
    lane = lax.broadcasted_iota(jnp.int32, (1, LANES), 1)
    mine = [lane < HALF, lane >= HALF]
    zero_bf = jnp.zeros((), BF16)

    q4 = jnp.concatenate([qt_ref[n * LANES:(n + 1) * LANES, :] for n in range(GROUP)], axis=1)
    qpos4 = start + (lax.broadcasted_iota(jnp.int32, (1, wide), 1) & (t - 1))
    cur = (start + lax.broadcasted_iota(jnp.int32, (1, t), 1)) >> SEL_SHIFT
    sub8 = lax.broadcasted_iota(jnp.int32, (STAT_ROWS, t), 0)

    def cmp_rank(n_rows, n_slots):
        cmp_end = lax.broadcasted_iota(jnp.int32, (n_rows, 1), 0) * CMP_STRIDE + (CMP_LEN - 1)
        vis = cmp_end <= qpos4
        slot = lax.broadcasted_iota(jnp.int32, (n_slots, t), 0)
        valid = slot <= cur
        forced = (slot == 0) | (slot == cur) | (slot == cur - 1)
        for g in range(N_KV):
            s = jnp.where(vis, _dot(jnp.where(mine[g], kcmp_ref[0, 0:n_rows, :], zero_bf), q4), NEG)
            e = jnp.where(vis, jnp.exp2(s - jnp.max(s, axis=0, keepdims=True)), 0.0)
            denom = jnp.sum(e, axis=0, keepdims=True)
            p = e * jnp.where(denom > 0.0, 1.0 / denom, 0.0)
            ocmp_ref[g] = _dot(vcmpt_ref[0, g * HEAD_DIM:(g + 1) * HEAD_DIM, 0:n_rows], p.astype(BF16))
            psum = p[:, 0:t]
            for n in range(1, GROUP):
                psum = psum + p[:, n * t:(n + 1) * t]
            ovl = ovl_ref[0:n_slots, 0:n_rows]
            imp = sum(_dot(ovl, part.astype(BF16)) for part in _split3(psum))
            score = jnp.where(valid, imp + jnp.where(forced, FORCE_BONUS, 0.0), -1.0)
            tiles = [score[r:r + STAT_ROWS, :] for r in range(0, n_slots, STAT_ROWS)]
            ranks = [jnp.zeros((STAT_ROWS, t), F32) for _ in tiles]
            for jp in range(n_slots):
                other = jnp.broadcast_to(score[jp:jp + 1, :], (STAT_ROWS, t))
                for k, tile in enumerate(tiles):
                    first = k * STAT_ROWS
                    ge = jnp.where(other >= tile, 1.0, 0.0)
                    gt = jnp.where(other > tile, 1.0, 0.0)
                    if first > jp:
                        ahead = ge
                    elif first + STAT_ROWS - 1 <= jp:
                        ahead = gt
                    else:
                        ahead = jnp.where(sub8 + first > jp, ge, gt)
                    ranks[k] = ranks[k] + ahead
            rank = jnp.concatenate(ranks, axis=0)
            selb = jnp.where(rank < topk, 0.0, MASK_FILL).astype(BF16)
            if n_slots < SEL_SLOTS:
                selb = jnp.concatenate([selb, jnp.full((SEL_SLOTS - n_slots, t), MASK_FILL, BF16)], axis=0)
            selb4 = jnp.concatenate([selb] * GROUP, axis=1)
            qaug_ref[g] = jnp.concatenate([q4[0:HALF], selb4] if g == 0 else [selb4, q4[HALF:LANES]], axis=0)

    slots_per_tile = t // SEL_LEN
    lo = 0
    for n_slots in range(SEL_SLOTS // 4, SEL_SLOTS + 1, SEL_SLOTS // 4):
        hi = n_slots // slots_per_tile
        n_rows = min(n_cmp, -(-(hi * t // CMP_STRIDE) // LANES) * LANES)
        last = n_slots == SEL_SLOTS

        @pl.when((i >= lo) if last else ((i >= lo) & (i < hi)))
        def _(n_rows=n_rows, n_slots=n_slots):
            cmp_rank(n_rows, n_slots)

        lo = hi
    o_cmp = [ocmp_ref[g] for g in range(N_KV)]

    kr = lax.broadcasted_iota(jnp.int32, (tk, wide), 0)
    q_local = lax.broadcasted_iota(jnp.int32, (tk, wide), 1) & (t - 1)
    key_row = lax.broadcasted_iota(jnp.int32, (tk, LANES), 0)
    lane_full = lax.broadcasted_iota(jnp.int32, (tk, LANES), 1)
    slot_minus_row = [lane_full - HALF - (key_row >> SEL_SHIFT), lane_full - (key_row >> SEL_SHIFT)]

    def sel_score(off, g):
        onehot = jnp.where(slot_minus_row[g] == (off >> SEL_SHIFT), 1.0, 0.0).astype(BF16)
        return _dot(jnp.where(mine[g], ks_ref[0, pl.ds(off, tk), :], onehot), qaug_ref[g])

    def win_score(off, g):
        return _dot(jnp.where(mine[g], kw_ref[0, pl.ds(off, tk), :], zero_bf), q4)

    def sel_chains(tiles):
        return [(sel_score, vst_ref, g, g, off, mask) for off, mask in tiles for g in range(N_KV)]

    def win_chains(tiles):
        return [(win_score, vwt_ref, g, N_KV + g, off, mask) for off, mask in tiles for g in range(N_KV)]

    def sweep(chains):
        def issue(c):
            score, _, g, _, off, _ = chains[c]
            return score(off, g)

        pending = {c: issue(c) for c in range(min(NSA_AHEAD, len(chains)))}
        for c, (_, vt_ref_, g, slot_id, off, mask) in enumerate(chains):
            if c + NSA_AHEAD < len(chains):
                pending[c + NSA_AHEAD] = issue(c + NSA_AHEAD)
            s = pending.pop(c)
            if mask is not None:
                s = jnp.where(mask, s, MASK_FILL)
            _flash_step_t(s, vt_ref_[g * HEAD_DIM:(g + 1) * HEAD_DIM, pl.ds(off, tk)], m_ref, acc_ref, slot_id)

    def tile_off(j):
        return pl.multiple_of(j * tk, tk)

    n_before = i * per_q
    _flash_reset(m_ref, acc_ref)
    _for_tiles(n_before, NSA_UNROLL, lambda first, count: sweep(
        sel_chains([(tile_off(first + u), None) for u in range(count)])))

    own = [(tile_off(n_before + u), kr + u * tk <= q_local) for u in range(per_q)]
    n_back = WINDOW // tk

    @pl.when(n_before >= n_back)
    def _():
        back = [(tile_off(n_before - n_back + u), (q_local - kr < u * tk) if u < per_q else None)
                for u in range(n_back)]
        sweep(sel_chains(own) + win_chains(back + own))

    @pl.when(n_before < n_back)
    def _():
        sweep(sel_chains(own))

        def win_body(j, carry):
            sweep(win_chains([(tile_off(j), None)]))
            return carry

        lax.fori_loop(0, n_before, win_body, 0)
        sweep(win_chains(own))

    o_slc = [_flash_finish(acc_ref, g) for g in range(N_KV)]
    o_win = [_flash_finish(acc_ref, N_KV + g) for g in range(N_KV)]

    gates = jax.nn.sigmoid(fgt_ref[...] + bg_ref[...])
    for n in range(GROUP):
        cols = slice(n * t, (n + 1) * t)
        mixed = []
        for g in range(N_KV):
            base = N_HEADS + (g * GROUP + n) * 3
            mixed.append(gates[base:base + 1, :] * o_cmp[g][:, cols]
                         + gates[base + 1:base + 2, :] * o_slc[g][:, cols]
                         + gates[base + 2:base + 3, :] * o_win[g][:, cols])
        o_ref[0, :, n * LANES:(n + 1) * LANES] = jnp.concatenate(mixed, axis=0).T.astype(BF16)


def _nsa(qnt, kcmp, vcmpt, ks, vst, kw, vwt, fgt, bg_t, ovl_t, topk):
    b, seq, _ = ks.shape
    t = NSA_TILE
    nq = seq // t
    n_cmp = kcmp.shape[1]
    whole = lambda bi, i: (bi, 0, 0)
    whole_t = lambda bi, i: (0, bi)
    tile_t = lambda bi, i: (0, bi * nq + i)
    fixed = lambda bi, i: (0, 0)
    return pl.pallas_call(
        functools.partial(_nsa_kernel, topk=topk),
        grid=(b, nq),
        in_specs=[pl.BlockSpec((WIDTH, t), tile_t),
                  pl.BlockSpec((1, n_cmp, KV_WIDTH), whole), pl.BlockSpec((1, KV_WIDTH, n_cmp), whole),
                  pl.BlockSpec((1, seq, KV_WIDTH), whole), pl.BlockSpec((KV_WIDTH, seq), whole_t),
                  pl.BlockSpec((1, seq, KV_WIDTH), whole), pl.BlockSpec((KV_WIDTH, seq), whole_t),
                  pl.BlockSpec((LANES, t), tile_t), pl.BlockSpec((LANES, t), fixed),
                  pl.BlockSpec(ovl_t.shape, fixed)],
        out_specs=pl.BlockSpec((1, t, WIDTH), lambda bi, i: (bi, i, 0)),
        out_shape=jax.ShapeDtypeStruct((b, seq, WIDTH), BF16),
        scratch_shapes=[pltpu.VMEM((N_KV, LANES, GROUP * t), BF16), pltpu.VMEM((STAT_ROWS, GROUP * t), F32),
                        pltpu.VMEM((2 * N_KV, ACC_ROWS, GROUP * t), F32),
                        pltpu.VMEM((N_KV, HEAD_DIM, GROUP * t), F32)],
        compiler_params=pltpu.CompilerParams(dimension_semantics=("parallel", "arbitrary"),
                                             vmem_limit_bytes=VMEM_LIMIT),
        name="nsa",
    )(qnt, kcmp, vcmpt, ks, vst, kw, vwt, fgt, bg_t, ovl_t)


def _post_kernel(x_ref, of_ref, on_ref, gf_ref, gn_ref, wo_ref, gm_ref, wu_ref, wd_ref, gl_ref, o_ref):
    yf = _rms(of_ref[...].astype(F32), gf_ref[...]).astype(BF16)
    yn = _rms(on_ref[...].astype(F32), gn_ref[...]).astype(BF16)
    h1 = x_ref[...] + _dot(yf, wo_ref[0:WIDTH, :]) + _dot(yn, wo_ref[WIDTH:2 * WIDTH, :])
    u = _dot(_rms(h1, gm_ref[...]).astype(BF16), wu_ref[...])
    act = jnp.square(jnp.maximum(u, 0.0)).astype(BF16)
    h2 = h1 + _dot(act, wd_ref[...])
    o_ref[...] = _rms(h2, gl_ref[...])


def _post(x2, ofox, onsa, g_fox, g_nsa, w_out, g_mlp, w_up, w_down, g_final, tm):
    m, d = x2.shape
    row = lambda i: (i, 0)
    fixed = lambda i: (0, 0)
    full = lambda a: pl.BlockSpec(a.shape, fixed, pipeline_mode=pl.Buffered(1))
    return pl.pallas_call(
        _post_kernel,
        grid=(m // tm,),
        in_specs=[pl.BlockSpec((tm, d), row), pl.BlockSpec((tm, WIDTH), row), pl.BlockSpec((tm, WIDTH), row),
                  full(g_fox), full(g_nsa), full(w_out), full(g_mlp), full(w_up), full(w_down), full(g_final)],
        out_specs=pl.BlockSpec((tm, d), row),
        out_shape=jax.ShapeDtypeStruct((m, d), F32),
        compiler_params=pltpu.CompilerParams(dimension_semantics=("parallel",), vmem_limit_bytes=VMEM_LIMIT),
        name="post",
    )(x2, ofox, onsa, g_fox, g_nsa, w_out, g_mlp, w_up, w_down, g_final)


_NSA_PERM = np.array([(p % 2) * GROUP + p // 2 for p in range(N_HEADS)])


def _perm_heads(a, axis):
    shape = a.shape
    a = a.reshape(shape[:axis] + (N_HEADS, HEAD_DIM) + shape[axis + 1:])
    a = jnp.take(a, _NSA_PERM, axis=axis)
    return a.reshape(shape)


def _prep_w_in(w):
    qscale = HEAD_DIM ** -0.5 * LOG2E
    sizes = [WIDTH, WIDTH, WIDTH, N_HEADS, WIDTH] + [KV_WIDTH] * 6 + [3 * N_HEADS]
    offs = np.cumsum([0] + sizes)
    qf, kf, vf, fl, qn, kc, vc, ks, vs, kw, vw, gate = [w[:, offs[k]:offs[k + 1]] for k in range(len(sizes))]
    fg = jnp.concatenate([fl, gate, jnp.zeros((w.shape[0], LANES - 4 * N_HEADS), w.dtype)], axis=1)
    w_nat = jnp.concatenate([kf, kc, vc, ks, kw, fg], axis=1).astype(BF16)
    w_tr = jnp.concatenate([qf * qscale, vf, _perm_heads(qn * qscale, 1), vs, vw, fg], axis=1).T.astype(BF16)
    assert w_nat.shape[1] == NAT_COLS and w_tr.shape[0] == TR_ROWS
    return w_nat, w_tr


def _prep_compress(pe, w1, b1, w2, b2):
    half = CMP_LEN // 2
    eye = jnp.eye(N_KV, dtype=F32)

    def expand_w1(wpart):
        w3 = wpart.reshape(half, HEAD_DIM, CMP_HIDDEN)
        return jnp.einsum('ldh,gk->lgdkh', w3, eye).reshape(half * KV_WIDTH, N_KV * CMP_HIDDEN).astype(BF16)

    def expand_pe(ppart):
        flat = jnp.broadcast_to(ppart[:, None, :], (half, N_KV, HEAD_DIM)).reshape(1, half * KV_WIDTH)
        return jnp.broadcast_to(flat, (8, half * KV_WIDTH))

    w2b = jnp.einsum('hd,gk->ghkd', w2, eye).reshape(N_KV * CMP_HIDDEN, KV_WIDTH).astype(BF16)
    return (expand_pe(pe[:half]), expand_pe(pe[half:]),
            expand_w1(w1[:half * HEAD_DIM]), expand_w1(w1[half * HEAD_DIM:]),
            jnp.tile(b1, N_KV)[None, :], w2b, jnp.tile(b2, N_KV)[None, :])


def _rope_tables(pos, reps):
    half = HEAD_DIM // 2
    inv = ROPE_THETA ** (-jnp.arange(half, dtype=F32) / half)
    ang = pos.astype(F32)[:, None] * inv[None, :]
    cos = jnp.cos(ang)
    sin = jnp.sin(ang)
    return (jnp.tile(jnp.concatenate([cos, cos], axis=1), (1, reps)),
            jnp.tile(jnp.concatenate([-sin, sin], axis=1), (1, reps)))


def _overlap_t(n_cmp_slots, n_cmp, n_blocks):
    c = np.arange(n_cmp_slots)[None, :] * CMP_STRIDE
    j = np.arange(SEL_SLOTS)[:, None]
    s = j * SEL_LEN
    ovl = (c < s + SEL_LEN) & (c + CMP_LEN > s) & (np.arange(n_cmp_slots)[None, :] < n_cmp) & (j < n_blocks)
    return jnp.asarray(ovl.astype(np.float32), BF16)


def _row_tile(m, want):
    t = want
    while m % t:
        t //= 2
    return t


def _layer(h, seq, g_attn, w_in, b_f, b_gate, cmpk, cmpv, g_fox, g_nsa, w_out, g_mlp, w_up, w_down, g_out):
    m, d = h.shape
    b = m // seq
    n_chunks = seq // CMP_STRIDE
    n_cmp = (seq - CMP_LEN) // CMP_STRIDE + 1
    n_blocks = seq // SEL_LEN
    assert seq % FOX_TILE == 0 and seq % NSA_TILE == 0 and n_chunks % LANES == 0
    assert SEL_TOPK <= n_blocks <= SEL_SLOTS
    assert n_cmp == n_chunks - 1

    cos_n, sin_n = _rope_tables(jnp.arange(seq), N_KV)
    cos_c, sin_c = _rope_tables(jnp.arange(n_chunks) * CMP_STRIDE + CMP_LEN - 1, N_KV)
    cos_t = jnp.tile(cos_n, (1, GROUP)).T
    sin_t = jnp.tile(sin_n, (1, GROUP)).T

    tm = _row_tile(seq, 512)
    w_nat, w_tr = _prep_w_in(w_in)
    kf, kc, vc, ks, kw, fg, qft, vft, qnt, vst, vwt, fgt = _inproj(
        h, g_attn[None, :], w_nat, w_tr, cos_n, sin_n, cos_t, sin_t, seq, tm)
    r3 = lambda a: a.reshape(b, seq, a.shape[-1])

    bf_pad = jnp.zeros((1, LANES), F32).at[0, :N_HEADS].set(b_f)
    bg_col = jnp.zeros((LANES,), F32).at[N_HEADS:4 * N_HEADS].set(b_gate)
    bg_t = jnp.broadcast_to(bg_col[:, None], (LANES, NSA_TILE))
    crow, kaug = _cumgate(r3(fg), bf_pad, r3(kf))

    chunked = lambda a: a.reshape(b, n_chunks, CMP_STRIDE * KV_WIDTH)
    kcmp, vcmpt = _compress(chunked(kc), chunked(vc), cos_c, sin_c, _prep_compress(*cmpk), _prep_compress(*cmpv))

    ofox = _fox(qft, kaug, vft, crow)
    onsa = _nsa(qnt, kcmp, vcmpt, r3(ks), vst, r3(kw), vwt, fgt, bg_t,
                _overlap_t(n_chunks, n_cmp, n_blocks), min(SEL_TOPK, n_blocks))

    w_out_p = jnp.concatenate([w_out[:WIDTH], _perm_heads(w_out[WIDTH:], 0)], axis=0).astype(BF16)
    return _post(h, ofox.reshape(m, WIDTH), onsa.reshape(m, WIDTH), g_fox[None, :], _perm_heads(g_nsa, 0)[None, :],
                 w_out_p, g_mlp[None, :], w_up.astype(BF16), w_down.astype(BF16), g_out[None, :],
                 _row_tile(seq, 512))


def kernel(x, g_attn, w_in, b_f, b_gate, cmpk_pe, cmpk_w1, cmpk_b1, cmpk_w2, cmpk_b2, cmpv_pe, cmpv_w1, cmpv_b1,
           cmpv_w2, cmpv_b2, g_fox, g_nsa, w_out, g_mlp, w_up, w_down, g_final):
    b, seq, d = x.shape
    depth = g_attn.shape[0]
    assert depth == 1, "the final rmsnorm is fused into the (single) layer's last kernel"
    h = x.reshape(b * seq, d)
    out = _layer(h, seq, g_attn[0], w_in[0], b_f[0], b_gate[0],
                 (cmpk_pe[0], cmpk_w1[0], cmpk_b1[0], cmpk_w2[0], cmpk_b2[0]),
                 (cmpv_pe[0], cmpv_w1[0], cmpv_b1[0], cmpv_w2[0], cmpv_b2[0]),
                 g_fox[0], g_nsa[0], w_out[0], g_mlp[0], w_up[0], w_down[0], g_final)
    return out.reshape(b, seq, d)
```

```python
import functools

import numpy as np
import jax
import jax.numpy as jnp
from jax import lax
from jax.experimental import pallas as pl
from jax.experimental.pallas import tpu as pltpu

F32 = jnp.float32
BF16 = jnp.bfloat16

HEAD_DIM = 64
N_HEADS = 8
N_KV = 2
GROUP = N_HEADS // N_KV
WIDTH = N_HEADS * HEAD_DIM
KV_WIDTH = N_KV * HEAD_DIM
CMP_LEN = 32
CMP_STRIDE = 16
CMP_HIDDEN = 256
SEL_LEN = 64
SEL_SHIFT = 6
SEL_TOPK = 16
SEL_SLOTS = 64
WINDOW = 512
ROPE_THETA = 10000.0
EPS = 1e-6
NEG = -1e30
MASK_FILL = -(2.0 ** 100)
FORCE_BONUS = 1e4
LOG2E = 1.4426950408889634
LANES = 128
HALF = LANES // 2
BF16_ROWS = 16

VMEM_LIMIT = 56 * 1024 * 1024


def _dot(a, b, precision=None):
    return jnp.dot(a, b, preferred_element_type=F32, precision=precision)


def _dot_nt(a, b, precision=None):
    return lax.dot_general(a, b, (((1,), (1,)), ((), ())), preferred_element_type=F32, precision=precision)


def _rms(x, g):
    return x * lax.rsqrt(jnp.mean(x * x, axis=-1, keepdims=True) + EPS) * g


def _rope(x, cos, sin_signed):
    w = x.shape[-1]
    lane = lax.broadcasted_iota(jnp.int32, (1, w), 1)
    first = (lane % HEAD_DIM) < (HEAD_DIM // 2)
    partner = jnp.where(first, pltpu.roll(x, w - HEAD_DIM // 2, 1), pltpu.roll(x, HEAD_DIM // 2, 1))
    return x * cos + partner * sin_signed


def _rope_t(x, cos, sin_signed):
    r = x.shape[0]
    row = lax.broadcasted_iota(jnp.int32, (r, 1), 0)
    first = (row % HEAD_DIM) < (HEAD_DIM // 2)
    partner = jnp.where(first, pltpu.roll(x, r - HEAD_DIM // 2, 0), pltpu.roll(x, HEAD_DIM // 2, 0))
    return x * cos + partner * sin_signed


def _split3(c):
    c1 = c.astype(BF16).astype(F32)
    r1 = c - c1
    c2 = r1.astype(BF16).astype(F32)
    return c1, c2, r1 - c2


NAT_KF, NAT_KC, NAT_VC, NAT_KS, NAT_KW, NAT_FG, NAT_COLS = 0, 512, 640, 768, 896, 1024, 1152
TR_QF, TR_VF, TR_QN, TR_VS, TR_VW, TR_FG, TR_ROWS = 0, 512, 1024, 1536, 1664, 1792, 1920


def _inproj_kernel(x_ref, g_ref, wn_ref, wt_ref, cos_ref, sin_ref, cost_ref, sint_ref,
                   kf_ref, kc_ref, vc_ref, ks_ref, kw_ref, fg_ref,
                   qft_ref, vft_ref, qnt_ref, vst_ref, vwt_ref, fgt_ref):
    hb = _rms(x_ref[...], g_ref[...]).astype(BF16)

    def nat(lo, width):
        return _dot(hb, wn_ref[:, lo:lo + width])

    def tr(lo, rows):
        return _dot_nt(wt_ref[lo:lo + rows, :], hb)

    cos = cos_ref[...]
    sin = sin_ref[...]
    kf_ref[...] = nat(NAT_KF, WIDTH).astype(BF16)
    kc_ref[...] = nat(NAT_KC, KV_WIDTH)
    vc_ref[...] = nat(NAT_VC, KV_WIDTH)
    ks_ref[...] = _rope(nat(NAT_KS, KV_WIDTH), cos, sin).astype(BF16)
    kw_ref[...] = _rope(nat(NAT_KW, KV_WIDTH), cos, sin).astype(BF16)
    fg_ref[...] = nat(NAT_FG, LANES)
    qft_ref[...] = tr(TR_QF, WIDTH).astype(BF16)
    vft_ref[...] = tr(TR_VF, WIDTH).astype(BF16)
    pairs = WIDTH // KV_WIDTH
    cos_rows = jnp.concatenate([cost_ref[...]] * pairs, axis=0)
    sin_rows = jnp.concatenate([sint_ref[...]] * pairs, axis=0)
    qnt_ref[...] = _rope_t(tr(TR_QN, WIDTH), cos_rows, sin_rows).astype(BF16)
    vst_ref[...] = tr(TR_VS, KV_WIDTH).astype(BF16)
    vwt_ref[...] = tr(TR_VW, KV_WIDTH).astype(BF16)
    fgt_ref[...] = tr(TR_FG, LANES)


def _inproj(x2, g_attn, w_nat, w_tr, cos_n, sin_n, cos_t, sin_t, seq, tm):
    m, d = x2.shape
    sblocks = seq // tm
    row = lambda i: (i, 0)
    col = lambda i: (0, i)
    fixed = lambda i: (0, 0)
    tab = lambda i: (i % sblocks, 0)
    tab_t = lambda i: (0, i % sblocks)
    sds = jax.ShapeDtypeStruct
    out_shape = [sds((m, WIDTH), BF16)] + [sds((m, KV_WIDTH), F32)] * 2 + [sds((m, KV_WIDTH), BF16)] * 2 + [
                 sds((m, LANES), F32),
                 sds((WIDTH, m), BF16), sds((WIDTH, m), BF16), sds((WIDTH, m), BF16),
                 sds((KV_WIDTH, m), BF16), sds((KV_WIDTH, m), BF16), sds((LANES, m), F32)]
    out_specs = ([pl.BlockSpec((tm, WIDTH), row)] + [pl.BlockSpec((tm, KV_WIDTH), row)] * 4
                 + [pl.BlockSpec((tm, LANES), row)]
                 + [pl.BlockSpec((WIDTH, tm), col)] * 3 + [pl.BlockSpec((KV_WIDTH, tm), col)] * 2
                 + [pl.BlockSpec((LANES, tm), col)])
    return pl.pallas_call(
        _inproj_kernel,
        grid=(m // tm,),
        in_specs=[pl.BlockSpec((tm, d), row), pl.BlockSpec((1, d), fixed),
                  pl.BlockSpec(w_nat.shape, fixed), pl.BlockSpec(w_tr.shape, fixed),
                  pl.BlockSpec((tm, KV_WIDTH), tab), pl.BlockSpec((tm, KV_WIDTH), tab),
                  pl.BlockSpec((KV_WIDTH, tm), tab_t), pl.BlockSpec((KV_WIDTH, tm), tab_t)],
        out_specs=out_specs,
        out_shape=out_shape,
        compiler_params=pltpu.CompilerParams(dimension_semantics=("parallel",), vmem_limit_bytes=VMEM_LIMIT),
        name="inproj",
    )(x2, g_attn, w_nat, w_tr, cos_n, sin_n, cos_t, sin_t)


CUM_BLOCK = 256
CUM_UNROLL = 4
N_BIAS = 3


def _bias_lane_base(h):
    return HALF if h % 2 == 0 else 0


def _bias_placement():
    p = np.zeros((LANES, N_HEADS * LANES), np.float32)
    for h in range(N_HEADS):
        base = h * LANES + _bias_lane_base(h)
        for part in range(N_BIAS):
            p[part * N_HEADS + h, base + N_BIAS + part] = -1.0
            p[N_BIAS * N_HEADS, base + part] = 1.0
    return jnp.asarray(p, BF16)


def _cumgate_kernel(fg_ref, bf_ref, kf_ref, place_ref, crow_ref, kaug_ref):
    seq = fg_ref.shape[1]
    r = lax.broadcasted_iota(jnp.int32, (CUM_BLOCK, CUM_BLOCK), 0)
    c = lax.broadcasted_iota(jnp.int32, (CUM_BLOCK, CUM_BLOCK), 1)
    tri = jnp.where(r >= c, 1.0, 0.0).astype(BF16)
    er = lax.broadcasted_iota(jnp.int32, (N_HEADS, LANES), 0)
    ec = lax.broadcasted_iota(jnp.int32, (N_HEADS, LANES), 1)
    pick = jnp.where(er == ec, 1.0, 0.0).astype(BF16)
    bias = bf_ref[...]
    lane = lax.broadcasted_iota(jnp.int32, (CUM_BLOCK, LANES), 1)

    def body(blk, carry):
        off = pl.multiple_of(blk * CUM_BLOCK, CUM_BLOCK)
        z = fg_ref[0, pl.ds(off, CUM_BLOCK), :] + bias
        logf = jnp.minimum(z, 0.0) - jnp.log(1.0 + jnp.exp(-jnp.abs(z)))
        cs = sum(_dot(tri, part.astype(BF16)) for part in _split3(logf)) + carry
        c2 = cs * LOG2E
        c1, cb, cc = _split3(jnp.where(lane < N_HEADS, c2, 0.0))
        crow_ref[0, :, pl.ds(off, CUM_BLOCK)] = sum(_dot_nt(pick, part.astype(BF16)) for part in (c1, cb, cc))
        parts = (c1 + pltpu.roll(cb, N_HEADS, 1) + pltpu.roll(cc, 2 * N_HEADS, 1)
                 + jnp.where(lane == N_BIAS * N_HEADS, 1.0, 0.0))
        feat = _dot(parts.astype(BF16), place_ref[...]).astype(BF16)
        for h in range(N_HEADS):
            mine = (lane < HALF) if h % 2 == 0 else (lane >= HALF)
            kcol = kf_ref[0, pl.ds(off, CUM_BLOCK), (h // 2) * LANES:(h // 2 + 1) * LANES]
            kaug_ref[0, h, pl.ds(off, CUM_BLOCK), :] = jnp.where(mine, kcol, feat[:, h * LANES:(h + 1) * LANES])
        return cs[CUM_BLOCK - 1:CUM_BLOCK, :]

    lax.fori_loop(0, seq // CUM_BLOCK, body, jnp.zeros((1, LANES), F32), unroll=CUM_UNROLL)


def _cumgate(fg3, bf_pad, kf3):
    b, seq, _ = fg3.shape
    place = _bias_placement()
    return pl.pallas_call(
        _cumgate_kernel,
        grid=(b,),
        in_specs=[pl.BlockSpec((1, seq, LANES), lambda i: (i, 0, 0)), pl.BlockSpec((1, LANES), lambda i: (0, 0)),
                  pl.BlockSpec((1, seq, WIDTH), lambda i: (i, 0, 0)), pl.BlockSpec(place.shape, lambda i: (0, 0))],
        out_specs=[pl.BlockSpec((1, N_HEADS, seq), lambda i: (i, 0, 0)),
                   pl.BlockSpec((1, N_HEADS, seq, LANES), lambda i: (i, 0, 0, 0))],
        out_shape=[jax.ShapeDtypeStruct((b, N_HEADS, seq), F32),
                   jax.ShapeDtypeStruct((b, N_HEADS, seq, LANES), BF16)],
        compiler_params=pltpu.CompilerParams(dimension_semantics=("parallel",), vmem_limit_bytes=VMEM_LIMIT),
        name="cumgate",
    )(fg3, bf_pad, kf3, place)


def _compress_one(x_ref, pea_ref, peb_ref, wa_ref, wb_ref, b1_ref, w2_ref, b2_ref):
    n = x_ref.shape[1] // CMP_STRIDE
    xk = jnp.concatenate([x_ref[0, pl.ds(l, n, stride=CMP_STRIDE), :].astype(BF16) for l in range(CMP_STRIDE)],
                         axis=1)
    wa = wa_ref[...]
    wb = wb_ref[...]
    first = _dot(xk, wa)
    second = _dot(xk, wb)
    pe_term = _dot(pea_ref[...].astype(BF16), wa) + _dot(peb_ref[...].astype(BF16), wb)
    hidden = first + pltpu.roll(second, n - 1, 0) + pe_term[0:1, :] + b1_ref[...]
    act = jax.nn.gelu(hidden)
    return _dot(act.astype(BF16), w2_ref[...]) + b2_ref[...]


def _compress_kernel(kc_ref, vc_ref, cos_ref, sin_ref,
                     kpea, kpeb, kwa, kwb, kb1, kw2, kb2,
                     vpea, vpeb, vwa, vwb, vb1, vw2, vb2,
                     kcmp_ref, vcmpt_ref):
    kcmp = _compress_one(kc_ref, kpea, kpeb, kwa, kwb, kb1, kw2, kb2)
    kcmp_ref[0] = _rope(kcmp, cos_ref[...], sin_ref[...]).astype(BF16)
    vcmpt_ref[0] = _compress_one(vc_ref, vpea, vpeb, vwa, vwb, vb1, vw2, vb2).T.astype(BF16)


def _compress(kc3, vc3, cos_c, sin_c, kparams, vparams):
    b, seq, w = kc3.shape
    n = seq // CMP_STRIDE
    x_spec = pl.BlockSpec((1, seq, w), lambda i: (i, 0, 0))
    full = lambda a: pl.BlockSpec(a.shape, lambda i: (0,) * a.ndim)
    params = list(kparams) + list(vparams)
    return pl.pallas_call(
        _compress_kernel,
        grid=(b,),
        in_specs=[x_spec, x_spec, full(cos_c), full(sin_c)] + [full(p) for p in params],
        out_specs=[pl.BlockSpec((1, n, KV_WIDTH), lambda i: (i, 0, 0)),
                   pl.BlockSpec((1, KV_WIDTH, n), lambda i: (i, 0, 0))],
        out_shape=[jax.ShapeDtypeStruct((b, n, KV_WIDTH), BF16), jax.ShapeDtypeStruct((b, KV_WIDTH, n), BF16)],
        compiler_params=pltpu.CompilerParams(dimension_semantics=("parallel",), vmem_limit_bytes=VMEM_LIMIT),
        name="compress",
    )(kc3, vc3, cos_c, sin_c, *params)


ACC_ROWS = HEAD_DIM + BF16_ROWS


def _flash_step_t(s, vt, m_ref, acc_ref, idx):
    row = slice(idx, idx + 1)
    m_old = m_ref[row, :]
    m_new = jnp.maximum(m_old, jnp.max(s, axis=0, keepdims=True))
    alpha = jnp.exp2(m_old - m_new)
    p = jnp.exp2((s - m_new).astype(BF16))
    m_ref[row, :] = m_new
    vt_ones = jnp.concatenate([vt, jnp.ones((BF16_ROWS, vt.shape[1]), BF16)], axis=0)
    acc_ref[idx] = alpha * acc_ref[idx] + _dot(vt_ones, p)


def _flash_reset(m_ref, acc_ref):
    m_ref[...] = jnp.full(m_ref.shape, NEG, F32)
    acc_ref[...] = jnp.zeros(acc_ref.shape, F32)


def _flash_finish(acc_ref, idx):
    acc = acc_ref[idx]
    return acc[0:HEAD_DIM] * (1.0 / acc[HEAD_DIM:HEAD_DIM + 1])


def _for_tiles(n, unroll, visit):
    groups = n // unroll

    def body(jj, carry):
        visit(jj * unroll, unroll)
        return carry

    lax.fori_loop(0, groups, body, 0)
    done = groups * unroll
    rem = n - done
    p = unroll // 2
    while p >= 1:
        first = done + (rem & ~(2 * p - 1))

        @pl.when((rem & p) != 0)
        def _(first=first, p=p):
            visit(first, p)

        p //= 2


FOX_TILE = 256
FOX_KEYS = 256
FOX_RUN = 1
FOX_AHEAD = 5
FOX_UNROLL = 4


def _fox_kernel(qt_ref, kaug_ref, vt_ref, crow_ref, o_ref, qaug_ref, m_ref, acc_ref):
    t = FOX_TILE
    tk = FOX_KEYS
    assert t == tk
    i = pl.program_id(1)
    kr = lax.broadcasted_iota(jnp.int32, (tk, t), 0)
    qc = lax.broadcasted_iota(jnp.int32, (tk, t), 1)
    frow = lax.broadcasted_iota(jnp.int32, (BF16_ROWS, t), 0)
    pad = jnp.zeros((HALF - BF16_ROWS, t), BF16)

    _flash_reset(m_ref, acc_ref)
    for h in range(N_HEADS):
        c1, c2, c3 = _split3(crow_ref[0, h:h + 1, :])
        feat = jnp.where(frow == 0, c1, jnp.where(frow == 1, c2, jnp.where(frow == 2, c3,
               jnp.where(frow < 2 * N_BIAS, 1.0, 0.0)))).astype(BF16)
        qh = qt_ref[h * HEAD_DIM:(h + 1) * HEAD_DIM, :]
        parts = [qh, feat, pad] if _bias_lane_base(h) == HALF else [feat, pad, qh]
        qaug_ref[h] = jnp.concatenate(parts, axis=0)

    def step(runs, mask):
        chains = [(off, nkeys, h) for off, nkeys in runs for h in range(N_HEADS)]

        def scores(c):
            off, nkeys, h = chains[c]
            return _dot(kaug_ref[0, h, pl.ds(off, nkeys), :], qaug_ref[h])

        pending = {c: scores(c) for c in range(FOX_AHEAD)}
        for c, (off, nkeys, h) in enumerate(chains):
            if c + FOX_AHEAD < len(chains):
                pending[c + FOX_AHEAD] = scores(c + FOX_AHEAD)
            s = pending.pop(c)
            if mask is not None:
                s = jnp.where(mask, s, MASK_FILL)
            _flash_step_t(s, vt_ref[h * HEAD_DIM:(h + 1) * HEAD_DIM, pl.ds(off, nkeys)], m_ref, acc_ref, h)

    def tile_off(j):
        return pl.multiple_of(j * tk, tk)

    def visit(first, count):
        run = min(count, FOX_RUN)
        step([(tile_off(first + u), run * tk) for u in range(0, count, run)], None)

    _for_tiles(i, FOX_UNROLL, visit)
    step([(tile_off(i), tk)], kr <= qc)

    for col in range(WIDTH // LANES):
        pair = [_flash_finish(acc_ref, h) for h in (2 * col, 2 * col + 1)]
        o_ref[0, :, col * LANES:(col + 1) * LANES] = jnp.concatenate(pair, axis=0).T.astype(BF16)


def _fox(qft, kaug, vft, crow):
    b, _, seq, _ = kaug.shape
    t = FOX_TILE
    nq = seq // t
    return pl.pallas_call(
        _fox_kernel,
        grid=(b, nq),
        in_specs=[pl.BlockSpec((WIDTH, t), lambda bi, i: (0, bi * nq + i)),
                  pl.BlockSpec((1, N_HEADS, seq, LANES), lambda bi, i: (bi, 0, 0, 0)),
                  pl.BlockSpec((WIDTH, seq), lambda bi, i: (0, bi)),
                  pl.BlockSpec((1, N_HEADS, t), lambda bi, i: (bi, 0, i))],
        out_specs=pl.BlockSpec((1, t, WIDTH), lambda bi, i: (bi, i, 0)),
        out_shape=jax.ShapeDtypeStruct((b, seq, WIDTH), BF16),
        scratch_shapes=[pltpu.VMEM((N_HEADS, LANES, t), BF16), pltpu.VMEM((N_HEADS, t), F32),
                        pltpu.VMEM((N_HEADS, ACC_ROWS, t), F32)],
        compiler_params=pltpu.CompilerParams(dimension_semantics=("parallel", "arbitrary"),
                                             vmem_limit_bytes=VMEM_LIMIT),
        name="fox",
    )(qft, kaug, vft, crow)


NSA_TILE = 256
NSA_KEYS = 128
STAT_ROWS = 8
NSA_UNROLL = 8
NSA_AHEAD = 5


def _nsa_kernel(qt_ref, kcmp_ref, vcmpt_ref, ks_ref, vst_ref, kw_ref, vwt_ref, fgt_ref, bg_ref, ovl_ref, o_ref,
                qaug_ref, m_ref, acc_ref, ocmp_ref, *, topk):
    t = NSA_TILE
    tk = NSA_KEYS
    per_q = t // tk
    wide = GROUP * t
    n_cmp = kcmp_ref.shape[1]
    i = pl.program_id(1)
    start = i * t
    lane = lax.broadcasted_iota(jnp.int32, (1, LANES), 1)
    mine = [lane < HALF, lane >= HALF]
    zero_bf = jnp.zeros((), BF16)

    q4 = jnp.concatenate([qt_ref[n * LANES:(n + 1) * LANES, :] for n in range(GROUP)], axis=1)
    qpos4 = start + (lax.broadcasted_iota(jnp.int32, (1, wide), 1) & (t - 1))
    cur = (start + lax.broadcasted_iota(jnp.int32, (1, t), 1)) >> SEL_SHIFT
    sub8 = lax.broadcasted_iota(jnp.int32, (STAT_ROWS, t), 0)

    def cmp_rank(n_rows, n_slots):
        cmp_end = lax.broadcasted_iota(jnp.int32, (n_rows, 1), 0) * CMP_STRIDE + (CMP_LEN - 1)
        vis = cmp_end <= qpos4
        slot = lax.broadcasted_iota(jnp.int32, (n_slots, t), 0)
        valid = slot <= cur
        forced = (slot == 0) | (slot == cur) | (slot == cur - 1)
        for g in range(N_KV):
            s = jnp.where(vis, _dot(jnp.where(mine[g], kcmp_ref[0, 0:n_rows, :], zero_bf), q4), NEG)
            e = jnp.where(vis, jnp.exp2(s - jnp.max(s, axis=0, keepdims=True)), 0.0)
            denom = jnp.sum(e, axis=0, keepdims=True)
            p = e * jnp.where(denom > 0.0, 1.0 / denom, 0.0)
            ocmp_ref[g] = _dot(vcmpt_ref[0, g * HEAD_DIM:(g + 1) * HEAD_DIM, 0:n_rows], p.astype(BF16))
            psum = p[:, 0:t]
            for n in range(1, GROUP):
                psum = psum + p[:, n * t:(n + 1) * t]
            ovl = ovl_ref[0:n_slots, 0:n_rows]
            imp = sum(_dot(ovl, part.astype(BF16)) for part in _split3(psum))
            score = jnp.where(valid, imp + jnp.where(forced, FORCE_BONUS, 0.0), -1.0)
            tiles = [score[r:r + STAT_ROWS, :] for r in range(0, n_slots, STAT_ROWS)]
            ranks = [jnp.zeros((STAT_ROWS, t), F32) for _ in tiles]
            for jp in range(n_slots):
                other = jnp.broadcast_to(score[jp:jp + 1, :], (STAT_ROWS, t))
                for k, tile in enumerate(tiles):
                    first = k * STAT_ROWS
                    ge = jnp.where(other >= tile, 1.0, 0.0)
                    gt = jnp.where(other > tile, 1.0, 0.0)
                    if first > jp:
                        ahead = ge
                    elif first + STAT_ROWS - 1 <= jp:
                        ahead = gt
                    else:
                        ahead = jnp.where(sub8 + first > jp, ge, gt)
                    ranks[k] = ranks[k] + ahead
            rank = jnp.concatenate(ranks, axis=0)
            selb = jnp.where(rank < topk, 0.0, MASK_FILL).astype(BF16)
            if n_slots < SEL_SLOTS:
                selb = jnp.concatenate([selb, jnp.full((SEL_SLOTS - n_slots, t), MASK_FILL, BF16)], axis=0)
            selb4 = jnp.concatenate([selb] * GROUP, axis=1)
            qaug_ref[g] = jnp.concatenate([q4[0:HALF], selb4] if g == 0 else [selb4, q4[HALF:LANES]], axis=0)

    slots_per_tile = t // SEL_LEN
    lo = 0
    for n_slots in range(SEL_SLOTS // 4, SEL_SLOTS + 1, SEL_SLOTS // 4):
        hi = n_slots // slots_per_tile
        n_rows = min(n_cmp, -(-(hi * t // CMP_STRIDE) // LANES) * LANES)
        last = n_slots == SEL_SLOTS

        @pl.when((i >= lo) if last else ((i >= lo) & (i < hi)))
        def _(n_rows=n_rows, n_slots=n_slots):
            cmp_rank(n_rows, n_slots)

        lo = hi
    o_cmp = [ocmp_ref[g] for g in range(N_KV)]

    kr = lax.broadcasted_iota(jnp.int32, (tk, wide), 0)
    q_local = lax.broadcasted_iota(jnp.int32, (tk, wide), 1) & (t - 1)
    key_row = lax.broadcasted_iota(jnp.int32, (tk, LANES), 0)
    lane_full = lax.broadcasted_iota(jnp.int32, (tk, LANES), 1)
    slot_minus_row = [lane_full - HALF - (key_row >> SEL_SHIFT), lane_full - (key_row >> SEL_SHIFT)]

    def sel_score(off, g):
        onehot = jnp.where(slot_minus_row[g] == (off >> SEL_SHIFT), 1.0, 0.0).astype(BF16)
        return _dot(jnp.where(mine[g], ks_ref[0, pl.ds(off, tk), :], onehot), qaug_ref[g])

    def win_score(off, g):
        return _dot(jnp.where(mine[g], kw_ref[0, pl.ds(off, tk), :], zero_bf), q4)

    def sel_chains(tiles):
        return [(sel_score, vst_ref, g, g, off, mask) for off, mask in tiles for g in range(N_KV)]

    def win_chains(tiles):
        return [(win_score, vwt_ref, g, N_KV + g, off, mask) for off, mask in tiles for g in range(N_KV)]

    def sweep(chains):
        def issue(c):
            score, _, g, _, off, _ = chains[c]
            return score(off, g)

        pending = {c: issue(c) for c in range(min(NSA_AHEAD, len(chains)))}
        for c, (_, vt_ref_, g, slot_id, off, mask) in enumerate(chains):
            if c + NSA_AHEAD < len(chains):
                pending[c + NSA_AHEAD] = issue(c + NSA_AHEAD)
            s = pending.pop(c)
            if mask is not None:
                s = jnp.where(mask, s, MASK_FILL)
            _flash_step_t(s, vt_ref_[g * HEAD_DIM:(g + 1) * HEAD_DIM, pl.ds(off, tk)], m_ref, acc_ref, slot_id)

    def tile_off(j):
        return pl.multiple_of(j * tk, tk)

    n_before = i * per_q
    _flash_reset(m_ref, acc_ref)
    _for_tiles(n_before, NSA_UNROLL, lambda first, count: sweep(
        sel_chains([(tile_off(first + u), None) for u in range(count)])))

    own = [(tile_off(n_before + u), kr + u * tk <= q_local) for u in range(per_q)]
    n_back = WINDOW // tk

    @pl.when(n_before >= n_back)
    def _():
        back = [(tile_off(n_before - n_back + u), (q_local - kr < u * tk) if u < per_q else None)
                for u in range(n_back)]
        sweep(sel_chains(own) + win_chains(back + own))

    @pl.when(n_before < n_back)
    def _():
        sweep(sel_chains(own))

        def win_body(j, carry):
            sweep(win_chains([(tile_off(j), None)]))
            return carry

        lax.fori_loop(0, n_before, win_body, 0)
        sweep(win_chains(own))

    o_slc = [_flash_finish(acc_ref, g) for g in range(N_KV)]
    o_win = [_flash_finish(acc_ref, N_KV + g) for g in range(N_KV)]

    gates = jax.nn.sigmoid(fgt_ref[...] + bg_ref[...])
    for n in range(GROUP):
        cols = slice(n * t, (n + 1) * t)
        mixed = []
        for g in range(N_KV):
            base = N_HEADS + (g * GROUP + n) * 3
            mixed.append(gates[base:base + 1, :] * o_cmp[g][:, cols]
                         + gates[base + 1:base + 2, :] * o_slc[g][:, cols]
                         + gates[base + 2:base + 3, :] * o_win[g][:, cols])
        o_ref[0, :, n * LANES:(n + 1) * LANES] = jnp.concatenate(mixed, axis=0).T.astype(BF16)


def _nsa(qnt, kcmp, vcmpt, ks, vst, kw, vwt, fgt, bg_t, ovl_t, topk):
    b, seq, _ = ks.shape
    t = NSA_TILE
    nq = seq // t
    n_cmp = kcmp.shape[1]
    whole = lambda bi, i: (bi, 0, 0)
    whole_t = lambda bi, i: (0, bi)
    tile_t = lambda bi, i: (0, bi * nq + i)
    fixed = lambda bi, i: (0, 0)
    return pl.pallas_call(
        functools.partial(_nsa_kernel, topk=topk),
        grid=(b, nq),
        in_specs=[pl.BlockSpec((WIDTH, t), tile_t),
                  pl.BlockSpec((1, n_cmp, KV_WIDTH), whole), pl.BlockSpec((1, KV_WIDTH, n_cmp), whole),
                  pl.BlockSpec((1, seq, KV_WIDTH), whole), pl.BlockSpec((KV_WIDTH, seq), whole_t),
                  pl.BlockSpec((1, seq, KV_WIDTH), whole), pl.BlockSpec((KV_WIDTH, seq), whole_t),
                  pl.BlockSpec((LANES, t), tile_t), pl.BlockSpec((LANES, t), fixed),
                  pl.BlockSpec(ovl_t.shape, fixed)],
        out_specs=pl.BlockSpec((1, t, WIDTH), lambda bi, i: (bi, i, 0)),
        out_shape=jax.ShapeDtypeStruct((b, seq, WIDTH), BF16),
        scratch_shapes=[pltpu.VMEM((N_KV, LANES, GROUP * t), BF16), pltpu.VMEM((STAT_ROWS, GROUP * t), F32),
                        pltpu.VMEM((2 * N_KV, ACC_ROWS, GROUP * t), F32),
                        pltpu.VMEM((N_KV, HEAD_DIM, GROUP * t), F32)],
        compiler_params=pltpu.CompilerParams(dimension_semantics=("parallel", "arbitrary"),
                                             vmem_limit_bytes=VMEM_LIMIT),
        name="nsa",
    )(qnt, kcmp, vcmpt, ks, vst, kw, vwt, fgt, bg_t, ovl_t)


def _post_kernel(x_ref, of_ref, on_ref, gf_ref, gn_ref, wo_ref, gm_ref, wu_ref, wd_ref, gl_ref, o_ref):
    yf = _rms(of_ref[...].astype(F32), gf_ref[...]).astype(BF16)
    yn = _rms(on_ref[...].astype(F32), gn_ref[...]).astype(BF16)
    h1 = x_ref[...] + _dot(yf, wo_ref[0:WIDTH, :]) + _dot(yn, wo_ref[WIDTH:2 * WIDTH, :])
    u = _dot(_rms(h1, gm_ref[...]).astype(BF16), wu_ref[...])
    act = jnp.square(jnp.maximum(u, 0.0)).astype(BF16)
    h2 = h1 + _dot(act, wd_ref[...])
    o_ref[...] = _rms(h2, gl_ref[...])


def _post(x2, ofox, onsa, g_fox, g_nsa, w_out, g_mlp, w_up, w_down, g_final, tm):
    m, d = x2.shape
    row = lambda i: (i, 0)
    fixed = lambda i: (0, 0)
    full = lambda a: pl.BlockSpec(a.shape, fixed, pipeline_mode=pl.Buffered(1))
    return pl.pallas_call(
        _post_kernel,
        grid=(m // tm,),
        in_specs=[pl.BlockSpec((tm, d), row), pl.BlockSpec((tm, WIDTH), row), pl.BlockSpec((tm, WIDTH), row),
                  full(g_fox), full(g_nsa), full(w_out), full(g_mlp), full(w_up), full(w_down), full(g_final)],
        out_specs=pl.BlockSpec((tm, d), row),
        out_shape=jax.ShapeDtypeStruct((m, d), F32),
        compiler_params=pltpu.CompilerParams(dimension_semantics=("parallel",), vmem_limit_bytes=VMEM_LIMIT),
        name="post",
    )(x2, ofox, onsa, g_fox, g_nsa, w_out, g_mlp, w_up, w_down, g_final)


_NSA_PERM = np.array([(p % 2) * GROUP + p // 2 for p in range(N_HEADS)])


def _perm_heads(a, axis):
    shape = a.shape
    a = a.reshape(shape[:axis] + (N_HEADS, HEAD_DIM) + shape[axis + 1:])
    a = jnp.take(a, _NSA_PERM, axis=axis)
    return a.reshape(shape)


def _prep_w_in(w):
    qscale = HEAD_DIM ** -0.5 * LOG2E
    sizes = [WIDTH, WIDTH, WIDTH, N_HEADS, WIDTH] + [KV_WIDTH] * 6 + [3 * N_HEADS]
    offs = np.cumsum([0] + sizes)
    qf, kf, vf, fl, qn, kc, vc, ks, vs, kw, vw, gate = [w[:, offs[k]:offs[k + 1]] for k in range(len(sizes))]
    fg = jnp.concatenate([fl, gate, jnp.zeros((w.shape[0], LANES - 4 * N_HEADS), w.dtype)], axis=1)
    w_nat = jnp.concatenate([kf, kc, vc, ks, kw, fg], axis=1).astype(BF16)
    w_tr = jnp.concatenate([qf * qscale, vf, _perm_heads(qn * qscale, 1), vs, vw, fg], axis=1).T.astype(BF16)
    assert w_nat.shape[1] == NAT_COLS and w_tr.shape[0] == TR_ROWS
    return w_nat, w_tr


def _prep_compress(pe, w1, b1, w2, b2):
    half = CMP_LEN // 2
    eye = jnp.eye(N_KV, dtype=F32)

    def expand_w1(wpart):
        w3 = wpart.reshape(half, HEAD_DIM, CMP_HIDDEN)
        return jnp.einsum('ldh,gk->lgdkh', w3, eye).reshape(half * KV_WIDTH, N_KV * CMP_HIDDEN).astype(BF16)

    def expand_pe(ppart):
        flat = jnp.broadcast_to(ppart[:, None, :], (half, N_KV, HEAD_DIM)).reshape(1, half * KV_WIDTH)
        return jnp.broadcast_to(flat, (8, half * KV_WIDTH))

    w2b = jnp.einsum('hd,gk->ghkd', w2, eye).reshape(N_KV * CMP_HIDDEN, KV_WIDTH).astype(BF16)
    return (expand_pe(pe[:half]), expand_pe(pe[half:]),
            expand_w1(w1[:half * HEAD_DIM]), expand_w1(w1[half * HEAD_DIM:]),
            jnp.tile(b1, N_KV)[None, :], w2b, jnp.tile(b2, N_KV)[None, :])


def _rope_tables(pos, reps):
    half = HEAD_DIM // 2
    inv = ROPE_THETA ** (-np.arange(half, dtype=np.float64) / half)
    ang = np.asarray(pos, np.float64)[:, None] * inv[None, :]
    cos = np.cos(ang)
    sin = np.sin(ang)
    return (np.tile(np.concatenate([cos, cos], axis=1), (1, reps)).astype(np.float32),
            np.tile(np.concatenate([-sin, sin], axis=1), (1, reps)).astype(np.float32))


def _overlap_t(n_cmp_slots, n_cmp, n_blocks):
    c = np.arange(n_cmp_slots)[None, :] * CMP_STRIDE
    j = np.arange(SEL_SLOTS)[:, None]
    s = j * SEL_LEN
    ovl = (c < s + SEL_LEN) & (c + CMP_LEN > s) & (np.arange(n_cmp_slots)[None, :] < n_cmp) & (j < n_blocks)
    return jnp.asarray(ovl.astype(np.float32), BF16)


def _row_tile(m, want):
    t = want
    while m % t:
        t //= 2
    return t


def _layer(h, seq, g_attn, w_in, b_f, b_gate, cmpk, cmpv, g_fox, g_nsa, w_out, g_mlp, w_up, w_down, g_out):
    m, d = h.shape
    b = m // seq
    n_chunks = seq // CMP_STRIDE
    n_cmp = (seq - CMP_LEN) // CMP_STRIDE + 1
    n_blocks = seq // SEL_LEN
    assert seq % FOX_TILE == 0 and seq % NSA_TILE == 0 and n_chunks % LANES == 0
    assert SEL_TOPK <= n_blocks <= SEL_SLOTS
    assert n_cmp == n_chunks - 1

    cos_n, sin_n = _rope_tables(np.arange(seq), N_KV)
    cos_c, sin_c = _rope_tables(np.arange(n_chunks) * CMP_STRIDE + CMP_LEN - 1, N_KV)
    cos_t = np.ascontiguousarray(cos_n.T)
    sin_t = np.ascontiguousarray(sin_n.T)

    tm = _row_tile(seq, 512)
    w_nat, w_tr = _prep_w_in(w_in)
    kf, kc, vc, ks, kw, fg, qft, vft, qnt, vst, vwt, fgt = _inproj(
        h, g_attn[None, :], w_nat, w_tr, cos_n, sin_n, cos_t, sin_t, seq, tm)
    r3 = lambda a: a.reshape(b, seq, a.shape[-1])

    bf_pad = jnp.zeros((1, LANES), F32).at[0, :N_HEADS].set(b_f)
    bg_col = jnp.zeros((LANES,), F32).at[N_HEADS:4 * N_HEADS].set(b_gate)
    bg_t = jnp.broadcast_to(bg_col[:, None], (LANES, NSA_TILE))
    crow, kaug = _cumgate(r3(fg), bf_pad, r3(kf))

    kcmp, vcmpt = _compress(r3(kc), r3(vc), cos_c, sin_c, _prep_compress(*cmpk), _prep_compress(*cmpv))

    ofox = _fox(qft, kaug, vft, crow)
    onsa = _nsa(qnt, kcmp, vcmpt, r3(ks), vst, r3(kw), vwt, fgt, bg_t,
                _overlap_t(n_chunks, n_cmp, n_blocks), min(SEL_TOPK, n_blocks))

    w_out_p = jnp.concatenate([w_out[:WIDTH], _perm_heads(w_out[WIDTH:], 0)], axis=0).astype(BF16)
    return _post(h, ofox.reshape(m, WIDTH), onsa.reshape(m, WIDTH), g_fox[None, :], _perm_heads(g_nsa, 0)[None, :],
                 w_out_p, g_mlp[None, :], w_up.astype(BF16), w_down.astype(BF16), g_out[None, :],
                 _row_tile(seq, 512))


def kernel(x, g_attn, w_in, b_f, b_gate, cmpk_pe, cmpk_w1, cmpk_b1, cmpk_w2, cmpk_b2, cmpv_pe, cmpv_w1, cmpv_b1,
           cmpv_w2, cmpv_b2, g_fox, g_nsa, w_out, g_mlp, w_up, w_down, g_final):
    b, seq, d = x.shape
    depth = g_attn.shape[0]
    assert depth == 1, "the final rmsnorm is fused into the (single) layer's last kernel"
    h = x.reshape(b * seq, d)
    out = _layer(h, seq, g_attn[0], w_in[0], b_f[0], b_gate[0],
                 (cmpk_pe[0], cmpk_w1[0], cmpk_b1[0], cmpk_w2[0], cmpk_b2[0]),
                 (cmpv_pe[0], cmpv_w1[0], cmpv_b1[0], cmpv_w2[0], cmpv_b2[0]),
                 g_fox[0], g_nsa[0], w_out[0], g_mlp[0], w_up[0], w_down[0], g_final)
    return out.reshape(b, seq, d)
```

```python
import functools

import numpy as np
import jax
import jax.numpy as jnp
from jax import lax
from jax.experimental import pallas as pl
from jax.experimental.pallas import tpu as pltpu

F32 = jnp.float32
BF16 = jnp.bfloat16

HEAD_DIM = 64
N_HEADS = 8
N_KV = 2
GROUP = N_HEADS // N_KV
WIDTH = N_HEADS * HEAD_DIM
KV_WIDTH = N_KV * HEAD_DIM
CMP_LEN = 32
CMP_STRIDE = 16
CMP_HIDDEN = 256
SEL_LEN = 64
SEL_SHIFT = 6
SEL_TOPK = 16
SEL_SLOTS = 64
WINDOW = 512
ROPE_THETA = 10000.0
EPS = 1e-6
NEG = -1e30
MASK_FILL = -(2.0 ** 100)
FORCE_BONUS = 1e4
LOG2E = 1.4426950408889634
LANES = 128
HALF = LANES // 2
BF16_ROWS = 16

VMEM_LIMIT = 56 * 1024 * 1024


def _dot(a, b, precision=None):
    return jnp.dot(a, b, preferred_element_type=F32, precision=precision)


def _dot_nt(a, b, precision=None):
    return lax.dot_general(a, b, (((1,), (1,)), ((), ())), preferred_element_type=F32, precision=precision)


def _rms(x, g):
    return x * lax.rsqrt(jnp.mean(x * x, axis=-1, keepdims=True) + EPS) * g


def _rope(x, cos, sin_signed):
    w = x.shape[-1]
    lane = lax.broadcasted_iota(jnp.int32, (1, w), 1)
    first = (lane % HEAD_DIM) < (HEAD_DIM // 2)
    partner = jnp.where(first, pltpu.roll(x, w - HEAD_DIM // 2, 1), pltpu.roll(x, HEAD_DIM // 2, 1))
    return x * cos + partner * sin_signed


def _rope_t(x, cos, sin_signed):
    r = x.shape[0]
    row = lax.broadcasted_iota(jnp.int32, (r, 1), 0)
    first = (row % HEAD_DIM) < (HEAD_DIM // 2)
    partner = jnp.where(first, pltpu.roll(x, r - HEAD_DIM // 2, 0), pltpu.roll(x, HEAD_DIM // 2, 0))
    return x * cos + partner * sin_signed


def _split3(c):
    c1 = c.astype(BF16).astype(F32)
    r1 = c - c1
    c2 = r1.astype(BF16).astype(F32)
    return c1, c2, r1 - c2


NAT_KF, NAT_KC, NAT_VC, NAT_KS, NAT_KW, NAT_COLS = 0, 512, 640, 768, 896, 1024
TR_QF, TR_VF, TR_QN, TR_VS, TR_VW, TR_FG, TR_ROWS = 0, 512, 1024, 1536, 1664, 1792, 1920


def _inproj_kernel(x_ref, g_ref, wn_ref, wt_ref, cos_ref, sin_ref, cost_ref, sint_ref,
                   kf_ref, kc_ref, vc_ref, ks_ref, kw_ref,
                   qft_ref, vft_ref, qnt_ref, vst_ref, vwt_ref, fgt_ref):
    hb = _rms(x_ref[...], g_ref[...]).astype(BF16)

    def nat(lo, width):
        return _dot(hb, wn_ref[:, lo:lo + width])

    def tr(lo, rows):
        return _dot_nt(wt_ref[lo:lo + rows, :], hb)

    cos = cos_ref[...]
    sin = sin_ref[...]
    kf_ref[...] = nat(NAT_KF, WIDTH).astype(BF16)
    kc_ref[...] = nat(NAT_KC, KV_WIDTH)
    vc_ref[...] = nat(NAT_VC, KV_WIDTH)
    ks_ref[...] = _rope(nat(NAT_KS, KV_WIDTH), cos, sin).astype(BF16)
    kw_ref[...] = _rope(nat(NAT_KW, KV_WIDTH), cos, sin).astype(BF16)
    qft_ref[...] = tr(TR_QF, WIDTH).astype(BF16)
    vft_ref[...] = tr(TR_VF, WIDTH).astype(BF16)
    pairs = WIDTH // KV_WIDTH
    cos_rows = jnp.concatenate([cost_ref[...]] * pairs, axis=0)
    sin_rows = jnp.concatenate([sint_ref[...]] * pairs, axis=0)
    qnt_ref[...] = _rope_t(tr(TR_QN, WIDTH), cos_rows, sin_rows).astype(BF16)
    vst_ref[...] = tr(TR_VS, KV_WIDTH).astype(BF16)
    vwt_ref[...] = tr(TR_VW, KV_WIDTH).astype(BF16)
    fgt_ref[...] = tr(TR_FG, LANES)


def _inproj(x2, g_attn, w_nat, w_tr, cos_n, sin_n, cos_t, sin_t, seq, tm):
    m, d = x2.shape
    sblocks = seq // tm
    row = lambda i: (i, 0)
    col = lambda i: (0, i)
    fixed = lambda i: (0, 0)
    tab = lambda i: (i % sblocks, 0)
    tab_t = lambda i: (0, i % sblocks)
    sds = jax.ShapeDtypeStruct
    out_shape = [sds((m, WIDTH), BF16)] + [sds((m, KV_WIDTH), F32)] * 2 + [sds((m, KV_WIDTH), BF16)] * 2 + [
                 sds((WIDTH, m), BF16), sds((WIDTH, m), BF16), sds((WIDTH, m), BF16),
                 sds((KV_WIDTH, m), BF16), sds((KV_WIDTH, m), BF16), sds((LANES, m), F32)]
    out_specs = ([pl.BlockSpec((tm, WIDTH), row)] + [pl.BlockSpec((tm, KV_WIDTH), row)] * 4
                 + [pl.BlockSpec((WIDTH, tm), col)] * 3 + [pl.BlockSpec((KV_WIDTH, tm), col)] * 2
                 + [pl.BlockSpec((LANES, tm), col)])
    return pl.pallas_call(
        _inproj_kernel,
        grid=(m // tm,),
        in_specs=[pl.BlockSpec((tm, d), row), pl.BlockSpec((1, d), fixed),
                  pl.BlockSpec(w_nat.shape, fixed), pl.BlockSpec(w_tr.shape, fixed),
                  pl.BlockSpec((tm, KV_WIDTH), tab), pl.BlockSpec((tm, KV_WIDTH), tab),
                  pl.BlockSpec((KV_WIDTH, tm), tab_t), pl.BlockSpec((KV_WIDTH, tm), tab_t)],
        out_specs=out_specs,
        out_shape=out_shape,
        compiler_params=pltpu.CompilerParams(dimension_semantics=("parallel",), vmem_limit_bytes=VMEM_LIMIT),
        name="inproj",
    )(x2, g_attn, w_nat, w_tr, cos_n, sin_n, cos_t, sin_t)


CUM_BLOCK = 256
CUM_UNROLL = 4
N_BIAS = 3


def _bias_lane_base(h):
    return HALF if h % 2 == 0 else 0


def _bias_placement():
    p = np.zeros((LANES, N_HEADS * LANES), np.float32)
    for h in range(N_HEADS):
        base = h * LANES + _bias_lane_base(h)
        for part in range(N_BIAS):
            p[part * N_HEADS + h, base + N_BIAS + part] = -1.0
            p[N_BIAS * N_HEADS, base + part] = 1.0
    return jnp.asarray(p, BF16)


def _cumgate_kernel(fgt_ref, bf_ref, kf_ref, place_ref, crow_ref, kaug_ref):
    seq = fgt_ref.shape[1]
    r = lax.broadcasted_iota(jnp.int32, (CUM_BLOCK, CUM_BLOCK), 0)
    c = lax.broadcasted_iota(jnp.int32, (CUM_BLOCK, CUM_BLOCK), 1)
    tri = jnp.where(r >= c, 1.0, 0.0).astype(BF16)
    er = lax.broadcasted_iota(jnp.int32, (N_HEADS, LANES), 0)
    ec = lax.broadcasted_iota(jnp.int32, (N_HEADS, LANES), 1)
    pick = jnp.where(er == ec, 1.0, 0.0).astype(BF16)
    bias = bf_ref[...]
    lane = lax.broadcasted_iota(jnp.int32, (CUM_BLOCK, LANES), 1)

    def body(blk, carry):
        off = pl.multiple_of(blk * CUM_BLOCK, CUM_BLOCK)
        z = fgt_ref[:, pl.ds(off, CUM_BLOCK)].T + bias
        logf = jnp.minimum(z, 0.0) - jnp.log(1.0 + jnp.exp(-jnp.abs(z)))
        cs = sum(_dot(tri, part.astype(BF16)) for part in _split3(logf)) + carry
        c2 = cs * LOG2E
        c1, cb, cc = _split3(jnp.where(lane < N_HEADS, c2, 0.0))
        crow_ref[0, :, pl.ds(off, CUM_BLOCK)] = sum(_dot_nt(pick, part.astype(BF16)) for part in (c1, cb, cc))
        parts = (c1 + pltpu.roll(cb, N_HEADS, 1) + pltpu.roll(cc, 2 * N_HEADS, 1)
                 + jnp.where(lane == N_BIAS * N_HEADS, 1.0, 0.0))
        feat = _dot(parts.astype(BF16), place_ref[...]).astype(BF16)
        for h in range(N_HEADS):
            mine = (lane < HALF) if h % 2 == 0 else (lane >= HALF)
            kcol = kf_ref[0, pl.ds(off, CUM_BLOCK), (h // 2) * LANES:(h // 2 + 1) * LANES]
            kaug_ref[0, h, pl.ds(off, CUM_BLOCK), :] = jnp.where(mine, kcol, feat[:, h * LANES:(h + 1) * LANES])
        return cs[CUM_BLOCK - 1:CUM_BLOCK, :]

    lax.fori_loop(0, seq // CUM_BLOCK, body, jnp.zeros((1, LANES), F32), unroll=CUM_UNROLL)


def _cumgate(fgt, bf_pad, kf3):
    b, seq, _ = kf3.shape
    place = _bias_placement()
    return pl.pallas_call(
        _cumgate_kernel,
        grid=(b,),
        in_specs=[pl.BlockSpec((LANES, seq), lambda i: (0, i)), pl.BlockSpec((1, LANES), lambda i: (0, 0)),
                  pl.BlockSpec((1, seq, WIDTH), lambda i: (i, 0, 0)), pl.BlockSpec(place.shape, lambda i: (0, 0))],
        out_specs=[pl.BlockSpec((1, N_HEADS, seq), lambda i: (i, 0, 0)),
                   pl.BlockSpec((1, N_HEADS, seq, LANES), lambda i: (i, 0, 0, 0))],
        out_shape=[jax.ShapeDtypeStruct((b, N_HEADS, seq), F32),
                   jax.ShapeDtypeStruct((b, N_HEADS, seq, LANES), BF16)],
        compiler_params=pltpu.CompilerParams(dimension_semantics=("parallel",), vmem_limit_bytes=VMEM_LIMIT),
        name="cumgate",
    )(fgt, bf_pad, kf3, place)


def _compress_one(x_ref, pea_ref, peb_ref, wa_ref, wb_ref, b1_ref, w2_ref, b2_ref):
    n = x_ref.shape[1] // CMP_STRIDE
    xk = jnp.concatenate([x_ref[0, pl.ds(l, n, stride=CMP_STRIDE), :].astype(BF16) for l in range(CMP_STRIDE)],
                         axis=1)
    wa = wa_ref[...]
    wb = wb_ref[...]
    first = _dot(xk, wa)
    second = _dot(xk, wb)
    pe_term = _dot(pea_ref[...].astype(BF16), wa) + _dot(peb_ref[...].astype(BF16), wb)
    hidden = first + pltpu.roll(second, n - 1, 0) + pe_term[0:1, :] + b1_ref[...]
    act = jax.nn.gelu(hidden)
    return _dot(act.astype(BF16), w2_ref[...]) + b2_ref[...]


def _compress_kernel(kc_ref, vc_ref, cos_ref, sin_ref,
                     kpea, kpeb, kwa, kwb, kb1, kw2, kb2,
                     vpea, vpeb, vwa, vwb, vb1, vw2, vb2,
                     kcmp_ref, vcmpt_ref):
    kcmp = _compress_one(kc_ref, kpea, kpeb, kwa, kwb, kb1, kw2, kb2)
    kcmp_ref[0] = _rope(kcmp, cos_ref[...], sin_ref[...]).astype(BF16)
    vcmpt_ref[0] = _compress_one(vc_ref, vpea, vpeb, vwa, vwb, vb1, vw2, vb2).T.astype(BF16)


def _compress(kc3, vc3, cos_c, sin_c, kparams, vparams):
    b, seq, w = kc3.shape
    n = seq // CMP_STRIDE
    x_spec = pl.BlockSpec((1, seq, w), lambda i: (i, 0, 0))
    full = lambda a: pl.BlockSpec(a.shape, lambda i: (0,) * a.ndim)
    params = list(kparams) + list(vparams)
    return pl.pallas_call(
        _compress_kernel,
        grid=(b,),
        in_specs=[x_spec, x_spec, full(cos_c), full(sin_c)] + [full(p) for p in params],
        out_specs=[pl.BlockSpec((1, n, KV_WIDTH), lambda i: (i, 0, 0)),
                   pl.BlockSpec((1, KV_WIDTH, n), lambda i: (i, 0, 0))],
        out_shape=[jax.ShapeDtypeStruct((b, n, KV_WIDTH), BF16), jax.ShapeDtypeStruct((b, KV_WIDTH, n), BF16)],
        compiler_params=pltpu.CompilerParams(dimension_semantics=("parallel",), vmem_limit_bytes=VMEM_LIMIT),
        name="compress",
    )(kc3, vc3, cos_c, sin_c, *params)


ACC_ROWS = HEAD_DIM + BF16_ROWS


def _flash_step_t(s, vt, m_ref, acc_ref, idx):
    row = slice(idx, idx + 1)
    m_old = m_ref[row, :]
    m_new = jnp.maximum(m_old, jnp.max(s, axis=0, keepdims=True))
    alpha = jnp.exp2(m_old - m_new)
    p = jnp.exp2((s - m_new).astype(BF16))
    m_ref[row, :] = m_new
    vt_ones = jnp.concatenate([vt, jnp.ones((BF16_ROWS, vt.shape[1]), BF16)], axis=0)
    acc_ref[idx] = alpha * acc_ref[idx] + _dot(vt_ones, p)


def _flash_reset(m_ref, acc_ref):
    m_ref[...] = jnp.full(m_ref.shape, NEG, F32)
    acc_ref[...] = jnp.zeros(acc_ref.shape, F32)


def _flash_finish(acc_ref, idx):
    acc = acc_ref[idx]
    return acc[0:HEAD_DIM] * (1.0 / acc[HEAD_DIM:HEAD_DIM + 1])


def _for_tiles(n, unroll, visit):
    groups = n // unroll

    def body(jj, carry):
        visit(jj * unroll, unroll)
        return carry

    lax.fori_loop(0, groups, body, 0)
    done = groups * unroll
    rem = n - done
    p = unroll // 2
    while p >= 1:
        first = done + (rem & ~(2 * p - 1))

        @pl.when((rem & p) != 0)
        def _(first=first, p=p):
            visit(first, p)

        p //= 2


FOX_TILE = 256
FOX_KEYS = 256
FOX_AHEAD = 5
FOX_UNROLL = 4


def _fox_kernel(qt_ref, kaug_ref, vt_ref, crow_ref, o_ref, qaug_ref, m_ref, acc_ref):
    t = FOX_TILE
    tk = FOX_KEYS
    assert t == tk
    i = pl.program_id(1)
    kr = lax.broadcasted_iota(jnp.int32, (tk, t), 0)
    qc = lax.broadcasted_iota(jnp.int32, (tk, t), 1)
    frow = lax.broadcasted_iota(jnp.int32, (BF16_ROWS, t), 0)
    pad = jnp.zeros((HALF - BF16_ROWS, t), BF16)

    _flash_reset(m_ref, acc_ref)
    for h in range(N_HEADS):
        c1, c2, c3 = _split3(crow_ref[0, h:h + 1, :])
        feat = jnp.where(frow == 0, c1, jnp.where(frow == 1, c2, jnp.where(frow == 2, c3,
               jnp.where(frow < 2 * N_BIAS, 1.0, 0.0)))).astype(BF16)
        qh = qt_ref[h * HEAD_DIM:(h + 1) * HEAD_DIM, :]
        parts = [qh, feat, pad] if _bias_lane_base(h) == HALF else [feat, pad, qh]
        qaug_ref[h] = jnp.concatenate(parts, axis=0)

    def step(tiles):
        chains = [(off, mask, h) for off, mask in tiles for h in range(N_HEADS)]

        def scores(c):
            off, _, h = chains[c]
            return _dot(kaug_ref[0, h, pl.ds(off, tk), :], qaug_ref[h])

        pending = {c: scores(c) for c in range(FOX_AHEAD)}
        for c, (off, mask, h) in enumerate(chains):
            if c + FOX_AHEAD < len(chains):
                pending[c + FOX_AHEAD] = scores(c + FOX_AHEAD)
            s = pending.pop(c)
            if mask is not None:
                s = jnp.where(mask, s, MASK_FILL)
            _flash_step_t(s, vt_ref[h * HEAD_DIM:(h + 1) * HEAD_DIM, pl.ds(off, tk)], m_ref, acc_ref, h)

    def tile_off(j):
        return pl.multiple_of(j * tk, tk)

    groups = i // FOX_UNROLL

    def body(jj, carry):
        step([(tile_off(jj * FOX_UNROLL + u), None) for u in range(FOX_UNROLL)])
        return carry

    lax.fori_loop(0, groups, body, 0)
    done = groups * FOX_UNROLL
    for left in range(FOX_UNROLL):
        @pl.when(i - done == left)
        def _(left=left):
            step([(tile_off(done + u), None) for u in range(left)] + [(tile_off(i), kr <= qc)])

    for col in range(WIDTH // LANES):
        pair = [_flash_finish(acc_ref, h) for h in (2 * col, 2 * col + 1)]
        o_ref[0, :, col * LANES:(col + 1) * LANES] = jnp.concatenate(pair, axis=0).T.astype(BF16)


def _fox(qft, kaug, vft, crow):
    b, _, seq, _ = kaug.shape
    t = FOX_TILE
    nq = seq // t
    return pl.pallas_call(
        _fox_kernel,
        grid=(b, nq),
        in_specs=[pl.BlockSpec((WIDTH, t), lambda bi, i: (0, bi * nq + i)),
                  pl.BlockSpec((1, N_HEADS, seq, LANES), lambda bi, i: (bi, 0, 0, 0)),
                  pl.BlockSpec((WIDTH, seq), lambda bi, i: (0, bi)),
                  pl.BlockSpec((1, N_HEADS, t), lambda bi, i: (bi, 0, i))],
        out_specs=pl.BlockSpec((1, t, WIDTH), lambda bi, i: (bi, i, 0)),
        out_shape=jax.ShapeDtypeStruct((b, seq, WIDTH), BF16),
        scratch_shapes=[pltpu.VMEM((N_HEADS, LANES, t), BF16), pltpu.VMEM((N_HEADS, t), F32),
                        pltpu.VMEM((N_HEADS, ACC_ROWS, t), F32)],
        compiler_params=pltpu.CompilerParams(dimension_semantics=("parallel", "arbitrary"),
                                             vmem_limit_bytes=VMEM_LIMIT),
        name="fox",
    )(qft, kaug, vft, crow)


NSA_TILE = 256
NSA_KEYS = 128
STAT_ROWS = 8
NSA_UNROLL = 8
NSA_AHEAD = 4


def _nsa_kernel(qt_ref, kcmp_ref, vcmpt_ref, ks_ref, vst_ref, kw_ref, vwt_ref, fgt_ref, bg_ref, ovl_ref, o_ref,
                qaug_ref, m_ref, acc_ref, ocmp_ref, *, topk):
    t = NSA_TILE
    tk = NSA_KEYS
    per_q = t // tk
    wide = GROUP * t
    n_cmp = kcmp_ref.shape[1]
    i = pl.program_id(1)
    start = i * t
    lane = lax.broadcasted_iota(jnp.int32, (1, LANES), 1)
    mine = [lane < HALF, lane >= HALF]
    zero_bf = jnp.zeros((), BF16)

    q4 = jnp.concatenate([qt_ref[n * LANES:(n + 1) * LANES, :] for n in range(GROUP)], axis=1)
    qpos4 = start + (lax.broadcasted_iota(jnp.int32, (1, wide), 1) & (t - 1))
    cur = (start + lax.broadcasted_iota(jnp.int32, (1, t), 1)) >> SEL_SHIFT
    sub8 = lax.broadcasted_iota(jnp.int32, (STAT_ROWS, t), 0)

    def cmp_rank(n_rows, n_slots):
        cmp_end = lax.broadcasted_iota(jnp.int32, (n_rows, 1), 0) * CMP_STRIDE + (CMP_LEN - 1)
        vis = cmp_end <= qpos4
        slot = lax.broadcasted_iota(jnp.int32, (n_slots, t), 0)
        valid = slot <= cur
        forced = (slot == 0) | (slot == cur) | (slot == cur - 1)
        for g in range(N_KV):
            s = jnp.where(vis, _dot(jnp.where(mine[g], kcmp_ref[0, 0:n_rows, :], zero_bf), q4), NEG)
            e = jnp.where(vis, jnp.exp2(s - jnp.max(s, axis=0, keepdims=True)), 0.0)
            denom = jnp.sum(e, axis=0, keepdims=True)
            p = e * jnp.where(denom > 0.0, 1.0 / denom, 0.0)
            ocmp_ref[g] = _dot(vcmpt_ref[0, g * HEAD_DIM:(g + 1) * HEAD_DIM, 0:n_rows], p.astype(BF16))
            psum = p[:, 0:t]
            for n in range(1, GROUP):
                psum = psum + p[:, n * t:(n + 1) * t]
            ovl = ovl_ref[0:n_slots, 0:n_rows]
            imp = sum(_dot(ovl, part.astype(BF16)) for part in _split3(psum))
            score = jnp.where(valid, imp + jnp.where(forced, FORCE_BONUS, 0.0), -1.0)
            tiles = [score[r:r + STAT_ROWS, :] for r in range(0, n_slots, STAT_ROWS)]
            ranks = [jnp.zeros((STAT_ROWS, t), F32) for _ in tiles]
            for jp in range(n_slots):
                other = jnp.broadcast_to(score[jp:jp + 1, :], (STAT_ROWS, t))
                for k, tile in enumerate(tiles):
                    first = k * STAT_ROWS
                    ge = jnp.where(other >= tile, 1.0, 0.0)
                    gt = jnp.where(other > tile, 1.0, 0.0)
                    if first > jp:
                        ahead = ge
                    elif first + STAT_ROWS - 1 <= jp:
                        ahead = gt
                    else:
                        ahead = jnp.where(sub8 + first > jp, ge, gt)
                    ranks[k] = ranks[k] + ahead
            rank = jnp.concatenate(ranks, axis=0)
            selb = jnp.where(rank < topk, 0.0, MASK_FILL).astype(BF16)
            if n_slots < SEL_SLOTS:
                selb = jnp.concatenate([selb, jnp.full((SEL_SLOTS - n_slots, t), MASK_FILL, BF16)], axis=0)
            selb4 = jnp.concatenate([selb] * GROUP, axis=1)
            qaug_ref[g] = jnp.concatenate([q4[0:HALF], selb4] if g == 0 else [selb4, q4[HALF:LANES]], axis=0)

    slots_per_tile = t // SEL_LEN
    lo = 0
    for n_slots in range(SEL_SLOTS // 4, SEL_SLOTS + 1, SEL_SLOTS // 4):
        hi = n_slots // slots_per_tile
        n_rows = min(n_cmp, -(-(hi * t // CMP_STRIDE) // LANES) * LANES)
        last = n_slots == SEL_SLOTS

        @pl.when((i >= lo) if last else ((i >= lo) & (i < hi)))
        def _(n_rows=n_rows, n_slots=n_slots):
            cmp_rank(n_rows, n_slots)

        lo = hi
    o_cmp = [ocmp_ref[g] for g in range(N_KV)]

    kr = lax.broadcasted_iota(jnp.int32, (tk, wide), 0)
    q_local = lax.broadcasted_iota(jnp.int32, (tk, wide), 1) & (t - 1)
    key_row = lax.broadcasted_iota(jnp.int32, (tk, LANES), 0)
    lane_full = lax.broadcasted_iota(jnp.int32, (tk, LANES), 1)
    slot_minus_row = [lane_full - HALF - (key_row >> SEL_SHIFT), lane_full - (key_row >> SEL_SHIFT)]

    def sel_score(off, g):
        onehot = jnp.where(slot_minus_row[g] == (off >> SEL_SHIFT), 1.0, 0.0).astype(BF16)
        return _dot(jnp.where(mine[g], ks_ref[0, pl.ds(off, tk), :], onehot), qaug_ref[g])

    def win_score(off, g):
        return _dot(jnp.where(mine[g], kw_ref[0, pl.ds(off, tk), :], zero_bf), q4)

    def sel_chains(tiles):
        return [(sel_score, vst_ref, g, g, off, mask) for off, mask in tiles for g in range(N_KV)]

    def win_chains(tiles):
        return [(win_score, vwt_ref, g, N_KV + g, off, mask) for off, mask in tiles for g in range(N_KV)]

    def sweep(chains):
        def issue(c):
            score, _, g, _, off, _ = chains[c]
            return score(off, g)

        pending = {c: issue(c) for c in range(min(NSA_AHEAD, len(chains)))}
        for c, (_, vt_ref_, g, slot_id, off, mask) in enumerate(chains):
            if c + NSA_AHEAD < len(chains):
                pending[c + NSA_AHEAD] = issue(c + NSA_AHEAD)
            s = pending.pop(c)
            if mask is not None:
                s = jnp.where(mask, s, MASK_FILL)
            _flash_step_t(s, vt_ref_[g * HEAD_DIM:(g + 1) * HEAD_DIM, pl.ds(off, tk)], m_ref, acc_ref, slot_id)

    def tile_off(j):
        return pl.multiple_of(j * tk, tk)

    n_before = i * per_q
    _flash_reset(m_ref, acc_ref)
    _for_tiles(n_before, NSA_UNROLL, lambda first, count: sweep(
        sel_chains([(tile_off(first + u), None) for u in range(count)])))

    own = [(tile_off(n_before + u), kr + u * tk <= q_local) for u in range(per_q)]
    n_back = WINDOW // tk

    @pl.when(n_before >= n_back)
    def _():
        back = [(tile_off(n_before - n_back + u), (q_local - kr < u * tk) if u < per_q else None)
                for u in range(n_back)]
        sweep(sel_chains(own) + win_chains(back + own))

    @pl.when(n_before < n_back)
    def _():
        sweep(sel_chains(own))

        def win_body(j, carry):
            sweep(win_chains([(tile_off(j), None)]))
            return carry

        lax.fori_loop(0, n_before, win_body, 0)
        sweep(win_chains(own))

    o_slc = [_flash_finish(acc_ref, g) for g in range(N_KV)]
    o_win = [_flash_finish(acc_ref, N_KV + g) for g in range(N_KV)]

    gates = jax.nn.sigmoid(fgt_ref[...] + bg_ref[...])
    for n in range(GROUP):
        cols = slice(n * t, (n + 1) * t)
        mixed = []
        for g in range(N_KV):
            base = N_HEADS + (g * GROUP + n) * 3
            mixed.append(gates[base:base + 1, :] * o_cmp[g][:, cols]
                         + gates[base + 1:base + 2, :] * o_slc[g][:, cols]
                         + gates[base + 2:base + 3, :] * o_win[g][:, cols])
        o_ref[0, :, n * LANES:(n + 1) * LANES] = jnp.concatenate(mixed, axis=0).T.astype(BF16)


def _nsa(qnt, kcmp, vcmpt, ks, vst, kw, vwt, fgt, bg_t, ovl_t, topk):
    b, seq, _ = ks.shape
    t = NSA_TILE
    nq = seq // t
    n_cmp = kcmp.shape[1]
    whole = lambda bi, i: (bi, 0, 0)
    whole_t = lambda bi, i: (0, bi)
    tile_t = lambda bi, i: (0, bi * nq + i)
    fixed = lambda bi, i: (0, 0)
    return pl.pallas_call(
        functools.partial(_nsa_kernel, topk=topk),
        grid=(b, nq),
        in_specs=[pl.BlockSpec((WIDTH, t), tile_t),
                  pl.BlockSpec((1, n_cmp, KV_WIDTH), whole), pl.BlockSpec((1, KV_WIDTH, n_cmp), whole),
                  pl.BlockSpec((1, seq, KV_WIDTH), whole), pl.BlockSpec((KV_WIDTH, seq), whole_t),
                  pl.BlockSpec((1, seq, KV_WIDTH), whole), pl.BlockSpec((KV_WIDTH, seq), whole_t),
                  pl.BlockSpec((LANES, t), tile_t), pl.BlockSpec((LANES, t), fixed),
                  pl.BlockSpec(ovl_t.shape, fixed)],
        out_specs=pl.BlockSpec((1, t, WIDTH), lambda bi, i: (bi, i, 0)),
        out_shape=jax.ShapeDtypeStruct((b, seq, WIDTH), BF16),
        scratch_shapes=[pltpu.VMEM((N_KV, LANES, GROUP * t), BF16), pltpu.VMEM((STAT_ROWS, GROUP * t), F32),
                        pltpu.VMEM((2 * N_KV, ACC_ROWS, GROUP * t), F32),
                        pltpu.VMEM((N_KV, HEAD_DIM, GROUP * t), F32)],
        compiler_params=pltpu.CompilerParams(dimension_semantics=("parallel", "arbitrary"),
                                             vmem_limit_bytes=VMEM_LIMIT),
        name="nsa",
    )(qnt, kcmp, vcmpt, ks, vst, kw, vwt, fgt, bg_t, ovl_t)


def _post_kernel(x_ref, of_ref, on_ref, gf_ref, gn_ref, wo_ref, gm_ref, wu_ref, wd_ref, gl_ref, o_ref):
    yf = _rms(of_ref[...].astype(F32), gf_ref[...]).astype(BF16)
    yn = _rms(on_ref[...].astype(F32), gn_ref[...]).astype(BF16)
    h1 = x_ref[...] + _dot(yf, wo_ref[0:WIDTH, :]) + _dot(yn, wo_ref[WIDTH:2 * WIDTH, :])
    u = _dot(_rms(h1, gm_ref[...]).astype(BF16), wu_ref[...])
    act = jnp.square(jnp.maximum(u, 0.0)).astype(BF16)
    h2 = h1 + _dot(act, wd_ref[...])
    o_ref[...] = _rms(h2, gl_ref[...])


def _post(x2, ofox, onsa, g_fox, g_nsa, w_out, g_mlp, w_up, w_down, g_final, tm):
    m, d = x2.shape
    row = lambda i: (i, 0)
    fixed = lambda i: (0, 0)
    full = lambda a: pl.BlockSpec(a.shape, fixed, pipeline_mode=pl.Buffered(1))
    return pl.pallas_call(
        _post_kernel,
        grid=(m // tm,),
        in_specs=[pl.BlockSpec((tm, d), row), pl.BlockSpec((tm, WIDTH), row), pl.BlockSpec((tm, WIDTH), row),
                  full(g_fox), full(g_nsa), full(w_out), full(g_mlp), full(w_up), full(w_down), full(g_final)],
        out_specs=pl.BlockSpec((tm, d), row),
        out_shape=jax.ShapeDtypeStruct((m, d), F32),
        compiler_params=pltpu.CompilerParams(dimension_semantics=("parallel",), vmem_limit_bytes=VMEM_LIMIT),
        name="post",
    )(x2, ofox, onsa, g_fox, g_nsa, w_out, g_mlp, w_up, w_down, g_final)


_NSA_PERM = np.array([(p % 2) * GROUP + p // 2 for p in range(N_HEADS)])


def _perm_heads(a, axis):
    shape = a.shape
    a = a.reshape(shape[:axis] + (N_HEADS, HEAD_DIM) + shape[axis + 1:])
    a = jnp.take(a, _NSA_PERM, axis=axis)
    return a.reshape(shape)


def _prep_w_in(w):
    qscale = HEAD_DIM ** -0.5 * LOG2E
    sizes = [WIDTH, WIDTH, WIDTH, N_HEADS, WIDTH] + [KV_WIDTH] * 6 + [3 * N_HEADS]
    offs = np.cumsum([0] + sizes)
    qf, kf, vf, fl, qn, kc, vc, ks, vs, kw, vw, gate = [w[:, offs[k]:offs[k + 1]] for k in range(len(sizes))]
    fg = jnp.concatenate([fl, gate, jnp.zeros((w.shape[0], LANES - 4 * N_HEADS), w.dtype)], axis=1)
    w_nat = jnp.concatenate([kf, kc, vc, ks, kw], axis=1).astype(BF16)
    w_tr = jnp.concatenate([qf * qscale, vf, _perm_heads(qn * qscale, 1), vs, vw, fg], axis=1).T.astype(BF16)
    assert w_nat.shape[1] == NAT_COLS and w_tr.shape[0] == TR_ROWS
    return w_nat, w_tr


def _prep_compress(pe, w1, b1, w2, b2):
    half = CMP_LEN // 2
    eye = jnp.eye(N_KV, dtype=F32)

    def expand_w1(wpart):
        w3 = wpart.reshape(half, HEAD_DIM, CMP_HIDDEN)
        return jnp.einsum('ldh,gk->lgdkh', w3, eye).reshape(half * KV_WIDTH, N_KV * CMP_HIDDEN).astype(BF16)

    def expand_pe(ppart):
        flat = jnp.broadcast_to(ppart[:, None, :], (half, N_KV, HEAD_DIM)).reshape(1, half * KV_WIDTH)
        return jnp.broadcast_to(flat, (8, half * KV_WIDTH))

    w2b = jnp.einsum('hd,gk->ghkd', w2, eye).reshape(N_KV * CMP_HIDDEN, KV_WIDTH).astype(BF16)
    return (expand_pe(pe[:half]), expand_pe(pe[half:]),
            expand_w1(w1[:half * HEAD_DIM]), expand_w1(w1[half * HEAD_DIM:]),
            jnp.tile(b1, N_KV)[None, :], w2b, jnp.tile(b2, N_KV)[None, :])


def _rope_tables(pos, reps):
    half = HEAD_DIM // 2
    inv = ROPE_THETA ** (-np.arange(half, dtype=np.float64) / half)
    ang = np.asarray(pos, np.float64)[:, None] * inv[None, :]
    cos = np.cos(ang)
    sin = np.sin(ang)
    return (np.tile(np.concatenate([cos, cos], axis=1), (1, reps)).astype(np.float32),
            np.tile(np.concatenate([-sin, sin], axis=1), (1, reps)).astype(np.float32))


def _overlap_t(n_cmp_slots, n_cmp, n_blocks):
    c = np.arange(n_cmp_slots)[None, :] * CMP_STRIDE
    j = np.arange(SEL_SLOTS)[:, None]
    s = j * SEL_LEN
    ovl = (c < s + SEL_LEN) & (c + CMP_LEN > s) & (np.arange(n_cmp_slots)[None, :] < n_cmp) & (j < n_blocks)
    return jnp.asarray(ovl.astype(np.float32), BF16)


def _row_tile(m, want):
    t = want
    while m % t:
        t //= 2
    return t


def _layer(h, seq, g_attn, w_in, b_f, b_gate, cmpk, cmpv, g_fox, g_nsa, w_out, g_mlp, w_up, w_down, g_out):
    m, d = h.shape
    b = m // seq
    n_chunks = seq // CMP_STRIDE
    n_cmp = (seq - CMP_LEN) // CMP_STRIDE + 1
    n_blocks = seq // SEL_LEN
    assert seq % FOX_TILE == 0 and seq % NSA_TILE == 0 and n_chunks % LANES == 0
    assert SEL_TOPK <= n_blocks <= SEL_SLOTS
    assert n_cmp == n_chunks - 1

    cos_n, sin_n = _rope_tables(np.arange(seq), N_KV)
    cos_c, sin_c = _rope_tables(np.arange(n_chunks) * CMP_STRIDE + CMP_LEN - 1, N_KV)
    cos_t = np.ascontiguousarray(cos_n.T)
    sin_t = np.ascontiguousarray(sin_n.T)

    tm = _row_tile(seq, 512)
    w_nat, w_tr = _prep_w_in(w_in)
    kf, kc, vc, ks, kw, qft, vft, qnt, vst, vwt, fgt = _inproj(
        h, g_attn[None, :], w_nat, w_tr, cos_n, sin_n, cos_t, sin_t, seq, tm)
    r3 = lambda a: a.reshape(b, seq, a.shape[-1])

    bf_pad = jnp.zeros((1, LANES), F32).at[0, :N_HEADS].set(b_f)
    bg_col = jnp.zeros((LANES,), F32).at[N_HEADS:4 * N_HEADS].set(b_gate)
    bg_t = jnp.broadcast_to(bg_col[:, None], (LANES, NSA_TILE))
    crow, kaug = _cumgate(fgt, bf_pad, r3(kf))

    kcmp, vcmpt = _compress(r3(kc), r3(vc), cos_c, sin_c, _prep_compress(*cmpk), _prep_compress(*cmpv))

    ofox = _fox(qft, kaug, vft, crow)
    onsa = _nsa(qnt, kcmp, vcmpt, r3(ks), vst, r3(kw), vwt, fgt, bg_t,
                _overlap_t(n_chunks, n_cmp, n_blocks), min(SEL_TOPK, n_blocks))

    w_out_p = jnp.concatenate([w_out[:WIDTH], _perm_heads(w_out[WIDTH:], 0)], axis=0).astype(BF16)
    return _post(h, ofox.reshape(m, WIDTH), onsa.reshape(m, WIDTH), g_fox[None, :], _perm_heads(g_nsa, 0)[None, :],
                 w_out_p, g_mlp[None, :], w_up.astype(BF16), w_down.astype(BF16), g_out[None, :],
                 _row_tile(seq, 512))


def kernel(x, g_attn, w_in, b_f, b_gate, cmpk_pe, cmpk_w1, cmpk_b1, cmpk_w2, cmpk_b2, cmpv_pe, cmpv_w1, cmpv_b1,
           cmpv_w2, cmpv_b2, g_fox, g_nsa, w_out, g_mlp, w_up, w_down, g_final):
    b, seq, d = x.shape
    depth = g_attn.shape[0]
    assert depth == 1, "the final rmsnorm is fused into the (single) layer's last kernel"
    h = x.reshape(b * seq, d)
    out = _layer(h, seq, g_attn[0], w_in[0], b_f[0], b_gate[0],
                 (cmpk_pe[0], cmpk_w1[0], cmpk_b1[0], cmpk_w2[0], cmpk_b2[0]),
                 (cmpv_pe[0], cmpv_w1[0], cmpv_b1[0], cmpv_w2[0], cmpv_b2[0]),
                 g_fox[0], g_nsa[0], w_out[0], g_mlp[0], w_up[0], w_down[0], g_final)
    return out.reshape(b, seq, d)
```

```python
import functools

import numpy as np
import jax
import jax.numpy as jnp
from jax import lax
from jax.experimental import pallas as pl
from jax.experimental.pallas import tpu as pltpu

F32 = jnp.float32
BF16 = jnp.bfloat16

HEAD_DIM = 64
N_HEADS = 8
N_KV = 2
GROUP = N_HEADS // N_KV
WIDTH = N_HEADS * HEAD_DIM
KV_WIDTH = N_KV * HEAD_DIM
CMP_LEN = 32
CMP_STRIDE = 16
CMP_HIDDEN = 256
SEL_LEN = 64
SEL_SHIFT = 6
SEL_TOPK = 16
SEL_SLOTS = 64
WINDOW = 512
ROPE_THETA = 10000.0
EPS = 1e-6
NEG = -1e30
MASK_FILL = -(2.0 ** 100)
FORCE_BONUS = 1e4
LOG2E = 1.4426950408889634
LANES = 128
HALF = LANES // 2
BF16_ROWS = 16

VMEM_LIMIT = 56 * 1024 * 1024


def _dot(a, b, precision=None):
    return jnp.dot(a, b, preferred_element_type=F32, precision=precision)


def _dot_nt(a, b, precision=None):
    return lax.dot_general(a, b, (((1,), (1,)), ((), ())), preferred_element_type=F32, precision=precision)


def _rms(x, g):
    return x * lax.rsqrt(jnp.mean(x * x, axis=-1, keepdims=True) + EPS) * g


def _rope(x, cos, sin_signed):
    w = x.shape[-1]
    lane = lax.broadcasted_iota(jnp.int32, (1, w), 1)
    first = (lane % HEAD_DIM) < (HEAD_DIM // 2)
    partner = jnp.where(first, pltpu.roll(x, w - HEAD_DIM // 2, 1), pltpu.roll(x, HEAD_DIM // 2, 1))
    return x * cos + partner * sin_signed


def _rope_t(x, cos, sin_signed):
    r = x.shape[0]
    row = lax.broadcasted_iota(jnp.int32, (r, 1), 0)
    first = (row % HEAD_DIM) < (HEAD_DIM // 2)
    partner = jnp.where(first, pltpu.roll(x, r - HEAD_DIM // 2, 0), pltpu.roll(x, HEAD_DIM // 2, 0))
    return x * cos + partner * sin_signed


def _split3(c):
    c1 = c.astype(BF16).astype(F32)
    r1 = c - c1
    c2 = r1.astype(BF16).astype(F32)
    return c1, c2, r1 - c2


NAT_KF, NAT_KC, NAT_VC, NAT_KS, NAT_KW, NAT_COLS = 0, 512, 640, 768, 896, 1024
TR_QF, TR_VF, TR_QN, TR_VS, TR_VW, TR_FG, TR_ROWS = 0, 512, 1024, 1536, 1664, 1792, 1920


def _inproj_kernel(x_ref, g_ref, wn_ref, wt_ref, cos_ref, sin_ref, cost_ref, sint_ref,
                   kf_ref, kc_ref, vc_ref, ks_ref, kw_ref,
                   qft_ref, vft_ref, qnt_ref, vst_ref, vwt_ref, fgt_ref):
    hb = _rms(x_ref[...], g_ref[...]).astype(BF16)

    def nat(lo, width):
        return _dot(hb, wn_ref[:, lo:lo + width])

    def tr(lo, rows):
        return _dot_nt(wt_ref[lo:lo + rows, :], hb)

    cos = cos_ref[...]
    sin = sin_ref[...]
    kf_ref[...] = nat(NAT_KF, WIDTH).astype(BF16)
    kc_ref[...] = nat(NAT_KC, KV_WIDTH)
    vc_ref[...] = nat(NAT_VC, KV_WIDTH)
    ks_ref[...] = _rope(nat(NAT_KS, KV_WIDTH), cos, sin).astype(BF16)
    kw_ref[...] = _rope(nat(NAT_KW, KV_WIDTH), cos, sin).astype(BF16)
    qft_ref[...] = tr(TR_QF, WIDTH).astype(BF16)
    vft_ref[...] = tr(TR_VF, WIDTH).astype(BF16)
    pairs = WIDTH // KV_WIDTH
    cos_rows = jnp.concatenate([cost_ref[...]] * pairs, axis=0)
    sin_rows = jnp.concatenate([sint_ref[...]] * pairs, axis=0)
    qnt_ref[...] = _rope_t(tr(TR_QN, WIDTH), cos_rows, sin_rows).astype(BF16)
    vst_ref[...] = tr(TR_VS, KV_WIDTH).astype(BF16)
    vwt_ref[...] = tr(TR_VW, KV_WIDTH).astype(BF16)
    fgt_ref[...] = tr(TR_FG, LANES)


def _inproj(x2, g_attn, w_nat, w_tr, cos_n, sin_n, cos_t, sin_t, seq, tm):
    m, d = x2.shape
    sblocks = seq // tm
    row = lambda i: (i, 0)
    col = lambda i: (0, i)
    fixed = lambda i: (0, 0)
    tab = lambda i: (i % sblocks, 0)
    tab_t = lambda i: (0, i % sblocks)
    sds = jax.ShapeDtypeStruct
    out_shape = [sds((m, WIDTH), BF16)] + [sds((m, KV_WIDTH), F32)] * 2 + [sds((m, KV_WIDTH), BF16)] * 2 + [
                 sds((WIDTH, m), BF16), sds((WIDTH, m), BF16), sds((WIDTH, m), BF16),
                 sds((KV_WIDTH, m), BF16), sds((KV_WIDTH, m), BF16), sds((LANES, m), F32)]
    out_specs = ([pl.BlockSpec((tm, WIDTH), row)] + [pl.BlockSpec((tm, KV_WIDTH), row)] * 4
                 + [pl.BlockSpec((WIDTH, tm), col)] * 3 + [pl.BlockSpec((KV_WIDTH, tm), col)] * 2
                 + [pl.BlockSpec((LANES, tm), col)])
    return pl.pallas_call(
        _inproj_kernel,
        grid=(m // tm,),
        in_specs=[pl.BlockSpec((tm, d), row), pl.BlockSpec((1, d), fixed),
                  pl.BlockSpec(w_nat.shape, fixed), pl.BlockSpec(w_tr.shape, fixed),
                  pl.BlockSpec((tm, KV_WIDTH), tab), pl.BlockSpec((tm, KV_WIDTH), tab),
                  pl.BlockSpec((KV_WIDTH, tm), tab_t), pl.BlockSpec((KV_WIDTH, tm), tab_t)],
        out_specs=out_specs,
        out_shape=out_shape,
        compiler_params=pltpu.CompilerParams(dimension_semantics=("parallel",), vmem_limit_bytes=VMEM_LIMIT),
        name="inproj",
    )(x2, g_attn, w_nat, w_tr, cos_n, sin_n, cos_t, sin_t)


CUM_BLOCK = 256
CUM_UNROLL = 4
N_BIAS = 3


def _bias_lane_base(h):
    return HALF if h % 2 == 0 else 0


def _bias_placement():
    p = np.zeros((LANES, N_HEADS * LANES), np.float32)
    for h in range(N_HEADS):
        base = h * LANES + _bias_lane_base(h)
        for part in range(N_BIAS):
            p[part * N_HEADS + h, base + N_BIAS + part] = -1.0
            p[N_BIAS * N_HEADS, base + part] = 1.0
    return jnp.asarray(p, BF16)


def _cumgate_kernel(fgt_ref, bf_ref, kf_ref, place_ref, crow_ref, kaug_ref):
    seq = fgt_ref.shape[1]
    r = lax.broadcasted_iota(jnp.int32, (CUM_BLOCK, CUM_BLOCK), 0)
    c = lax.broadcasted_iota(jnp.int32, (CUM_BLOCK, CUM_BLOCK), 1)
    tri = jnp.where(r >= c, 1.0, 0.0).astype(BF16)
    er = lax.broadcasted_iota(jnp.int32, (N_HEADS, LANES), 0)
    ec = lax.broadcasted_iota(jnp.int32, (N_HEADS, LANES), 1)
    pick = jnp.where(er == ec, 1.0, 0.0).astype(BF16)
    bias = bf_ref[...]
    lane = lax.broadcasted_iota(jnp.int32, (CUM_BLOCK, LANES), 1)

    def body(blk, carry):
        off = pl.multiple_of(blk * CUM_BLOCK, CUM_BLOCK)
        z = fgt_ref[:, pl.ds(off, CUM_BLOCK)].T + bias
        logf = jnp.minimum(z, 0.0) - jnp.log(1.0 + jnp.exp(-jnp.abs(z)))
        cs = sum(_dot(tri, part.astype(BF16)) for part in _split3(logf)) + carry
        c2 = cs * LOG2E
        c1, cb, cc = _split3(jnp.where(lane < N_HEADS, c2, 0.0))
        crow_ref[0, :, pl.ds(off, CUM_BLOCK)] = sum(_dot_nt(pick, part.astype(BF16)) for part in (c1, cb, cc))
        parts = (c1 + pltpu.roll(cb, N_HEADS, 1) + pltpu.roll(cc, 2 * N_HEADS, 1)
                 + jnp.where(lane == N_BIAS * N_HEADS, 1.0, 0.0))
        feat = _dot(parts.astype(BF16), place_ref[...]).astype(BF16)
        for h in range(N_HEADS):
            mine = (lane < HALF) if h % 2 == 0 else (lane >= HALF)
            kcol = kf_ref[0, pl.ds(off, CUM_BLOCK), (h // 2) * LANES:(h // 2 + 1) * LANES]
            kaug_ref[0, h, pl.ds(off, CUM_BLOCK), :] = jnp.where(mine, kcol, feat[:, h * LANES:(h + 1) * LANES])
        return cs[CUM_BLOCK - 1:CUM_BLOCK, :]

    lax.fori_loop(0, seq // CUM_BLOCK, body, jnp.zeros((1, LANES), F32), unroll=CUM_UNROLL)


def _cumgate(fgt, bf_pad, kf3):
    b, seq, _ = kf3.shape
    place = _bias_placement()
    return pl.pallas_call(
        _cumgate_kernel,
        grid=(b,),
        in_specs=[pl.BlockSpec((LANES, seq), lambda i: (0, i)), pl.BlockSpec((1, LANES), lambda i: (0, 0)),
                  pl.BlockSpec((1, seq, WIDTH), lambda i: (i, 0, 0)), pl.BlockSpec(place.shape, lambda i: (0, 0))],
        out_specs=[pl.BlockSpec((1, N_HEADS, seq), lambda i: (i, 0, 0)),
                   pl.BlockSpec((1, N_HEADS, seq, LANES), lambda i: (i, 0, 0, 0))],
        out_shape=[jax.ShapeDtypeStruct((b, N_HEADS, seq), F32),
                   jax.ShapeDtypeStruct((b, N_HEADS, seq, LANES), BF16)],
        compiler_params=pltpu.CompilerParams(dimension_semantics=("parallel",), vmem_limit_bytes=VMEM_LIMIT),
        name="cumgate",
    )(fgt, bf_pad, kf3, place)


def _compress_one(x_ref, pea_ref, peb_ref, wa_ref, wb_ref, b1_ref, w2_ref, b2_ref):
    n = x_ref.shape[1] // CMP_STRIDE
    xk = jnp.concatenate([x_ref[0, pl.ds(l, n, stride=CMP_STRIDE), :].astype(BF16) for l in range(CMP_STRIDE)],
                         axis=1)
    wa = wa_ref[...]
    wb = wb_ref[...]
    first = _dot(xk, wa)
    second = _dot(xk, wb)
    pe_term = _dot(pea_ref[...].astype(BF16), wa) + _dot(peb_ref[...].astype(BF16), wb)
    hidden = first + pltpu.roll(second, n - 1, 0) + pe_term[0:1, :] + b1_ref[...]
    act = jax.nn.gelu(hidden)
    return _dot(act.astype(BF16), w2_ref[...]) + b2_ref[...]


def _compress_kernel(kc_ref, vc_ref, cos_ref, sin_ref,
                     kpea, kpeb, kwa, kwb, kb1, kw2, kb2,
                     vpea, vpeb, vwa, vwb, vb1, vw2, vb2,
                     kcmp_ref, vcmpt_ref):
    kcmp = _compress_one(kc_ref, kpea, kpeb, kwa, kwb, kb1, kw2, kb2)
    kcmp_ref[0] = _rope(kcmp, cos_ref[...], sin_ref[...]).astype(BF16)
    vcmpt_ref[0] = _compress_one(vc_ref, vpea, vpeb, vwa, vwb, vb1, vw2, vb2).T.astype(BF16)


def _compress(kc3, vc3, cos_c, sin_c, kparams, vparams):
    b, seq, w = kc3.shape
    n = seq // CMP_STRIDE
    x_spec = pl.BlockSpec((1, seq, w), lambda i: (i, 0, 0))
    full = lambda a: pl.BlockSpec(a.shape, lambda i: (0,) * a.ndim)
    params = list(kparams) + list(vparams)
    return pl.pallas_call(
        _compress_kernel,
        grid=(b,),
        in_specs=[x_spec, x_spec, full(cos_c), full(sin_c)] + [full(p) for p in params],
        out_specs=[pl.BlockSpec((1, n, KV_WIDTH), lambda i: (i, 0, 0)),
                   pl.BlockSpec((1, KV_WIDTH, n), lambda i: (i, 0, 0))],
        out_shape=[jax.ShapeDtypeStruct((b, n, KV_WIDTH), BF16), jax.ShapeDtypeStruct((b, KV_WIDTH, n), BF16)],
        compiler_params=pltpu.CompilerParams(dimension_semantics=("parallel",), vmem_limit_bytes=VMEM_LIMIT),
        name="compress",
    )(kc3, vc3, cos_c, sin_c, *params)


ACC_ROWS = HEAD_DIM + BF16_ROWS


def _flash_step_t(s, vt, m_ref, acc_ref, idx):
    row = slice(idx, idx + 1)
    m_old = m_ref[row, :]
    m_new = jnp.maximum(m_old, jnp.max(s, axis=0, keepdims=True))
    alpha = jnp.exp2(m_old - m_new)
    p = jnp.exp2((s - m_new).astype(BF16))
    m_ref[row, :] = m_new
    vt_ones = jnp.concatenate([vt, jnp.ones((BF16_ROWS, vt.shape[1]), BF16)], axis=0)
    acc_ref[idx] = alpha * acc_ref[idx] + _dot(vt_ones, p)


def _flash_reset(m_ref, acc_ref):
    m_ref[...] = jnp.full(m_ref.shape, NEG, F32)
    acc_ref[...] = jnp.zeros(acc_ref.shape, F32)


def _flash_finish(acc_ref, idx):
    acc = acc_ref[idx]
    return acc[0:HEAD_DIM] * (1.0 / acc[HEAD_DIM:HEAD_DIM + 1])


def _for_tiles(n, unroll, visit):
    groups = n // unroll

    def body(jj, carry):
        visit(jj * unroll, unroll)
        return carry

    lax.fori_loop(0, groups, body, 0)
    done = groups * unroll
    rem = n - done
    p = unroll // 2
    while p >= 1:
        first = done + (rem & ~(2 * p - 1))

        @pl.when((rem & p) != 0)
        def _(first=first, p=p):
            visit(first, p)

        p //= 2


FOX_TILE = 256
FOX_KEYS = 256
FOX_AHEAD = 5
FOX_UNROLL = 4


def _fox_kernel(qt_ref, kaug_ref, vt_ref, crow_ref, o_ref, qaug_ref, m_ref, acc_ref):
    t = FOX_TILE
    tk = FOX_KEYS
    assert t == tk
    i = pl.program_id(1)
    kr = lax.broadcasted_iota(jnp.int32, (tk, t), 0)
    qc = lax.broadcasted_iota(jnp.int32, (tk, t), 1)
    frow = lax.broadcasted_iota(jnp.int32, (BF16_ROWS, t), 0)
    pad = jnp.zeros((HALF - BF16_ROWS, t), BF16)

    _flash_reset(m_ref, acc_ref)
    for h in range(N_HEADS):
        c1, c2, c3 = _split3(crow_ref[0, h:h + 1, :])
        feat = jnp.where(frow == 0, c1, jnp.where(frow == 1, c2, jnp.where(frow == 2, c3,
               jnp.where(frow < 2 * N_BIAS, 1.0, 0.0)))).astype(BF16)
        qh = qt_ref[h * HEAD_DIM:(h + 1) * HEAD_DIM, :]
        parts = [qh, feat, pad] if _bias_lane_base(h) == HALF else [feat, pad, qh]
        qaug_ref[h] = jnp.concatenate(parts, axis=0)

    def step(tiles):
        chains = [(off, mask, h) for off, mask in tiles for h in range(N_HEADS)]

        def scores(c):
            off, _, h = chains[c]
            return _dot(kaug_ref[0, h, pl.ds(off, tk), :], qaug_ref[h])

        pending = {c: scores(c) for c in range(FOX_AHEAD)}
        for c, (off, mask, h) in enumerate(chains):
            if c + FOX_AHEAD < len(chains):
                pending[c + FOX_AHEAD] = scores(c + FOX_AHEAD)
            s = pending.pop(c)
            if mask is not None:
                s = jnp.where(mask, s, MASK_FILL)
            _flash_step_t(s, vt_ref[h * HEAD_DIM:(h + 1) * HEAD_DIM, pl.ds(off, tk)], m_ref, acc_ref, h)

    def tile_off(j):
        return pl.multiple_of(j * tk, tk)

    groups = i // FOX_UNROLL

    def body(jj, carry):
        step([(tile_off(jj * FOX_UNROLL + u), None) for u in range(FOX_UNROLL)])
        return carry

    lax.fori_loop(0, groups, body, 0)
    done = groups * FOX_UNROLL
    for left in range(FOX_UNROLL):
        @pl.when(i - done == left)
        def _(left=left):
            step([(tile_off(done + u), None) for u in range(left)] + [(tile_off(i), kr <= qc)])

    for col in range(WIDTH // LANES):
        pair = [_flash_finish(acc_ref, h) for h in (2 * col, 2 * col + 1)]
        o_ref[0, :, col * LANES:(col + 1) * LANES] = jnp.concatenate(pair, axis=0).T.astype(BF16)


def _fox(qft, kaug, vft, crow):
    b, _, seq, _ = kaug.shape
    t = FOX_TILE
    nq = seq // t
    return pl.pallas_call(
        _fox_kernel,
        grid=(b, nq),
        in_specs=[pl.BlockSpec((WIDTH, t), lambda bi, i: (0, bi * nq + i)),
                  pl.BlockSpec((1, N_HEADS, seq, LANES), lambda bi, i: (bi, 0, 0, 0)),
                  pl.BlockSpec((WIDTH, seq), lambda bi, i: (0, bi)),
                  pl.BlockSpec((1, N_HEADS, t), lambda bi, i: (bi, 0, i))],
        out_specs=pl.BlockSpec((1, t, WIDTH), lambda bi, i: (bi, i, 0)),
        out_shape=jax.ShapeDtypeStruct((b, seq, WIDTH), BF16),
        scratch_shapes=[pltpu.VMEM((N_HEADS, LANES, t), BF16), pltpu.VMEM((N_HEADS, t), F32),
                        pltpu.VMEM((N_HEADS, ACC_ROWS, t), F32)],
        compiler_params=pltpu.CompilerParams(dimension_semantics=("parallel", "arbitrary"),
                                             vmem_limit_bytes=VMEM_LIMIT),
        name="fox",
    )(qft, kaug, vft, crow)


NSA_TILE = 256
NSA_KEYS = 128
STAT_ROWS = 8
NSA_UNROLL = 8
NSA_AHEAD = 4


def _nsa_kernel(qt_ref, kcmp_ref, vcmpt_ref, ks_ref, vst_ref, kw_ref, vwt_ref, fgt_ref, bg_ref, ovl_ref, o_ref,
                qaug_ref, m_ref, acc_ref, ocmp_ref, *, topk):
    t = NSA_TILE
    tk = NSA_KEYS
    per_q = t // tk
    wide = GROUP * t
    n_cmp = kcmp_ref.shape[1]
    i = pl.program_id(1)
    start = i * t
    lane = lax.broadcasted_iota(jnp.int32, (1, LANES), 1)
    mine = [lane < HALF, lane >= HALF]
    zero_bf = jnp.zeros((), BF16)

    q4 = jnp.concatenate([qt_ref[n * LANES:(n + 1) * LANES, :] for n in range(GROUP)], axis=1)
    qpos4 = start + (lax.broadcasted_iota(jnp.int32, (1, wide), 1) & (t - 1))
    cur = (start + lax.broadcasted_iota(jnp.int32, (1, t), 1)) >> SEL_SHIFT
    sub8 = lax.broadcasted_iota(jnp.int32, (STAT_ROWS, t), 0)

    def cmp_rank(n_rows, n_slots):
        cmp_end = lax.broadcasted_iota(jnp.int32, (n_rows, 1), 0) * CMP_STRIDE + (CMP_LEN - 1)
        vis = cmp_end <= qpos4
        slot = lax.broadcasted_iota(jnp.int32, (n_slots, t), 0)
        valid = slot <= cur
        forced = (slot == 0) | (slot == cur) | (slot == cur - 1)
        for g in range(N_KV):
            s = jnp.where(vis, _dot(jnp.where(mine[g], kcmp_ref[0, 0:n_rows, :], zero_bf), q4), NEG)
            e = jnp.where(vis, jnp.exp2(s - jnp.max(s, axis=0, keepdims=True)), 0.0)
            denom = jnp.sum(e, axis=0, keepdims=True)
            p = e * jnp.where(denom > 0.0, 1.0 / denom, 0.0)
            ocmp_ref[g] = _dot(vcmpt_ref[0, g * HEAD_DIM:(g + 1) * HEAD_DIM, 0:n_rows], p.astype(BF16))
            psum = p[:, 0:t]
            for n in range(1, GROUP):
                psum = psum + p[:, n * t:(n + 1) * t]
            ovl = ovl_ref[0:n_slots, 0:n_rows]
            imp = sum(_dot(ovl, part.astype(BF16)) for part in _split3(psum))
            score = jnp.where(valid, imp + jnp.where(forced, FORCE_BONUS, 0.0), -1.0)
            tiles = [score[r:r + STAT_ROWS, :] for r in range(0, n_slots, STAT_ROWS)]
            ranks = [jnp.zeros((STAT_ROWS, t), F32) for _ in tiles]
            for jp in range(n_slots):
                other = jnp.broadcast_to(score[jp:jp + 1, :], (STAT_ROWS, t))
                for k, tile in enumerate(tiles):
                    first = k * STAT_ROWS
                    ge = jnp.where(other >= tile, 1.0, 0.0)
                    gt = jnp.where(other > tile, 1.0, 0.0)
                    if first > jp:
                        ahead = ge
                    elif first + STAT_ROWS - 1 <= jp:
                        ahead = gt
                    else:
                        ahead = jnp.where(sub8 + first > jp, ge, gt)
                    ranks[k] = ranks[k] + ahead
            rank = jnp.concatenate(ranks, axis=0)
            selb = jnp.where(rank < topk, 0.0, MASK_FILL)
            if n_slots < SEL_SLOTS:
                selb = jnp.concatenate([selb, jnp.full((SEL_SLOTS - n_slots, t), MASK_FILL, F32)], axis=0)
            selb4 = jnp.concatenate([selb.astype(BF16)] * GROUP, axis=1)
            qaug_ref[g] = jnp.concatenate([q4[0:HALF], selb4] if g == 0 else [selb4, q4[HALF:LANES]], axis=0)

    slots_per_tile = t // SEL_LEN
    lo = 0
    for n_slots in range(SEL_SLOTS // 8, SEL_SLOTS + 1, SEL_SLOTS // 8):
        hi = n_slots // slots_per_tile
        n_rows = min(n_cmp, -(-(hi * t // CMP_STRIDE) // LANES) * LANES)
        last = n_slots == SEL_SLOTS

        @pl.when((i >= lo) if last else ((i >= lo) & (i < hi)))
        def _(n_rows=n_rows, n_slots=n_slots):
            cmp_rank(n_rows, n_slots)

        lo = hi
    o_cmp = [ocmp_ref[g] for g in range(N_KV)]

    kr = lax.broadcasted_iota(jnp.int32, (tk, wide), 0)
    q_local = lax.broadcasted_iota(jnp.int32, (tk, wide), 1) & (t - 1)
    key_row = lax.broadcasted_iota(jnp.int32, (tk, LANES), 0)
    lane_full = lax.broadcasted_iota(jnp.int32, (tk, LANES), 1)
    slot_minus_row = [lane_full - HALF - (key_row >> SEL_SHIFT), lane_full - (key_row >> SEL_SHIFT)]

    def sel_score(off, g):
        onehot = jnp.where(slot_minus_row[g] == (off >> SEL_SHIFT), 1.0, 0.0).astype(BF16)
        return _dot(jnp.where(mine[g], ks_ref[0, pl.ds(off, tk), :], onehot), qaug_ref[g])

    def win_score(off, g):
        return _dot(jnp.where(mine[g], kw_ref[0, pl.ds(off, tk), :], zero_bf), q4)

    def sel_chains(tiles):
        return [(sel_score, vst_ref, g, g, off, mask) for off, mask in tiles for g in range(N_KV)]

    def win_chains(tiles):
        return [(win_score, vwt_ref, g, N_KV + g, off, mask) for off, mask in tiles for g in range(N_KV)]

    def sweep(chains):
        def issue(c):
            score, _, g, _, off, _ = chains[c]
            return score(off, g)

        pending = {c: issue(c) for c in range(min(NSA_AHEAD, len(chains)))}
        for c, (_, vt_ref_, g, slot_id, off, mask) in enumerate(chains):
            if c + NSA_AHEAD < len(chains):
                pending[c + NSA_AHEAD] = issue(c + NSA_AHEAD)
            s = pending.pop(c)
            if mask is not None:
                s = jnp.where(mask, s, MASK_FILL)
            _flash_step_t(s, vt_ref_[g * HEAD_DIM:(g + 1) * HEAD_DIM, pl.ds(off, tk)], m_ref, acc_ref, slot_id)

    def tile_off(j):
        return pl.multiple_of(j * tk, tk)

    n_before = i * per_q
    n_back = WINDOW // tk
    groups = jnp.where(n_before >= n_back, n_before // NSA_UNROLL, 0)
    _flash_reset(m_ref, acc_ref)

    def sel_body(jj, carry):
        sweep(sel_chains([(tile_off(jj * NSA_UNROLL + u), None) for u in range(NSA_UNROLL)]))
        return carry

    lax.fori_loop(0, groups, sel_body, 0)
    done = groups * NSA_UNROLL

    own = [(tile_off(n_before + u), kr + u * tk <= q_local) for u in range(per_q)]

    for left in range(0, NSA_UNROLL, per_q):
        @pl.when((n_before >= n_back) & (n_before - done == left))
        def _(left=left):
            back = [(tile_off(n_before - n_back + u), (q_local - kr < u * tk) if u < per_q else None)
                    for u in range(n_back)]
            sweep(sel_chains([(tile_off(done + u), None) for u in range(left)] + own) + win_chains(back + own))

    @pl.when(n_before < n_back)
    def _():
        def body(j, carry):
            sweep(sel_chains([(tile_off(j), None)]) + win_chains([(tile_off(j), None)]))
            return carry

        lax.fori_loop(0, n_before, body, 0)
        sweep(sel_chains(own) + win_chains(own))

    o_slc = [_flash_finish(acc_ref, g) for g in range(N_KV)]
    o_win = [_flash_finish(acc_ref, N_KV + g) for g in range(N_KV)]

    gates = jax.nn.sigmoid(fgt_ref[...] + bg_ref[...])
    for n in range(GROUP):
        cols = slice(n * t, (n + 1) * t)
        mixed = []
        for g in range(N_KV):
            base = N_HEADS + (g * GROUP + n) * 3
            mixed.append(gates[base:base + 1, :] * o_cmp[g][:, cols]
                         + gates[base + 1:base + 2, :] * o_slc[g][:, cols]
                         + gates[base + 2:base + 3, :] * o_win[g][:, cols])
        o_ref[0, :, n * LANES:(n + 1) * LANES] = jnp.concatenate(mixed, axis=0).T.astype(BF16)


def _nsa(qnt, kcmp, vcmpt, ks, vst, kw, vwt, fgt, bg_t, ovl_t, topk):
    b, seq, _ = ks.shape
    t = NSA_TILE
    nq = seq // t
    n_cmp = kcmp.shape[1]
    whole = lambda bi, i: (bi, 0, 0)
    whole_t = lambda bi, i: (0, bi)
    tile_t = lambda bi, i: (0, bi * nq + i)
    fixed = lambda bi, i: (0, 0)
    return pl.pallas_call(
        functools.partial(_nsa_kernel, topk=topk),
        grid=(b, nq),
        in_specs=[pl.BlockSpec((WIDTH, t), tile_t),
                  pl.BlockSpec((1, n_cmp, KV_WIDTH), whole), pl.BlockSpec((1, KV_WIDTH, n_cmp), whole),
                  pl.BlockSpec((1, seq, KV_WIDTH), whole), pl.BlockSpec((KV_WIDTH, seq), whole_t),
                  pl.BlockSpec((1, seq, KV_WIDTH), whole), pl.BlockSpec((KV_WIDTH, seq), whole_t),
                  pl.BlockSpec((LANES, t), tile_t), pl.BlockSpec((LANES, t), fixed),
                  pl.BlockSpec(ovl_t.shape, fixed)],
        out_specs=pl.BlockSpec((1, t, WIDTH), lambda bi, i: (bi, i, 0)),
        out_shape=jax.ShapeDtypeStruct((b, seq, WIDTH), BF16),
        scratch_shapes=[pltpu.VMEM((N_KV, LANES, GROUP * t), BF16), pltpu.VMEM((STAT_ROWS, GROUP * t), F32),
                        pltpu.VMEM((2 * N_KV, ACC_ROWS, GROUP * t), F32),
                        pltpu.VMEM((N_KV, HEAD_DIM, GROUP * t), F32)],
        compiler_params=pltpu.CompilerParams(dimension_semantics=("parallel", "arbitrary"),
                                             vmem_limit_bytes=VMEM_LIMIT),
        name="nsa",
    )(qnt, kcmp, vcmpt, ks, vst, kw, vwt, fgt, bg_t, ovl_t)


def _post_kernel(x_ref, of_ref, on_ref, gf_ref, gn_ref, wo_ref, gm_ref, wu_ref, wd_ref, gl_ref, o_ref):
    yf = _rms(of_ref[...].astype(F32), gf_ref[...]).astype(BF16)
    yn = _rms(on_ref[...].astype(F32), gn_ref[...]).astype(BF16)
    h1 = x_ref[...] + _dot(yf, wo_ref[0:WIDTH, :]) + _dot(yn, wo_ref[WIDTH:2 * WIDTH, :])
    u = _dot(_rms(h1, gm_ref[...]).astype(BF16), wu_ref[...])
    act = jnp.square(jnp.maximum(u, 0.0)).astype(BF16)
    h2 = h1 + _dot(act, wd_ref[...])
    o_ref[...] = _rms(h2, gl_ref[...])


def _post(x2, ofox, onsa, g_fox, g_nsa, w_out, g_mlp, w_up, w_down, g_final, tm):
    m, d = x2.shape
    row = lambda i: (i, 0)
    fixed = lambda i: (0, 0)
    full = lambda a: pl.BlockSpec(a.shape, fixed, pipeline_mode=pl.Buffered(1))
    return pl.pallas_call(
        _post_kernel,
        grid=(m // tm,),
        in_specs=[pl.BlockSpec((tm, d), row), pl.BlockSpec((tm, WIDTH), row), pl.BlockSpec((tm, WIDTH), row),
                  full(g_fox), full(g_nsa), full(w_out), full(g_mlp), full(w_up), full(w_down), full(g_final)],
        out_specs=pl.BlockSpec((tm, d), row),
        out_shape=jax.ShapeDtypeStruct((m, d), F32),
        compiler_params=pltpu.CompilerParams(dimension_semantics=("parallel",), vmem_limit_bytes=VMEM_LIMIT),
        name="post",
    )(x2, ofox, onsa, g_fox, g_nsa, w_out, g_mlp, w_up, w_down, g_final)


_NSA_PERM = np.array([(p % 2) * GROUP + p // 2 for p in range(N_HEADS)])


def _perm_heads(a, axis):
    shape = a.shape
    a = a.reshape(shape[:axis] + (N_HEADS, HEAD_DIM) + shape[axis + 1:])
    a = jnp.take(a, _NSA_PERM, axis=axis)
    return a.reshape(shape)


def _prep_w_in(w):
    qscale = HEAD_DIM ** -0.5 * LOG2E
    sizes = [WIDTH, WIDTH, WIDTH, N_HEADS, WIDTH] + [KV_WIDTH] * 6 + [3 * N_HEADS]
    offs = np.cumsum([0] + sizes)
    qf, kf, vf, fl, qn, kc, vc, ks, vs, kw, vw, gate = [w[:, offs[k]:offs[k + 1]] for k in range(len(sizes))]
    fg = jnp.concatenate([fl, gate, jnp.zeros((w.shape[0], LANES - 4 * N_HEADS), w.dtype)], axis=1)
    w_nat = jnp.concatenate([kf, kc, vc, ks, kw], axis=1).astype(BF16)
    w_tr = jnp.concatenate([qf * qscale, vf, _perm_heads(qn * qscale, 1), vs, vw, fg], axis=1).T.astype(BF16)
    assert w_nat.shape[1] == NAT_COLS and w_tr.shape[0] == TR_ROWS
    return w_nat, w_tr


def _prep_compress(pe, w1, b1, w2, b2):
    half = CMP_LEN // 2
    eye = jnp.eye(N_KV, dtype=F32)

    def expand_w1(wpart):
        w3 = wpart.reshape(half, HEAD_DIM, CMP_HIDDEN)
        return jnp.einsum('ldh,gk->lgdkh', w3, eye).reshape(half * KV_WIDTH, N_KV * CMP_HIDDEN).astype(BF16)

    def expand_pe(ppart):
        flat = jnp.broadcast_to(ppart[:, None, :], (half, N_KV, HEAD_DIM)).reshape(1, half * KV_WIDTH)
        return jnp.broadcast_to(flat, (8, half * KV_WIDTH))

    w2b = jnp.einsum('hd,gk->ghkd', w2, eye).reshape(N_KV * CMP_HIDDEN, KV_WIDTH).astype(BF16)
    return (expand_pe(pe[:half]), expand_pe(pe[half:]),
            expand_w1(w1[:half * HEAD_DIM]), expand_w1(w1[half * HEAD_DIM:]),
            jnp.tile(b1, N_KV)[None, :], w2b, jnp.tile(b2, N_KV)[None, :])


def _rope_tables(pos, reps):
    half = HEAD_DIM // 2
    inv = ROPE_THETA ** (-np.arange(half, dtype=np.float64) / half)
    ang = np.asarray(pos, np.float64)[:, None] * inv[None, :]
    cos = np.cos(ang)
    sin = np.sin(ang)
    return (np.tile(np.concatenate([cos, cos], axis=1), (1, reps)).astype(np.float32),
            np.tile(np.concatenate([-sin, sin], axis=1), (1, reps)).astype(np.float32))


def _overlap_t(n_cmp_slots, n_cmp, n_blocks):
    c = np.arange(n_cmp_slots)[None, :] * CMP_STRIDE
    j = np.arange(SEL_SLOTS)[:, None]
    s = j * SEL_LEN
    ovl = (c < s + SEL_LEN) & (c + CMP_LEN > s) & (np.arange(n_cmp_slots)[None, :] < n_cmp) & (j < n_blocks)
    return jnp.asarray(ovl.astype(np.float32), BF16)


def _row_tile(m, want):
    t = want
    while m % t:
        t //= 2
    return t


def _layer(h, seq, g_attn, w_in, b_f, b_gate, cmpk, cmpv, g_fox, g_nsa, w_out, g_mlp, w_up, w_down, g_out):
    m, d = h.shape
    b = m // seq
    n_chunks = seq // CMP_STRIDE
    n_cmp = (seq - CMP_LEN) // CMP_STRIDE + 1
    n_blocks = seq // SEL_LEN
    assert seq % FOX_TILE == 0 and seq % NSA_TILE == 0 and n_chunks % LANES == 0
    assert SEL_TOPK <= n_blocks <= SEL_SLOTS
    assert n_cmp == n_chunks - 1

    cos_n, sin_n = _rope_tables(np.arange(seq), N_KV)
    cos_c, sin_c = _rope_tables(np.arange(n_chunks) * CMP_STRIDE + CMP_LEN - 1, N_KV)
    cos_t = np.ascontiguousarray(cos_n.T)
    sin_t = np.ascontiguousarray(sin_n.T)

    tm = _row_tile(seq, 512)
    w_nat, w_tr = _prep_w_in(w_in)
    kf, kc, vc, ks, kw, qft, vft, qnt, vst, vwt, fgt = _inproj(
        h, g_attn[None, :], w_nat, w_tr, cos_n, sin_n, cos_t, sin_t, seq, tm)
    r3 = lambda a: a.reshape(b, seq, a.shape[-1])

    bf_pad = jnp.zeros((1, LANES), F32).at[0, :N_HEADS].set(b_f)
    bg_col = jnp.zeros((LANES,), F32).at[N_HEADS:4 * N_HEADS].set(b_gate)
    bg_t = jnp.broadcast_to(bg_col[:, None], (LANES, NSA_TILE))
    crow, kaug = _cumgate(fgt, bf_pad, r3(kf))

    kcmp, vcmpt = _compress(r3(kc), r3(vc), cos_c, sin_c, _prep_compress(*cmpk), _prep_compress(*cmpv))

    ofox = _fox(qft, kaug, vft, crow)
    onsa = _nsa(qnt, kcmp, vcmpt, r3(ks), vst, r3(kw), vwt, fgt, bg_t,
                _overlap_t(n_chunks, n_cmp, n_blocks), min(SEL_TOPK, n_blocks))

    w_out_p = jnp.concatenate([w_out[:WIDTH], _perm_heads(w_out[WIDTH:], 0)], axis=0).astype(BF16)
    return _post(h, ofox.reshape(m, WIDTH), onsa.reshape(m, WIDTH), g_fox[None, :], _perm_heads(g_nsa, 0)[None, :],
                 w_out_p, g_mlp[None, :], w_up.astype(BF16), w_down.astype(BF16), g_out[None, :],
                 _row_tile(seq, 512))


def kernel(x, g_attn, w_in, b_f, b_gate, cmpk_pe, cmpk_w1, cmpk_b1, cmpk_w2, cmpk_b2, cmpv_pe, cmpv_w1, cmpv_b1,
           cmpv_w2, cmpv_b2, g_fox, g_nsa, w_out, g_mlp, w_up, w_down, g_final):
    b, seq, d = x.shape
    depth = g_attn.shape[0]
    assert depth == 1, "the final rmsnorm is fused into the (single) layer's last kernel"
    h = x.reshape(b * seq, d)
    out = _layer(h, seq, g_attn[0], w_in[0], b_f[0], b_gate[0],
                 (cmpk_pe[0], cmpk_w1[0], cmpk_b1[0], cmpk_w2[0], cmpk_b2[0]),
                 (cmpv_pe[0], cmpv_w1[0], cmpv_b1[0], cmpv_w2[0], cmpv_b2[0]),
                 g_fox[0], g_nsa[0], w_out[0], g_mlp[0], w_up[0], w_down[0], g_final)
    return out.reshape(b, seq, d)
```

```python
import functools

import numpy as np
import jax
import jax.numpy as jnp
from jax import lax
from jax.experimental import pallas as pl
from jax.experimental.pallas import tpu as pltpu

F32 = jnp.float32
BF16 = jnp.bfloat16

HEAD_DIM = 64
N_HEADS = 8
N_KV = 2
GROUP = N_HEADS // N_KV
WIDTH = N_HEADS * HEAD_DIM
KV_WIDTH = N_KV * HEAD_DIM
CMP_LEN = 32
CMP_STRIDE = 16
CMP_HIDDEN = 256
SEL_LEN = 64
SEL_SHIFT = 6
SEL_TOPK = 16
SEL_SLOTS = 64
WINDOW = 512
ROPE_THETA = 10000.0
EPS = 1e-6
NEG = -1e30
MASK_FILL = -(2.0 ** 100)
FORCE_BONUS = 1e4
LOG2E = 1.4426950408889634
LANES = 128
HALF = LANES // 2
BF16_ROWS = 16

VMEM_LIMIT = 56 * 1024 * 1024


def _dot(a, b, precision=None):
    return jnp.dot(a, b, preferred_element_type=F32, precision=precision)


def _dot_nt(a, b, precision=None):
    return lax.dot_general(a, b, (((1,), (1,)), ((), ())), preferred_element_type=F32, precision=precision)


def _rms(x, g):
    return x * lax.rsqrt(jnp.mean(x * x, axis=-1, keepdims=True) + EPS) * g


def _rope(x, cos, sin_signed):
    w = x.shape[-1]
    lane = lax.broadcasted_iota(jnp.int32, (1, w), 1)
    first = (lane % HEAD_DIM) < (HEAD_DIM // 2)
    partner = jnp.where(first, pltpu.roll(x, w - HEAD_DIM // 2, 1), pltpu.roll(x, HEAD_DIM // 2, 1))
    return x * cos + partner * sin_signed


def _rope_t(x, cos, sin_signed):
    r = x.shape[0]
    row = lax.broadcasted_iota(jnp.int32, (r, 1), 0)
    first = (row % HEAD_DIM) < (HEAD_DIM // 2)
    partner = jnp.where(first, pltpu.roll(x, r - HEAD_DIM // 2, 0), pltpu.roll(x, HEAD_DIM // 2, 0))
    return x * cos + partner * sin_signed


def _split3(c):
    c1 = c.astype(BF16).astype(F32)
    r1 = c - c1
    c2 = r1.astype(BF16).astype(F32)
    return c1, c2, r1 - c2


NAT_KF, NAT_KC, NAT_VC, NAT_KS, NAT_KW, NAT_COLS = 0, 512, 640, 768, 896, 1024
TR_QF, TR_VF, TR_QN, TR_VS, TR_VW, TR_FG, TR_ROWS = 0, 512, 1024, 1536, 1664, 1792, 1920


def _inproj_kernel(x_ref, g_ref, wn_ref, wt_ref, cos_ref, sin_ref, cost_ref, sint_ref,
                   kf_ref, kc_ref, vc_ref, ks_ref, kw_ref,
                   qft_ref, vft_ref, qnt_ref, vst_ref, vwt_ref, fgt_ref):
    hb = _rms(x_ref[...], g_ref[...]).astype(BF16)

    def nat(lo, width):
        return _dot(hb, wn_ref[:, lo:lo + width])

    def tr(lo, rows):
        return _dot_nt(wt_ref[lo:lo + rows, :], hb)

    cos = cos_ref[...]
    sin = sin_ref[...]
    kf_ref[...] = nat(NAT_KF, WIDTH).astype(BF16)
    kc_ref[...] = nat(NAT_KC, KV_WIDTH)
    vc_ref[...] = nat(NAT_VC, KV_WIDTH)
    ks_ref[...] = _rope(nat(NAT_KS, KV_WIDTH), cos, sin).astype(BF16)
    kw_ref[...] = _rope(nat(NAT_KW, KV_WIDTH), cos, sin).astype(BF16)
    qft_ref[...] = tr(TR_QF, WIDTH).astype(BF16)
    vft_ref[...] = tr(TR_VF, WIDTH).astype(BF16)
    pairs = WIDTH // KV_WIDTH
    cos_rows = jnp.concatenate([cost_ref[...]] * pairs, axis=0)
    sin_rows = jnp.concatenate([sint_ref[...]] * pairs, axis=0)
    qnt_ref[...] = _rope_t(tr(TR_QN, WIDTH), cos_rows, sin_rows).astype(BF16)
    vst_ref[...] = tr(TR_VS, KV_WIDTH).astype(BF16)
    vwt_ref[...] = tr(TR_VW, KV_WIDTH).astype(BF16)
    fgt_ref[...] = tr(TR_FG, LANES)


def _inproj(x2, g_attn, w_nat, w_tr, cos_n, sin_n, cos_t, sin_t, seq, tm):
    m, d = x2.shape
    sblocks = seq // tm
    row = lambda i: (i, 0)
    col = lambda i: (0, i)
    fixed = lambda i: (0, 0)
    tab = lambda i: (i % sblocks, 0)
    tab_t = lambda i: (0, i % sblocks)
    sds = jax.ShapeDtypeStruct
    out_shape = [sds((m, WIDTH), BF16)] + [sds((m, KV_WIDTH), F32)] * 2 + [sds((m, KV_WIDTH), BF16)] * 2 + [
                 sds((WIDTH, m), BF16), sds((WIDTH, m), BF16), sds((WIDTH, m), BF16),
                 sds((KV_WIDTH, m), BF16), sds((KV_WIDTH, m), BF16), sds((LANES, m), F32)]
    out_specs = ([pl.BlockSpec((tm, WIDTH), row)] + [pl.BlockSpec((tm, KV_WIDTH), row)] * 4
                 + [pl.BlockSpec((WIDTH, tm), col)] * 3 + [pl.BlockSpec((KV_WIDTH, tm), col)] * 2
                 + [pl.BlockSpec((LANES, tm), col)])
    return pl.pallas_call(
        _inproj_kernel,
        grid=(m // tm,),
        in_specs=[pl.BlockSpec((tm, d), row), pl.BlockSpec((1, d), fixed),
                  pl.BlockSpec(w_nat.shape, fixed), pl.BlockSpec(w_tr.shape, fixed),
                  pl.BlockSpec((tm, KV_WIDTH), tab), pl.BlockSpec((tm, KV_WIDTH), tab),
                  pl.BlockSpec((KV_WIDTH, tm), tab_t), pl.BlockSpec((KV_WIDTH, tm), tab_t)],
        out_specs=out_specs,
        out_shape=out_shape,
        compiler_params=pltpu.CompilerParams(dimension_semantics=("parallel",), vmem_limit_bytes=VMEM_LIMIT),
        name="inproj",
    )(x2, g_attn, w_nat, w_tr, cos_n, sin_n, cos_t, sin_t)


CUM_BLOCK = 256
CUM_UNROLL = 4
N_BIAS = 3


def _bias_lane_base(h):
    return HALF if h % 2 == 0 else 0


def _bias_placement():
    p = np.zeros((LANES, N_HEADS * LANES), np.float32)
    for h in range(N_HEADS):
        base = h * LANES + _bias_lane_base(h)
        for part in range(N_BIAS):
            p[part * N_HEADS + h, base + N_BIAS + part] = -1.0
            p[N_BIAS * N_HEADS, base + part] = 1.0
    return jnp.asarray(p, BF16)


def _cumgate_kernel(fgt_ref, bf_ref, kf_ref, place_ref, crow_ref, kaug_ref):
    seq = fgt_ref.shape[1]
    r = lax.broadcasted_iota(jnp.int32, (CUM_BLOCK, CUM_BLOCK), 0)
    c = lax.broadcasted_iota(jnp.int32, (CUM_BLOCK, CUM_BLOCK), 1)
    tri = jnp.where(r >= c, 1.0, 0.0).astype(BF16)
    er = lax.broadcasted_iota(jnp.int32, (N_HEADS, LANES), 0)
    ec = lax.broadcasted_iota(jnp.int32, (N_HEADS, LANES), 1)
    pick = jnp.where(er == ec, 1.0, 0.0).astype(BF16)
    bias = bf_ref[...]
    lane = lax.broadcasted_iota(jnp.int32, (CUM_BLOCK, LANES), 1)

    def body(blk, carry):
        off = pl.multiple_of(blk * CUM_BLOCK, CUM_BLOCK)
        z = fgt_ref[:, pl.ds(off, CUM_BLOCK)].T + bias
        logf = jnp.minimum(z, 0.0) - jnp.log(1.0 + jnp.exp(-jnp.abs(z)))
        cs = sum(_dot(tri, part.astype(BF16)) for part in _split3(logf)) + carry
        c2 = cs * LOG2E
        c1, cb, cc = _split3(jnp.where(lane < N_HEADS, c2, 0.0))
        crow_ref[0, :, pl.ds(off, CUM_BLOCK)] = sum(_dot_nt(pick, part.astype(BF16)) for part in (c1, cb, cc))
        parts = (c1 + pltpu.roll(cb, N_HEADS, 1) + pltpu.roll(cc, 2 * N_HEADS, 1)
                 + jnp.where(lane == N_BIAS * N_HEADS, 1.0, 0.0))
        feat = _dot(parts.astype(BF16), place_ref[...]).astype(BF16)
        for h in range(N_HEADS):
            mine = (lane < HALF) if h % 2 == 0 else (lane >= HALF)
            kcol = kf_ref[0, pl.ds(off, CUM_BLOCK), (h // 2) * LANES:(h // 2 + 1) * LANES]
            kaug_ref[0, h, pl.ds(off, CUM_BLOCK), :] = jnp.where(mine, kcol, feat[:, h * LANES:(h + 1) * LANES])
        return cs[CUM_BLOCK - 1:CUM_BLOCK, :]

    lax.fori_loop(0, seq // CUM_BLOCK, body, jnp.zeros((1, LANES), F32), unroll=CUM_UNROLL)


def _cumgate(fgt, bf_pad, kf3):
    b, seq, _ = kf3.shape
    place = _bias_placement()
    return pl.pallas_call(
        _cumgate_kernel,
        grid=(b,),
        in_specs=[pl.BlockSpec((LANES, seq), lambda i: (0, i)), pl.BlockSpec((1, LANES), lambda i: (0, 0)),
                  pl.BlockSpec((1, seq, WIDTH), lambda i: (i, 0, 0)), pl.BlockSpec(place.shape, lambda i: (0, 0))],
        out_specs=[pl.BlockSpec((1, N_HEADS, seq), lambda i: (i, 0, 0)),
                   pl.BlockSpec((1, N_HEADS, seq, LANES), lambda i: (i, 0, 0, 0))],
        out_shape=[jax.ShapeDtypeStruct((b, N_HEADS, seq), F32),
                   jax.ShapeDtypeStruct((b, N_HEADS, seq, LANES), BF16)],
        compiler_params=pltpu.CompilerParams(dimension_semantics=("parallel",), vmem_limit_bytes=VMEM_LIMIT),
        name="cumgate",
    )(fgt, bf_pad, kf3, place)


def _compress_one(x_ref, pea_ref, peb_ref, wa_ref, wb_ref, b1_ref, w2_ref, b2_ref):
    n = x_ref.shape[1] // CMP_STRIDE
    xk = jnp.concatenate([x_ref[0, pl.ds(l, n, stride=CMP_STRIDE), :].astype(BF16) for l in range(CMP_STRIDE)],
                         axis=1)
    wa = wa_ref[...]
    wb = wb_ref[...]
    first = _dot(xk, wa)
    second = _dot(xk, wb)
    pe_term = _dot(pea_ref[...].astype(BF16), wa) + _dot(peb_ref[...].astype(BF16), wb)
    hidden = first + pltpu.roll(second, n - 1, 0) + pe_term[0:1, :] + b1_ref[...]
    act = jax.nn.gelu(hidden)
    return _dot(act.astype(BF16), w2_ref[...]) + b2_ref[...]


def _compress_kernel(kc_ref, vc_ref, cos_ref, sin_ref,
                     kpea, kpeb, kwa, kwb, kb1, kw2, kb2,
                     vpea, vpeb, vwa, vwb, vb1, vw2, vb2,
                     kcmp_ref, vcmpt_ref):
    kcmp = _compress_one(kc_ref, kpea, kpeb, kwa, kwb, kb1, kw2, kb2)
    kcmp_ref[0] = _rope(kcmp, cos_ref[...], sin_ref[...]).astype(BF16)
    vcmpt_ref[0] = _compress_one(vc_ref, vpea, vpeb, vwa, vwb, vb1, vw2, vb2).T.astype(BF16)


def _compress(kc3, vc3, cos_c, sin_c, kparams, vparams):
    b, seq, w = kc3.shape
    n = seq // CMP_STRIDE
    x_spec = pl.BlockSpec((1, seq, w), lambda i: (i, 0, 0))
    full = lambda a: pl.BlockSpec(a.shape, lambda i: (0,) * a.ndim)
    params = list(kparams) + list(vparams)
    return pl.pallas_call(
        _compress_kernel,
        grid=(b,),
        in_specs=[x_spec, x_spec, full(cos_c), full(sin_c)] + [full(p) for p in params],
        out_specs=[pl.BlockSpec((1, n, KV_WIDTH), lambda i: (i, 0, 0)),
                   pl.BlockSpec((1, KV_WIDTH, n), lambda i: (i, 0, 0))],
        out_shape=[jax.ShapeDtypeStruct((b, n, KV_WIDTH), BF16), jax.ShapeDtypeStruct((b, KV_WIDTH, n), BF16)],
        compiler_params=pltpu.CompilerParams(dimension_semantics=("parallel",), vmem_limit_bytes=VMEM_LIMIT),
        name="compress",
    )(kc3, vc3, cos_c, sin_c, *params)


ACC_ROWS = HEAD_DIM + BF16_ROWS


def _flash_step_t(s, vt, m_ref, acc_ref, idx):
    row = slice(idx, idx + 1)
    m_old = m_ref[row, :]
    m_new = jnp.maximum(m_old, jnp.max(s, axis=0, keepdims=True))
    alpha = jnp.exp2(m_old - m_new)
    p = jnp.exp2((s - m_new).astype(BF16))
    m_ref[row, :] = m_new
    vt_ones = jnp.concatenate([vt, jnp.ones((BF16_ROWS, vt.shape[1]), BF16)], axis=0)
    acc_ref[idx] = alpha * acc_ref[idx] + _dot(vt_ones, p)


def _flash_reset(m_ref, acc_ref):
    m_ref[...] = jnp.full(m_ref.shape, NEG, F32)
    acc_ref[...] = jnp.zeros(acc_ref.shape, F32)


def _flash_finish(acc_ref, idx):
    acc = acc_ref[idx]
    return acc[0:HEAD_DIM] * (1.0 / acc[HEAD_DIM:HEAD_DIM + 1])


def _for_tiles(n, unroll, visit):
    groups = n // unroll

    def body(jj, carry):
        visit(jj * unroll, unroll)
        return carry

    lax.fori_loop(0, groups, body, 0)
    done = groups * unroll
    rem = n - done
    p = unroll // 2
    while p >= 1:
        first = done + (rem & ~(2 * p - 1))

        @pl.when((rem & p) != 0)
        def _(first=first, p=p):
            visit(first, p)

        p //= 2


FOX_TILE = 256
FOX_KEYS = 256
FOX_AHEAD = 5
FOX_UNROLL = 8


def _fox_kernel(qt_ref, kaug_ref, vt_ref, crow_ref, o_ref, qaug_ref, m_ref, acc_ref):
    t = FOX_TILE
    tk = FOX_KEYS
    assert t == tk
    i = pl.program_id(1)
    kr = lax.broadcasted_iota(jnp.int32, (tk, t), 0)
    qc = lax.broadcasted_iota(jnp.int32, (tk, t), 1)
    frow = lax.broadcasted_iota(jnp.int32, (BF16_ROWS, t), 0)
    pad = jnp.zeros((HALF - BF16_ROWS, t), BF16)

    _flash_reset(m_ref, acc_ref)
    for h in range(N_HEADS):
        c1, c2, c3 = _split3(crow_ref[0, h:h + 1, :])
        feat = jnp.where(frow == 0, c1, jnp.where(frow == 1, c2, jnp.where(frow == 2, c3,
               jnp.where(frow < 2 * N_BIAS, 1.0, 0.0)))).astype(BF16)
        qh = qt_ref[h * HEAD_DIM:(h + 1) * HEAD_DIM, :]
        parts = [qh, feat, pad] if _bias_lane_base(h) == HALF else [feat, pad, qh]
        qaug_ref[h] = jnp.concatenate(parts, axis=0)

    def step(tiles):
        chains = [(off, mask, h) for off, mask in tiles for h in range(N_HEADS)]

        def scores(c):
            off, _, h = chains[c]
            return _dot(kaug_ref[0, h, pl.ds(off, tk), :], qaug_ref[h])

        pending = {c: scores(c) for c in range(FOX_AHEAD)}
        for c, (off, mask, h) in enumerate(chains):
            if c + FOX_AHEAD < len(chains):
                pending[c + FOX_AHEAD] = scores(c + FOX_AHEAD)
            s = pending.pop(c)
            if mask is not None:
                s = jnp.where(mask, s, MASK_FILL)
            _flash_step_t(s, vt_ref[h * HEAD_DIM:(h + 1) * HEAD_DIM, pl.ds(off, tk)], m_ref, acc_ref, h)

    def tile_off(j):
        return pl.multiple_of(j * tk, tk)

    groups = i // FOX_UNROLL

    def body(jj, carry):
        step([(tile_off(jj * FOX_UNROLL + u), None) for u in range(FOX_UNROLL)])
        return carry

    lax.fori_loop(0, groups, body, 0)
    done = groups * FOX_UNROLL
    for left in range(FOX_UNROLL):
        @pl.when(i - done == left)
        def _(left=left):
            step([(tile_off(done + u), None) for u in range(left)] + [(tile_off(i), kr <= qc)])

    for col in range(WIDTH // LANES):
        pair = [_flash_finish(acc_ref, h) for h in (2 * col, 2 * col + 1)]
        o_ref[0, :, col * LANES:(col + 1) * LANES] = jnp.concatenate(pair, axis=0).T.astype(BF16)


def _fox(qft, kaug, vft, crow):
    b, _, seq, _ = kaug.shape
    t = FOX_TILE
    nq = seq // t
    return pl.pallas_call(
        _fox_kernel,
        grid=(b, nq),
        in_specs=[pl.BlockSpec((WIDTH, t), lambda bi, i: (0, bi * nq + i)),
                  pl.BlockSpec((1, N_HEADS, seq, LANES), lambda bi, i: (bi, 0, 0, 0)),
                  pl.BlockSpec((WIDTH, seq), lambda bi, i: (0, bi)),
                  pl.BlockSpec((1, N_HEADS, t), lambda bi, i: (bi, 0, i))],
        out_specs=pl.BlockSpec((1, t, WIDTH), lambda bi, i: (bi, i, 0)),
        out_shape=jax.ShapeDtypeStruct((b, seq, WIDTH), BF16),
        scratch_shapes=[pltpu.VMEM((N_HEADS, LANES, t), BF16), pltpu.VMEM((N_HEADS, t), F32),
                        pltpu.VMEM((N_HEADS, ACC_ROWS, t), F32)],
        compiler_params=pltpu.CompilerParams(dimension_semantics=("parallel", "arbitrary"),
                                             vmem_limit_bytes=VMEM_LIMIT),
        name="fox",
    )(qft, kaug, vft, crow)


NSA_TILE = 256
NSA_KEYS = 128
STAT_ROWS = 8
NSA_UNROLL = 8
NSA_AHEAD = 4


def _nsa_kernel(qt_ref, kcmp_ref, vcmpt_ref, ks_ref, vst_ref, kw_ref, vwt_ref, fgt_ref, bg_ref, ovl_ref, o_ref,
                qaug_ref, m_ref, acc_ref, ocmp_ref, *, topk):
    t = NSA_TILE
    tk = NSA_KEYS
    per_q = t // tk
    wide = GROUP * t
    n_cmp = kcmp_ref.shape[1]
    i = pl.program_id(1)
    start = i * t
    lane = lax.broadcasted_iota(jnp.int32, (1, LANES), 1)
    mine = [lane < HALF, lane >= HALF]
    zero_bf = jnp.zeros((), BF16)

    q4 = jnp.concatenate([qt_ref[n * LANES:(n + 1) * LANES, :] for n in range(GROUP)], axis=1)
    qpos4 = start + (lax.broadcasted_iota(jnp.int32, (1, wide), 1) & (t - 1))
    cur = (start + lax.broadcasted_iota(jnp.int32, (1, t), 1)) >> SEL_SHIFT
    sub8 = lax.broadcasted_iota(jnp.int32, (STAT_ROWS, t), 0)

    def cmp_rank(n_rows, n_slots):
        cmp_end = lax.broadcasted_iota(jnp.int32, (n_rows, 1), 0) * CMP_STRIDE + (CMP_LEN - 1)
        vis = cmp_end <= qpos4
        slot = lax.broadcasted_iota(jnp.int32, (n_slots, t), 0)
        valid = slot <= cur
        forced = (slot == 0) | (slot == cur) | (slot == cur - 1)
        for g in range(N_KV):
            s = jnp.where(vis, _dot(jnp.where(mine[g], kcmp_ref[0, 0:n_rows, :], zero_bf), q4), NEG)
            e = jnp.where(vis, jnp.exp2(s - jnp.max(s, axis=0, keepdims=True)), 0.0)
            denom = jnp.sum(e, axis=0, keepdims=True)
            p = e * jnp.where(denom > 0.0, 1.0 / denom, 0.0)
            ocmp_ref[g] = _dot(vcmpt_ref[0, g * HEAD_DIM:(g + 1) * HEAD_DIM, 0:n_rows], p.astype(BF16))
            psum = p[:, 0:t]
            for n in range(1, GROUP):
                psum = psum + p[:, n * t:(n + 1) * t]
            ovl = ovl_ref[0:n_slots, 0:n_rows]
            imp = sum(_dot(ovl, part.astype(BF16)) for part in _split3(psum))
            score = jnp.where(valid, imp + jnp.where(forced, FORCE_BONUS, 0.0), -1.0)
            tiles = [score[r:r + STAT_ROWS, :] for r in range(0, n_slots, STAT_ROWS)]
            ranks = [jnp.zeros((STAT_ROWS, t), F32) for _ in tiles]
            for jp in range(n_slots):
                other = jnp.broadcast_to(score[jp:jp + 1, :], (STAT_ROWS, t))
                for k, tile in enumerate(tiles):
                    first = k * STAT_ROWS
                    ge = jnp.where(other >= tile, 1.0, 0.0)
                    gt = jnp.where(other > tile, 1.0, 0.0)
                    if first > jp:
                        ahead = ge
                    elif first + STAT_ROWS - 1 <= jp:
                        ahead = gt
                    else:
                        ahead = jnp.where(sub8 + first > jp, ge, gt)
                    ranks[k] = ranks[k] + ahead
            rank = jnp.concatenate(ranks, axis=0)
            selb = jnp.where(rank < topk, 0.0, MASK_FILL).astype(BF16)
            if n_slots < SEL_SLOTS:
                selb = jnp.concatenate([selb, jnp.full((SEL_SLOTS - n_slots, t), MASK_FILL, BF16)], axis=0)
            selb4 = jnp.concatenate([selb] * GROUP, axis=1)
            qaug_ref[g] = jnp.concatenate([q4[0:HALF], selb4] if g == 0 else [selb4, q4[HALF:LANES]], axis=0)

    slots_per_tile = t // SEL_LEN
    lo = 0
    for n_slots in range(SEL_SLOTS // 4, SEL_SLOTS + 1, SEL_SLOTS // 4):
        hi = n_slots // slots_per_tile
        n_rows = min(n_cmp, -(-(hi * t // CMP_STRIDE) // LANES) * LANES)
        last = n_slots == SEL_SLOTS

        @pl.when((i >= lo) if last else ((i >= lo) & (i < hi)))
        def _(n_rows=n_rows, n_slots=n_slots):
            cmp_rank(n_rows, n_slots)

        lo = hi
    o_cmp = [ocmp_ref[g] for g in range(N_KV)]

    kr = lax.broadcasted_iota(jnp.int32, (tk, wide), 0)
    q_local = lax.broadcasted_iota(jnp.int32, (tk, wide), 1) & (t - 1)
    key_row = lax.broadcasted_iota(jnp.int32, (tk, LANES), 0)
    lane_full = lax.broadcasted_iota(jnp.int32, (tk, LANES), 1)
    slot_minus_row = [lane_full - HALF - (key_row >> SEL_SHIFT), lane_full - (key_row >> SEL_SHIFT)]

    def sel_score(off, g):
        onehot = jnp.where(slot_minus_row[g] == (off >> SEL_SHIFT), 1.0, 0.0).astype(BF16)
        return _dot(jnp.where(mine[g], ks_ref[0, pl.ds(off, tk), :], onehot), qaug_ref[g])

    def win_score(off, g):
        return _dot(jnp.where(mine[g], kw_ref[0, pl.ds(off, tk), :], zero_bf), q4)

    def sel_chains(tiles):
        return [(sel_score, vst_ref, g, g, off, mask) for off, mask in tiles for g in range(N_KV)]

    def win_chains(tiles):
        return [(win_score, vwt_ref, g, N_KV + g, off, mask) for off, mask in tiles for g in range(N_KV)]

    def sweep(chains):
        def issue(c):
            score, _, g, _, off, _ = chains[c]
            return score(off, g)

        pending = {c: issue(c) for c in range(min(NSA_AHEAD, len(chains)))}
        for c, (_, vt_ref_, g, slot_id, off, mask) in enumerate(chains):
            if c + NSA_AHEAD < len(chains):
                pending[c + NSA_AHEAD] = issue(c + NSA_AHEAD)
            s = pending.pop(c)
            if mask is not None:
                s = jnp.where(mask, s, MASK_FILL)
            _flash_step_t(s, vt_ref_[g * HEAD_DIM:(g + 1) * HEAD_DIM, pl.ds(off, tk)], m_ref, acc_ref, slot_id)

    def tile_off(j):
        return pl.multiple_of(j * tk, tk)

    n_before = i * per_q
    _flash_reset(m_ref, acc_ref)
    _for_tiles(n_before, NSA_UNROLL, lambda first, count: sweep(
        sel_chains([(tile_off(first + u), None) for u in range(count)])))

    own = [(tile_off(n_before + u), kr + u * tk <= q_local) for u in range(per_q)]
    n_back = WINDOW // tk

    @pl.when(n_before >= n_back)
    def _():
        back = [(tile_off(n_before - n_back + u), (q_local - kr < u * tk) if u < per_q else None)
                for u in range(n_back)]
        sweep(sel_chains(own) + win_chains(back + own))

    @pl.when(n_before < n_back)
    def _():
        sweep(sel_chains(own))

        def win_body(j, carry):
            sweep(win_chains([(tile_off(j), None)]))
            return carry

        lax.fori_loop(0, n_before, win_body, 0)
        sweep(win_chains(own))

    o_slc = [_flash_finish(acc_ref, g) for g in range(N_KV)]
    o_win = [_flash_finish(acc_ref, N_KV + g) for g in range(N_KV)]

    gates = jax.nn.sigmoid(fgt_ref[...] + bg_ref[...])
    for n in range(GROUP):
        cols = slice(n * t, (n + 1) * t)
        mixed = []
        for g in range(N_KV):
            base = N_HEADS + (g * GROUP + n) * 3
            mixed.append(gates[base:base + 1, :] * o_cmp[g][:, cols]
                         + gates[base + 1:base + 2, :] * o_slc[g][:, cols]
                         + gates[base + 2:base + 3, :] * o_win[g][:, cols])
        o_ref[0, :, n * LANES:(n + 1) * LANES] = jnp.concatenate(mixed, axis=0).T.astype(BF16)


def _nsa(qnt, kcmp, vcmpt, ks, vst, kw, vwt, fgt, bg_t, ovl_t, topk):
    b, seq, _ = ks.shape
    t = NSA_TILE
    nq = seq // t
    n_cmp = kcmp.shape[1]
    whole = lambda bi, i: (bi, 0, 0)
    whole_t = lambda bi, i: (0, bi)
    tile_t = lambda bi, i: (0, bi * nq + i)
    fixed = lambda bi, i: (0, 0)
    return pl.pallas_call(
        functools.partial(_nsa_kernel, topk=topk),
        grid=(b, nq),
        in_specs=[pl.BlockSpec((WIDTH, t), tile_t),
                  pl.BlockSpec((1, n_cmp, KV_WIDTH), whole), pl.BlockSpec((1, KV_WIDTH, n_cmp), whole),
                  pl.BlockSpec((1, seq, KV_WIDTH), whole), pl.BlockSpec((KV_WIDTH, seq), whole_t),
                  pl.BlockSpec((1, seq, KV_WIDTH), whole), pl.BlockSpec((KV_WIDTH, seq), whole_t),
                  pl.BlockSpec((LANES, t), tile_t), pl.BlockSpec((LANES, t), fixed),
                  pl.BlockSpec(ovl_t.shape, fixed)],
        out_specs=pl.BlockSpec((1, t, WIDTH), lambda bi, i: (bi, i, 0)),
        out_shape=jax.ShapeDtypeStruct((b, seq, WIDTH), BF16),
        scratch_shapes=[pltpu.VMEM((N_KV, LANES, GROUP * t), BF16), pltpu.VMEM((STAT_ROWS, GROUP * t), F32),
                        pltpu.VMEM((2 * N_KV, ACC_ROWS, GROUP * t), F32),
                        pltpu.VMEM((N_KV, HEAD_DIM, GROUP * t), F32)],
        compiler_params=pltpu.CompilerParams(dimension_semantics=("parallel", "arbitrary"),
                                             vmem_limit_bytes=VMEM_LIMIT),
        name="nsa",
    )(qnt, kcmp, vcmpt, ks, vst, kw, vwt, fgt, bg_t, ovl_t)


def _post_kernel(x_ref, of_ref, on_ref, gf_ref, gn_ref, wo_ref, gm_ref, wu_ref, wd_ref, gl_ref, o_ref):
    yf = _rms(of_ref[...].astype(F32), gf_ref[...]).astype(BF16)
    yn = _rms(on_ref[...].astype(F32), gn_ref[...]).astype(BF16)
    h1 = x_ref[...] + _dot(yf, wo_ref[0:WIDTH, :]) + _dot(yn, wo_ref[WIDTH:2 * WIDTH, :])
    u = _dot(_rms(h1, gm_ref[...]).astype(BF16), wu_ref[...])
    act = jnp.square(jnp.maximum(u, 0.0)).astype(BF16)
    h2 = h1 + _dot(act, wd_ref[...])
    o_ref[...] = _rms(h2, gl_ref[...])


def _post(x2, ofox, onsa, g_fox, g_nsa, w_out, g_mlp, w_up, w_down, g_final, tm):
    m, d = x2.shape
    row = lambda i: (i, 0)
    fixed = lambda i: (0, 0)
    full = lambda a: pl.BlockSpec(a.shape, fixed, pipeline_mode=pl.Buffered(1))
    return pl.pallas_call(
        _post_kernel,
        grid=(m // tm,),
        in_specs=[pl.BlockSpec((tm, d), row), pl.BlockSpec((tm, WIDTH), row), pl.BlockSpec((tm, WIDTH), row),
                  full(g_fox), full(g_nsa), full(w_out), full(g_mlp), full(w_up), full(w_down), full(g_final)],
        out_specs=pl.BlockSpec((tm, d), row),
        out_shape=jax.ShapeDtypeStruct((m, d), F32),
        compiler_params=pltpu.CompilerParams(dimension_semantics=("parallel",), vmem_limit_bytes=VMEM_LIMIT),
        name="post",
    )(x2, ofox, onsa, g_fox, g_nsa, w_out, g_mlp, w_up, w_down, g_final)


_NSA_PERM = np.array([(p % 2) * GROUP + p // 2 for p in range(N_HEADS)])


def _perm_heads(a, axis):
    shape = a.shape
    a = a.reshape(shape[:axis] + (N_HEADS, HEAD_DIM) + shape[axis + 1:])
    a = jnp.take(a, _NSA_PERM, axis=axis)
    return a.reshape(shape)


def _prep_w_in(w):
    qscale = HEAD_DIM ** -0.5 * LOG2E
    sizes = [WIDTH, WIDTH, WIDTH, N_HEADS, WIDTH] + [KV_WIDTH] * 6 + [3 * N_HEADS]
    offs = np.cumsum([0] + sizes)
    qf, kf, vf, fl, qn, kc, vc, ks, vs, kw, vw, gate = [w[:, offs[k]:offs[k + 1]] for k in range(len(sizes))]
    fg = jnp.concatenate([fl, gate, jnp.zeros((w.shape[0], LANES - 4 * N_HEADS), w.dtype)], axis=1)
    w_nat = jnp.concatenate([kf, kc, vc, ks, kw], axis=1).astype(BF16)
    w_tr = jnp.concatenate([qf * qscale, vf, _perm_heads(qn * qscale, 1), vs, vw, fg], axis=1).T.astype(BF16)
    assert w_nat.shape[1] == NAT_COLS and w_tr.shape[0] == TR_ROWS
    return w_nat, w_tr


def _prep_compress(pe, w1, b1, w2, b2):
    half = CMP_LEN // 2
    eye = jnp.eye(N_KV, dtype=F32)

    def expand_w1(wpart):
        w3 = wpart.reshape(half, HEAD_DIM, CMP_HIDDEN)
        return jnp.einsum('ldh,gk->lgdkh', w3, eye).reshape(half * KV_WIDTH, N_KV * CMP_HIDDEN).astype(BF16)

    def expand_pe(ppart):
        flat = jnp.broadcast_to(ppart[:, None, :], (half, N_KV, HEAD_DIM)).reshape(1, half * KV_WIDTH)
        return jnp.broadcast_to(flat, (8, half * KV_WIDTH))

    w2b = jnp.einsum('hd,gk->ghkd', w2, eye).reshape(N_KV * CMP_HIDDEN, KV_WIDTH).astype(BF16)
    return (expand_pe(pe[:half]), expand_pe(pe[half:]),
            expand_w1(w1[:half * HEAD_DIM]), expand_w1(w1[half * HEAD_DIM:]),
            jnp.tile(b1, N_KV)[None, :], w2b, jnp.tile(b2, N_KV)[None, :])


def _rope_tables(pos, reps):
    half = HEAD_DIM // 2
    inv = ROPE_THETA ** (-np.arange(half, dtype=np.float64) / half)
    ang = np.asarray(pos, np.float64)[:, None] * inv[None, :]
    cos = np.cos(ang)
    sin = np.sin(ang)
    return (np.tile(np.concatenate([cos, cos], axis=1), (1, reps)).astype(np.float32),
            np.tile(np.concatenate([-sin, sin], axis=1), (1, reps)).astype(np.float32))


def _overlap_t(n_cmp_slots, n_cmp, n_blocks):
    c = np.arange(n_cmp_slots)[None, :] * CMP_STRIDE
    j = np.arange(SEL_SLOTS)[:, None]
    s = j * SEL_LEN
    ovl = (c < s + SEL_LEN) & (c + CMP_LEN > s) & (np.arange(n_cmp_slots)[None, :] < n_cmp) & (j < n_blocks)
    return jnp.asarray(ovl.astype(np.float32), BF16)


def _row_tile(m, want):
    t = want
    while m % t:
        t //= 2
    return t


def _layer(h, seq, g_attn, w_in, b_f, b_gate, cmpk, cmpv, g_fox, g_nsa, w_out, g_mlp, w_up, w_down, g_out):
    m, d = h.shape
    b = m // seq
    n_chunks = seq // CMP_STRIDE
    n_cmp = (seq - CMP_LEN) // CMP_STRIDE + 1
    n_blocks = seq // SEL_LEN
    assert seq % FOX_TILE == 0 and seq % NSA_TILE == 0 and n_chunks % LANES == 0
    assert SEL_TOPK <= n_blocks <= SEL_SLOTS
    assert n_cmp == n_chunks - 1

    cos_n, sin_n = _rope_tables(np.arange(seq), N_KV)
    cos_c, sin_c = _rope_tables(np.arange(n_chunks) * CMP_STRIDE + CMP_LEN - 1, N_KV)
    cos_t = np.ascontiguousarray(cos_n.T)
    sin_t = np.ascontiguousarray(sin_n.T)

    tm = _row_tile(seq, 512)
    w_nat, w_tr = _prep_w_in(w_in)
    kf, kc, vc, ks, kw, qft, vft, qnt, vst, vwt, fgt = _inproj(
        h, g_attn[None, :], w_nat, w_tr, cos_n, sin_n, cos_t, sin_t, seq, tm)
    r3 = lambda a: a.reshape(b, seq, a.shape[-1])

    bf_pad = jnp.zeros((1, LANES), F32).at[0, :N_HEADS].set(b_f)
    bg_col = jnp.zeros((LANES,), F32).at[N_HEADS:4 * N_HEADS].set(b_gate)
    bg_t = jnp.broadcast_to(bg_col[:, None], (LANES, NSA_TILE))
    crow, kaug = _cumgate(fgt, bf_pad, r3(kf))

    kcmp, vcmpt = _compress(r3(kc), r3(vc), cos_c, sin_c, _prep_compress(*cmpk), _prep_compress(*cmpv))

    ofox = _fox(qft, kaug, vft, crow)
    onsa = _nsa(qnt, kcmp, vcmpt, r3(ks), vst, r3(kw), vwt, fgt, bg_t,
                _overlap_t(n_chunks, n_cmp, n_blocks), min(SEL_TOPK, n_blocks))

    w_out_p = jnp.concatenate([w_out[:WIDTH], _perm_heads(w_out[WIDTH:], 0)], axis=0).astype(BF16)
    return _post(h, ofox.reshape(m, WIDTH), onsa.reshape(m, WIDTH), g_fox[None, :], _perm_heads(g_nsa, 0)[None, :],
                 w_out_p, g_mlp[None, :], w_up.astype(BF16), w_down.astype(BF16), g_out[None, :],
                 _row_tile(seq, 512))


def kernel(x, g_attn, w_in, b_f, b_gate, cmpk_pe, cmpk_w1, cmpk_b1, cmpk_w2, cmpk_b2, cmpv_pe, cmpv_w1, cmpv_b1,
           cmpv_w2, cmpv_b2, g_fox, g_nsa, w_out, g_mlp, w_up, w_down, g_final):
    b, seq, d = x.shape
    depth = g_attn.shape[0]
    assert depth == 1, "the final rmsnorm is fused into the (single) layer's last kernel"
    h = x.reshape(b * seq, d)
    out = _layer(h, seq, g_attn[0], w_in[0], b_f[0], b_gate[0],
                 (cmpk_pe[0], cmpk_w1[0], cmpk_b1[0], cmpk_w2[0], cmpk_b2[0]),
                 (cmpv_pe[0], cmpv_w1[0], cmpv_b1[0], cmpv_w2[0], cmpv_b2[0]),
                 g_fox[0], g_nsa[0], w_out[0], g_mlp[0], w_up[0], w_down[0], g_final)
    return out.reshape(b, seq, d)
```

```python
import functools

import numpy as np
import jax
import jax.numpy as jnp
from jax import lax
from jax.experimental import pallas as pl
from jax.experimental.pallas import tpu as pltpu

F32 = jnp.float32
BF16 = jnp.bfloat16

HEAD_DIM = 64
N_HEADS = 8
N_KV = 2
GROUP = N_HEADS // N_KV
WIDTH = N_HEADS * HEAD_DIM
KV_WIDTH = N_KV * HEAD_DIM
CMP_LEN = 32
CMP_STRIDE = 16
CMP_HIDDEN = 256
SEL_LEN = 64
SEL_SHIFT = 6
SEL_TOPK = 16
SEL_SLOTS = 64
WINDOW = 512
ROPE_THETA = 10000.0
EPS = 1e-6
NEG = -1e30
MASK_FILL = -(2.0 ** 100)
FORCE_BONUS = 1e4
LOG2E = 1.4426950408889634
LANES = 128
HALF = LANES // 2
BF16_ROWS = 16

VMEM_LIMIT = 56 * 1024 * 1024


def _dot(a, b, precision=None):
    return jnp.dot(a, b, preferred_element_type=F32, precision=precision)


def _dot_nt(a, b, precision=None):
    return lax.dot_general(a, b, (((1,), (1,)), ((), ())), preferred_element_type=F32, precision=precision)


def _rms(x, g):
    return x * lax.rsqrt(jnp.mean(x * x, axis=-1, keepdims=True) + EPS) * g


def _rope(x, cos, sin_signed):
    w = x.shape[-1]
    lane = lax.broadcasted_iota(jnp.int32, (1, w), 1)
    first = (lane % HEAD_DIM) < (HEAD_DIM // 2)
    partner = jnp.where(first, pltpu.roll(x, w - HEAD_DIM // 2, 1), pltpu.roll(x, HEAD_DIM // 2, 1))
    return x * cos + partner * sin_signed


def _rope_t(x, cos, sin_signed):
    r = x.shape[0]
    row = lax.broadcasted_iota(jnp.int32, (r, 1), 0)
    first = (row % HEAD_DIM) < (HEAD_DIM // 2)
    partner = jnp.where(first, pltpu.roll(x, r - HEAD_DIM // 2, 0), pltpu.roll(x, HEAD_DIM // 2, 0))
    return x * cos + partner * sin_signed


def _split3(c):
    c1 = c.astype(BF16).astype(F32)
    r1 = c - c1
    c2 = r1.astype(BF16).astype(F32)
    return c1, c2, r1 - c2


NAT_KF, NAT_KC, NAT_VC, NAT_KS, NAT_KW, NAT_COLS = 0, 512, 640, 768, 896, 1024
TR_QF, TR_VF, TR_QN, TR_VS, TR_VW, TR_FG, TR_ROWS = 0, 512, 1024, 1536, 1664, 1792, 1920


def _inproj_kernel(x_ref, g_ref, wn_ref, wt_ref, cos_ref, sin_ref, cost_ref, sint_ref,
                   kf_ref, kc_ref, vc_ref, ks_ref, kw_ref,
                   qft_ref, vft_ref, qnt_ref, vst_ref, vwt_ref, fgt_ref):
    hb = _rms(x_ref[...], g_ref[...]).astype(BF16)

    def nat(lo, width):
        return _dot(hb, wn_ref[:, lo:lo + width])

    def tr(lo, rows):
        return _dot_nt(wt_ref[lo:lo + rows, :], hb)

    cos = cos_ref[...]
    sin = sin_ref[...]
    kf_ref[...] = nat(NAT_KF, WIDTH).astype(BF16)
    kc_ref[...] = nat(NAT_KC, KV_WIDTH)
    vc_ref[...] = nat(NAT_VC, KV_WIDTH)
    ks_ref[...] = _rope(nat(NAT_KS, KV_WIDTH), cos, sin).astype(BF16)
    kw_ref[...] = _rope(nat(NAT_KW, KV_WIDTH), cos, sin).astype(BF16)
    qft_ref[...] = tr(TR_QF, WIDTH).astype(BF16)
    vft_ref[...] = tr(TR_VF, WIDTH).astype(BF16)
    pairs = WIDTH // KV_WIDTH
    cos_rows = jnp.concatenate([cost_ref[...]] * pairs, axis=0)
    sin_rows = jnp.concatenate([sint_ref[...]] * pairs, axis=0)
    qnt_ref[...] = _rope_t(tr(TR_QN, WIDTH), cos_rows, sin_rows).astype(BF16)
    vst_ref[...] = tr(TR_VS, KV_WIDTH).astype(BF16)
    vwt_ref[...] = tr(TR_VW, KV_WIDTH).astype(BF16)
    fgt_ref[...] = tr(TR_FG, LANES)


def _inproj(x2, g_attn, w_nat, w_tr, cos_n, sin_n, cos_t, sin_t, seq, tm):
    m, d = x2.shape
    sblocks = seq // tm
    row = lambda i: (i, 0)
    col = lambda i: (0, i)
    fixed = lambda i: (0, 0)
    tab = lambda i: (i % sblocks, 0)
    tab_t = lambda i: (0, i % sblocks)
    sds = jax.ShapeDtypeStruct
    out_shape = [sds((m, WIDTH), BF16)] + [sds((m, KV_WIDTH), F32)] * 2 + [sds((m, KV_WIDTH), BF16)] * 2 + [
                 sds((WIDTH, m), BF16), sds((WIDTH, m), BF16), sds((WIDTH, m), BF16),
                 sds((KV_WIDTH, m), BF16), sds((KV_WIDTH, m), BF16), sds((LANES, m), F32)]
    out_specs = ([pl.BlockSpec((tm, WIDTH), row)] + [pl.BlockSpec((tm, KV_WIDTH), row)] * 4
                 + [pl.BlockSpec((WIDTH, tm), col)] * 3 + [pl.BlockSpec((KV_WIDTH, tm), col)] * 2
                 + [pl.BlockSpec((LANES, tm), col)])
    return pl.pallas_call(
        _inproj_kernel,
        grid=(m // tm,),
        in_specs=[pl.BlockSpec((tm, d), row), pl.BlockSpec((1, d), fixed),
                  pl.BlockSpec(w_nat.shape, fixed), pl.BlockSpec(w_tr.shape, fixed),
                  pl.BlockSpec((tm, KV_WIDTH), tab), pl.BlockSpec((tm, KV_WIDTH), tab),
                  pl.BlockSpec((KV_WIDTH, tm), tab_t), pl.BlockSpec((KV_WIDTH, tm), tab_t)],
        out_specs=out_specs,
        out_shape=out_shape,
        compiler_params=pltpu.CompilerParams(dimension_semantics=("parallel",), vmem_limit_bytes=VMEM_LIMIT),
        name="inproj",
    )(x2, g_attn, w_nat, w_tr, cos_n, sin_n, cos_t, sin_t)


CUM_BLOCK = 256
CUM_UNROLL = 8
N_BIAS = 3


def _bias_lane_base(h):
    return HALF if h % 2 == 0 else 0


def _bias_placement():
    p = np.zeros((LANES, N_HEADS * LANES), np.float32)
    for h in range(N_HEADS):
        base = h * LANES + _bias_lane_base(h)
        for part in range(N_BIAS):
            p[part * N_HEADS + h, base + N_BIAS + part] = -1.0
            p[N_BIAS * N_HEADS, base + part] = 1.0
    return jnp.asarray(p, BF16)


def _cumgate_kernel(fgt_ref, bf_ref, kf_ref, place_ref, crow_ref, kaug_ref):
    seq = fgt_ref.shape[1]
    r = lax.broadcasted_iota(jnp.int32, (CUM_BLOCK, CUM_BLOCK), 0)
    c = lax.broadcasted_iota(jnp.int32, (CUM_BLOCK, CUM_BLOCK), 1)
    tri = jnp.where(r >= c, 1.0, 0.0).astype(BF16)
    er = lax.broadcasted_iota(jnp.int32, (N_HEADS, LANES), 0)
    ec = lax.broadcasted_iota(jnp.int32, (N_HEADS, LANES), 1)
    pick = jnp.where(er == ec, 1.0, 0.0).astype(BF16)
    bias = bf_ref[...]
    lane = lax.broadcasted_iota(jnp.int32, (CUM_BLOCK, LANES), 1)

    def body(blk, carry):
        off = pl.multiple_of(blk * CUM_BLOCK, CUM_BLOCK)
        z = fgt_ref[:, pl.ds(off, CUM_BLOCK)].T + bias
        logf = jnp.minimum(z, 0.0) - jnp.log(1.0 + jnp.exp(-jnp.abs(z)))
        cs = sum(_dot(tri, part.astype(BF16)) for part in _split3(logf)) + carry
        c2 = cs * LOG2E
        c1, cb, cc = _split3(jnp.where(lane < N_HEADS, c2, 0.0))
        crow_ref[0, :, pl.ds(off, CUM_BLOCK)] = sum(_dot_nt(pick, part.astype(BF16)) for part in (c1, cb, cc))
        parts = (c1 + pltpu.roll(cb, N_HEADS, 1) + pltpu.roll(cc, 2 * N_HEADS, 1)
                 + jnp.where(lane == N_BIAS * N_HEADS, 1.0, 0.0))
        feat = _dot(parts.astype(BF16), place_ref[...]).astype(BF16)
        for h in range(N_HEADS):
            mine = (lane < HALF) if h % 2 == 0 else (lane >= HALF)
            kcol = kf_ref[0, pl.ds(off, CUM_BLOCK), (h // 2) * LANES:(h // 2 + 1) * LANES]
            kaug_ref[0, h, pl.ds(off, CUM_BLOCK), :] = jnp.where(mine, kcol, feat[:, h * LANES:(h + 1) * LANES])
        return cs[CUM_BLOCK - 1:CUM_BLOCK, :]

    lax.fori_loop(0, seq // CUM_BLOCK, body, jnp.zeros((1, LANES), F32), unroll=CUM_UNROLL)


def _cumgate(fgt, bf_pad, kf3):
    b, seq, _ = kf3.shape
    place = _bias_placement()
    return pl.pallas_call(
        _cumgate_kernel,
        grid=(b,),
        in_specs=[pl.BlockSpec((LANES, seq), lambda i: (0, i)), pl.BlockSpec((1, LANES), lambda i: (0, 0)),
                  pl.BlockSpec((1, seq, WIDTH), lambda i: (i, 0, 0)), pl.BlockSpec(place.shape, lambda i: (0, 0))],
        out_specs=[pl.BlockSpec((1, N_HEADS, seq), lambda i: (i, 0, 0)),
                   pl.BlockSpec((1, N_HEADS, seq, LANES), lambda i: (i, 0, 0, 0))],
        out_shape=[jax.ShapeDtypeStruct((b, N_HEADS, seq), F32),
                   jax.ShapeDtypeStruct((b, N_HEADS, seq, LANES), BF16)],
        compiler_params=pltpu.CompilerParams(dimension_semantics=("parallel",), vmem_limit_bytes=VMEM_LIMIT),
        name="cumgate",
    )(fgt, bf_pad, kf3, place)


def _compress_one(x_ref, pea_ref, peb_ref, wa_ref, wb_ref, b1_ref, w2_ref, b2_ref):
    n = x_ref.shape[1] // CMP_STRIDE
    xk = jnp.concatenate([x_ref[0, pl.ds(l, n, stride=CMP_STRIDE), :].astype(BF16) for l in range(CMP_STRIDE)],
                         axis=1)
    wa = wa_ref[...]
    wb = wb_ref[...]
    first = _dot(xk, wa)
    second = _dot(xk, wb)
    pe_term = _dot(pea_ref[...].astype(BF16), wa) + _dot(peb_ref[...].astype(BF16), wb)
    hidden = first + pltpu.roll(second, n - 1, 0) + pe_term[0:1, :] + b1_ref[...]
    act = jax.nn.gelu(hidden)
    return _dot(act.astype(BF16), w2_ref[...]) + b2_ref[...]


def _compress_kernel(kc_ref, vc_ref, cos_ref, sin_ref,
                     kpea, kpeb, kwa, kwb, kb1, kw2, kb2,
                     vpea, vpeb, vwa, vwb, vb1, vw2, vb2,
                     kcmp_ref, vcmpt_ref):
    kcmp = _compress_one(kc_ref, kpea, kpeb, kwa, kwb, kb1, kw2, kb2)
    kcmp_ref[0] = _rope(kcmp, cos_ref[...], sin_ref[...]).astype(BF16)
    vcmpt_ref[0] = _compress_one(vc_ref, vpea, vpeb, vwa, vwb, vb1, vw2, vb2).T.astype(BF16)


def _compress(kc3, vc3, cos_c, sin_c, kparams, vparams):
    b, seq, w = kc3.shape
    n = seq // CMP_STRIDE
    x_spec = pl.BlockSpec((1, seq, w), lambda i: (i, 0, 0))
    full = lambda a: pl.BlockSpec(a.shape, lambda i: (0,) * a.ndim)
    params = list(kparams) + list(vparams)
    return pl.pallas_call(
        _compress_kernel,
        grid=(b,),
        in_specs=[x_spec, x_spec, full(cos_c), full(sin_c)] + [full(p) for p in params],
        out_specs=[pl.BlockSpec((1, n, KV_WIDTH), lambda i: (i, 0, 0)),
                   pl.BlockSpec((1, KV_WIDTH, n), lambda i: (i, 0, 0))],
        out_shape=[jax.ShapeDtypeStruct((b, n, KV_WIDTH), BF16), jax.ShapeDtypeStruct((b, KV_WIDTH, n), BF16)],
        compiler_params=pltpu.CompilerParams(dimension_semantics=("parallel",), vmem_limit_bytes=VMEM_LIMIT),
        name="compress",
    )(kc3, vc3, cos_c, sin_c, *params)


ACC_ROWS = HEAD_DIM + BF16_ROWS


def _flash_step_t(s, vt, m_ref, acc_ref, idx):
    row = slice(idx, idx + 1)
    m_old = m_ref[row, :]
    m_new = jnp.maximum(m_old, jnp.max(s, axis=0, keepdims=True))
    alpha = jnp.exp2(m_old - m_new)
    p = jnp.exp2((s - m_new).astype(BF16))
    m_ref[row, :] = m_new
    vt_ones = jnp.concatenate([vt, jnp.ones((BF16_ROWS, vt.shape[1]), BF16)], axis=0)
    acc_ref[idx] = alpha * acc_ref[idx] + _dot(vt_ones, p)


def _flash_reset(m_ref, acc_ref):
    m_ref[...] = jnp.full(m_ref.shape, NEG, F32)
    acc_ref[...] = jnp.zeros(acc_ref.shape, F32)


def _flash_finish(acc_ref, idx):
    acc = acc_ref[idx]
    return acc[0:HEAD_DIM] * (1.0 / acc[HEAD_DIM:HEAD_DIM + 1])


def _for_tiles(n, unroll, visit):
    groups = n // unroll

    def body(jj, carry):
        visit(jj * unroll, unroll)
        return carry

    lax.fori_loop(0, groups, body, 0)
    done = groups * unroll
    rem = n - done
    p = unroll // 2
    while p >= 1:
        first = done + (rem & ~(2 * p - 1))

        @pl.when((rem & p) != 0)
        def _(first=first, p=p):
            visit(first, p)

        p //= 2


FOX_TILE = 256
FOX_KEYS = 256
FOX_AHEAD = 5
FOX_UNROLL = 8


def _fox_kernel(qt_ref, kaug_ref, vt_ref, crow_ref, o_ref, qaug_ref, m_ref, acc_ref):
    t = FOX_TILE
    tk = FOX_KEYS
    assert t == tk
    i = pl.program_id(1)
    kr = lax.broadcasted_iota(jnp.int32, (tk, t), 0)
    qc = lax.broadcasted_iota(jnp.int32, (tk, t), 1)
    frow = lax.broadcasted_iota(jnp.int32, (BF16_ROWS, t), 0)
    pad = jnp.zeros((HALF - BF16_ROWS, t), BF16)

    _flash_reset(m_ref, acc_ref)
    for h in range(N_HEADS):
        c1, c2, c3 = _split3(crow_ref[0, h:h + 1, :])
        feat = jnp.where(frow == 0, c1, jnp.where(frow == 1, c2, jnp.where(frow == 2, c3,
               jnp.where(frow < 2 * N_BIAS, 1.0, 0.0)))).astype(BF16)
        qh = qt_ref[h * HEAD_DIM:(h + 1) * HEAD_DIM, :]
        parts = [qh, feat, pad] if _bias_lane_base(h) == HALF else [feat, pad, qh]
        qaug_ref[h] = jnp.concatenate(parts, axis=0)

    def step(tiles):
        chains = [(off, mask, h) for off, mask in tiles for h in range(N_HEADS)]

        def scores(c):
            off, _, h = chains[c]
            return _dot(kaug_ref[0, h, pl.ds(off, tk), :], qaug_ref[h])

        pending = {c: scores(c) for c in range(FOX_AHEAD)}
        for c, (off, mask, h) in enumerate(chains):
            if c + FOX_AHEAD < len(chains):
                pending[c + FOX_AHEAD] = scores(c + FOX_AHEAD)
            s = pending.pop(c)
            if mask is not None:
                s = jnp.where(mask, s, MASK_FILL)
            _flash_step_t(s, vt_ref[h * HEAD_DIM:(h + 1) * HEAD_DIM, pl.ds(off, tk)], m_ref, acc_ref, h)

    def tile_off(j):
        return pl.multiple_of(j * tk, tk)

    groups = i // FOX_UNROLL

    def body(jj, carry):
        step([(tile_off(jj * FOX_UNROLL + u), None) for u in range(FOX_UNROLL)])
        return carry

    lax.fori_loop(0, groups, body, 0)
    done = groups * FOX_UNROLL
    for left in range(FOX_UNROLL):
        @pl.when(i - done == left)
        def _(left=left):
            step([(tile_off(done + u), None) for u in range(left)] + [(tile_off(i), kr <= qc)])

    for col in range(WIDTH // LANES):
        pair = [_flash_finish(acc_ref, h) for h in (2 * col, 2 * col + 1)]
        o_ref[0, :, col * LANES:(col + 1) * LANES] = jnp.concatenate(pair, axis=0).T.astype(BF16)


def _fox(qft, kaug, vft, crow):
    b, _, seq, _ = kaug.shape
    t = FOX_TILE
    nq = seq // t
    return pl.pallas_call(
        _fox_kernel,
        grid=(b, nq),
        in_specs=[pl.BlockSpec((WIDTH, t), lambda bi, i: (0, bi * nq + i)),
                  pl.BlockSpec((1, N_HEADS, seq, LANES), lambda bi, i: (bi, 0, 0, 0)),
                  pl.BlockSpec((WIDTH, seq), lambda bi, i: (0, bi)),
                  pl.BlockSpec((1, N_HEADS, t), lambda bi, i: (bi, 0, i))],
        out_specs=pl.BlockSpec((1, t, WIDTH), lambda bi, i: (bi, i, 0)),
        out_shape=jax.ShapeDtypeStruct((b, seq, WIDTH), BF16),
        scratch_shapes=[pltpu.VMEM((N_HEADS, LANES, t), BF16), pltpu.VMEM((N_HEADS, t), F32),
                        pltpu.VMEM((N_HEADS, ACC_ROWS, t), F32)],
        compiler_params=pltpu.CompilerParams(dimension_semantics=("parallel", "arbitrary"),
                                             vmem_limit_bytes=VMEM_LIMIT),
        name="fox",
    )(qft, kaug, vft, crow)


NSA_TILE = 256
NSA_KEYS = 128
STAT_ROWS = 8
NSA_UNROLL = 8
NSA_AHEAD = 4


def _nsa_kernel(qt_ref, kcmp_ref, vcmpt_ref, ks_ref, vst_ref, kw_ref, vwt_ref, fgt_ref, bg_ref, ovl_ref, o_ref,
                qaug_ref, m_ref, acc_ref, ocmp_ref, *, topk):
    t = NSA_TILE
    tk = NSA_KEYS
    per_q = t // tk
    wide = GROUP * t
    n_cmp = kcmp_ref.shape[1]
    i = pl.program_id(1)
    start = i * t
    lane = lax.broadcasted_iota(jnp.int32, (1, LANES), 1)
    mine = [lane < HALF, lane >= HALF]
    zero_bf = jnp.zeros((), BF16)

    q4 = jnp.concatenate([qt_ref[n * LANES:(n + 1) * LANES, :] for n in range(GROUP)], axis=1)
    qpos4 = start + (lax.broadcasted_iota(jnp.int32, (1, wide), 1) & (t - 1))
    cur = (start + lax.broadcasted_iota(jnp.int32, (1, t), 1)) >> SEL_SHIFT
    sub8 = lax.broadcasted_iota(jnp.int32, (STAT_ROWS, t), 0)

    def cmp_rank(n_rows, n_slots):
        cmp_end = lax.broadcasted_iota(jnp.int32, (n_rows, 1), 0) * CMP_STRIDE + (CMP_LEN - 1)
        vis = cmp_end <= qpos4
        slot = lax.broadcasted_iota(jnp.int32, (n_slots, t), 0)
        valid = slot <= cur
        forced = (slot == 0) | (slot == cur) | (slot == cur - 1)
        for g in range(N_KV):
            s = jnp.where(vis, _dot(jnp.where(mine[g], kcmp_ref[0, 0:n_rows, :], zero_bf), q4), NEG)
            e = jnp.where(vis, jnp.exp2(s - jnp.max(s, axis=0, keepdims=True)), 0.0)
            denom = jnp.sum(e, axis=0, keepdims=True)
            p = e * jnp.where(denom > 0.0, 1.0 / denom, 0.0)
            ocmp_ref[g] = _dot(vcmpt_ref[0, g * HEAD_DIM:(g + 1) * HEAD_DIM, 0:n_rows], p.astype(BF16))
            psum = p[:, 0:t]
            for n in range(1, GROUP):
                psum = psum + p[:, n * t:(n + 1) * t]
            ovl = ovl_ref[0:n_slots, 0:n_rows]
            imp = sum(_dot(ovl, part.astype(BF16)) for part in _split3(psum))
            score = jnp.where(valid, imp + jnp.where(forced, FORCE_BONUS, 0.0), -1.0)
            tiles = [score[r:r + STAT_ROWS, :] for r in range(0, n_slots, STAT_ROWS)]
            ranks = [jnp.zeros((STAT_ROWS, t), F32) for _ in tiles]
            for jp in range(n_slots):
                other = jnp.broadcast_to(score[jp:jp + 1, :], (STAT_ROWS, t))
                for k, tile in enumerate(tiles):
                    first = k * STAT_ROWS
                    ge = jnp.where(other >= tile, 1.0, 0.0)
                    gt = jnp.where(other > tile, 1.0, 0.0)
                    if first > jp:
                        ahead = ge
                    elif first + STAT_ROWS - 1 <= jp:
                        ahead = gt
                    else:
                        ahead = jnp.where(sub8 + first > jp, ge, gt)
                    ranks[k] = ranks[k] + ahead
            rank = jnp.concatenate(ranks, axis=0)
            selb = jnp.where(rank < topk, 0.0, MASK_FILL).astype(BF16)
            if n_slots < SEL_SLOTS:
                selb = jnp.concatenate([selb, jnp.full((SEL_SLOTS - n_slots, t), MASK_FILL, BF16)], axis=0)
            selb4 = jnp.concatenate([selb] * GROUP, axis=1)
            qaug_ref[g] = jnp.concatenate([q4[0:HALF], selb4] if g == 0 else [selb4, q4[HALF:LANES]], axis=0)

    slots_per_tile = t // SEL_LEN
    lo = 0
    for n_slots in range(SEL_SLOTS // 4, SEL_SLOTS + 1, SEL_SLOTS // 4):
        hi = n_slots // slots_per_tile
        n_rows = min(n_cmp, -(-(hi * t // CMP_STRIDE) // LANES) * LANES)
        last = n_slots == SEL_SLOTS

        @pl.when((i >= lo) if last else ((i >= lo) & (i < hi)))
        def _(n_rows=n_rows, n_slots=n_slots):
            cmp_rank(n_rows, n_slots)

        lo = hi
    o_cmp = [ocmp_ref[g] for g in range(N_KV)]

    kr = lax.broadcasted_iota(jnp.int32, (tk, wide), 0)
    q_local = lax.broadcasted_iota(jnp.int32, (tk, wide), 1) & (t - 1)
    key_row = lax.broadcasted_iota(jnp.int32, (tk, LANES), 0)
    lane_full = lax.broadcasted_iota(jnp.int32, (tk, LANES), 1)
    slot_minus_row = [lane_full - HALF - (key_row >> SEL_SHIFT), lane_full - (key_row >> SEL_SHIFT)]

    def sel_score(off, g):
        onehot = jnp.where(slot_minus_row[g] == (off >> SEL_SHIFT), 1.0, 0.0).astype(BF16)
        return _dot(jnp.where(mine[g], ks_ref[0, pl.ds(off, tk), :], onehot), qaug_ref[g])

    def win_score(off, g):
        return _dot(jnp.where(mine[g], kw_ref[0, pl.ds(off, tk), :], zero_bf), q4)

    def sel_chains(tiles):
        return [(sel_score, vst_ref, g, g, off, mask) for off, mask in tiles for g in range(N_KV)]

    def win_chains(tiles):
        return [(win_score, vwt_ref, g, N_KV + g, off, mask) for off, mask in tiles for g in range(N_KV)]

    def sweep(chains):
        def issue(c):
            score, _, g, _, off, _ = chains[c]
            return score(off, g)

        pending = {c: issue(c) for c in range(min(NSA_AHEAD, len(chains)))}
        for c, (_, vt_ref_, g, slot_id, off, mask) in enumerate(chains):
            if c + NSA_AHEAD < len(chains):
                pending[c + NSA_AHEAD] = issue(c + NSA_AHEAD)
            s = pending.pop(c)
            if mask is not None:
                s = jnp.where(mask, s, MASK_FILL)
            _flash_step_t(s, vt_ref_[g * HEAD_DIM:(g + 1) * HEAD_DIM, pl.ds(off, tk)], m_ref, acc_ref, slot_id)

    def tile_off(j):
        return pl.multiple_of(j * tk, tk)

    n_before = i * per_q
    _flash_reset(m_ref, acc_ref)
    _for_tiles(n_before, NSA_UNROLL, lambda first, count: sweep(
        sel_chains([(tile_off(first + u), None) for u in range(count)])))

    own = [(tile_off(n_before + u), kr + u * tk <= q_local) for u in range(per_q)]
    n_back = WINDOW // tk

    @pl.when(n_before >= n_back)
    def _():
        back = [(tile_off(n_before - n_back + u), (q_local - kr < u * tk) if u < per_q else None)
                for u in range(n_back)]
        sweep(sel_chains(own) + win_chains(back + own))

    for early in range(0, n_back, per_q):
        @pl.when(n_before == early)
        def _(early=early):
            before = [(tile_off(u), None) for u in range(early)]
            sweep(sel_chains(own) + win_chains(before + own))

    o_slc = [_flash_finish(acc_ref, g) for g in range(N_KV)]
    o_win = [_flash_finish(acc_ref, N_KV + g) for g in range(N_KV)]

    gates = jax.nn.sigmoid(fgt_ref[...] + bg_ref[...])
    for n in range(GROUP):
        cols = slice(n * t, (n + 1) * t)
        mixed = []
        for g in range(N_KV):
            base = N_HEADS + (g * GROUP + n) * 3
            mixed.append(gates[base:base + 1, :] * o_cmp[g][:, cols]
                         + gates[base + 1:base + 2, :] * o_slc[g][:, cols]
                         + gates[base + 2:base + 3, :] * o_win[g][:, cols])
        o_ref[0, :, n * LANES:(n + 1) * LANES] = jnp.concatenate(mixed, axis=0).T.astype(BF16)


def _nsa(qnt, kcmp, vcmpt, ks, vst, kw, vwt, fgt, bg_t, ovl_t, topk):
    b, seq, _ = ks.shape
    t = NSA_TILE
    nq = seq // t
    n_cmp = kcmp.shape[1]
    whole = lambda bi, i: (bi, 0, 0)
    whole_t = lambda bi, i: (0, bi)
    tile_t = lambda bi, i: (0, bi * nq + i)
    fixed = lambda bi, i: (0, 0)
    return pl.pallas_call(
        functools.partial(_nsa_kernel, topk=topk),
        grid=(b, nq),
        in_specs=[pl.BlockSpec((WIDTH, t), tile_t),
                  pl.BlockSpec((1, n_cmp, KV_WIDTH), whole), pl.BlockSpec((1, KV_WIDTH, n_cmp), whole),
                  pl.BlockSpec((1, seq, KV_WIDTH), whole), pl.BlockSpec((KV_WIDTH, seq), whole_t),
                  pl.BlockSpec((1, seq, KV_WIDTH), whole), pl.BlockSpec((KV_WIDTH, seq), whole_t),
                  pl.BlockSpec((LANES, t), tile_t), pl.BlockSpec((LANES, t), fixed),
                  pl.BlockSpec(ovl_t.shape, fixed)],
        out_specs=pl.BlockSpec((1, t, WIDTH), lambda bi, i: (bi, i, 0)),
        out_shape=jax.ShapeDtypeStruct((b, seq, WIDTH), BF16),
        scratch_shapes=[pltpu.VMEM((N_KV, LANES, GROUP * t), BF16), pltpu.VMEM((STAT_ROWS, GROUP * t), F32),
                        pltpu.VMEM((2 * N_KV, ACC_ROWS, GROUP * t), F32),
                        pltpu.VMEM((N_KV, HEAD_DIM, GROUP * t), F32)],
        compiler_params=pltpu.CompilerParams(dimension_semantics=("parallel", "arbitrary"),
                                             vmem_limit_bytes=VMEM_LIMIT),
        name="nsa",
    )(qnt, kcmp, vcmpt, ks, vst, kw, vwt, fgt, bg_t, ovl_t)


def _post_kernel(x_ref, of_ref, on_ref, gf_ref, gn_ref, wo_ref, gm_ref, wu_ref, wd_ref, gl_ref, o_ref):
    yf = _rms(of_ref[...].astype(F32), gf_ref[...]).astype(BF16)
    yn = _rms(on_ref[...].astype(F32), gn_ref[...]).astype(BF16)
    h1 = x_ref[...] + _dot(yf, wo_ref[0:WIDTH, :]) + _dot(yn, wo_ref[WIDTH:2 * WIDTH, :])
    u = _dot(_rms(h1, gm_ref[...]).astype(BF16), wu_ref[...])
    act = jnp.square(jnp.maximum(u, 0.0)).astype(BF16)
    h2 = h1 + _dot(act, wd_ref[...])
    o_ref[...] = _rms(h2, gl_ref[...])


def _post(x2, ofox, onsa, g_fox, g_nsa, w_out, g_mlp, w_up, w_down, g_final, tm):
    m, d = x2.shape
    row = lambda i: (i, 0)
    fixed = lambda i: (0, 0)
    full = lambda a: pl.BlockSpec(a.shape, fixed, pipeline_mode=pl.Buffered(1))
    return pl.pallas_call(
        _post_kernel,
        grid=(m // tm,),
        in_specs=[pl.BlockSpec((tm, d), row), pl.BlockSpec((tm, WIDTH), row), pl.BlockSpec((tm, WIDTH), row),
                  full(g_fox), full(g_nsa), full(w_out), full(g_mlp), full(w_up), full(w_down), full(g_final)],
        out_specs=pl.BlockSpec((tm, d), row),
        out_shape=jax.ShapeDtypeStruct((m, d), F32),
        compiler_params=pltpu.CompilerParams(dimension_semantics=("parallel",), vmem_limit_bytes=VMEM_LIMIT),
        name="post",
    )(x2, ofox, onsa, g_fox, g_nsa, w_out, g_mlp, w_up, w_down, g_final)


_NSA_PERM = np.array([(p % 2) * GROUP + p // 2 for p in range(N_HEADS)])


def _perm_heads(a, axis):
    shape = a.shape
    a = a.reshape(shape[:axis] + (N_HEADS, HEAD_DIM) + shape[axis + 1:])
    a = jnp.take(a, _NSA_PERM, axis=axis)
    return a.reshape(shape)


def _prep_w_in(w):
    qscale = HEAD_DIM ** -0.5 * LOG2E
    sizes = [WIDTH, WIDTH, WIDTH, N_HEADS, WIDTH] + [KV_WIDTH] * 6 + [3 * N_HEADS]
    offs = np.cumsum([0] + sizes)
    qf, kf, vf, fl, qn, kc, vc, ks, vs, kw, vw, gate = [w[:, offs[k]:offs[k + 1]] for k in range(len(sizes))]
    fg = jnp.concatenate([fl, gate, jnp.zeros((w.shape[0], LANES - 4 * N_HEADS), w.dtype)], axis=1)
    w_nat = jnp.concatenate([kf, kc, vc, ks, kw], axis=1).astype(BF16)
    w_tr = jnp.concatenate([qf * qscale, vf, _perm_heads(qn * qscale, 1), vs, vw, fg], axis=1).T.astype(BF16)
    assert w_nat.shape[1] == NAT_COLS and w_tr.shape[0] == TR_ROWS
    return w_nat, w_tr


def _prep_compress(pe, w1, b1, w2, b2):
    half = CMP_LEN // 2
    eye = jnp.eye(N_KV, dtype=F32)

    def expand_w1(wpart):
        w3 = wpart.reshape(half, HEAD_DIM, CMP_HIDDEN)
        return jnp.einsum('ldh,gk->lgdkh', w3, eye).reshape(half * KV_WIDTH, N_KV * CMP_HIDDEN).astype(BF16)

    def expand_pe(ppart):
        flat = jnp.broadcast_to(ppart[:, None, :], (half, N_KV, HEAD_DIM)).reshape(1, half * KV_WIDTH)
        return jnp.broadcast_to(flat, (8, half * KV_WIDTH))

    w2b = jnp.einsum('hd,gk->ghkd', w2, eye).reshape(N_KV * CMP_HIDDEN, KV_WIDTH).astype(BF16)
    return (expand_pe(pe[:half]), expand_pe(pe[half:]),
            expand_w1(w1[:half * HEAD_DIM]), expand_w1(w1[half * HEAD_DIM:]),
            jnp.tile(b1, N_KV)[None, :], w2b, jnp.tile(b2, N_KV)[None, :])


def _rope_tables(pos, reps):
    half = HEAD_DIM // 2
    inv = ROPE_THETA ** (-np.arange(half, dtype=np.float64) / half)
    ang = np.asarray(pos, np.float64)[:, None] * inv[None, :]
    cos = np.cos(ang)
    sin = np.sin(ang)
    return (np.tile(np.concatenate([cos, cos], axis=1), (1, reps)).astype(np.float32),
            np.tile(np.concatenate([-sin, sin], axis=1), (1, reps)).astype(np.float32))


def _overlap_t(n_cmp_slots, n_cmp, n_blocks):
    c = np.arange(n_cmp_slots)[None, :] * CMP_STRIDE
    j = np.arange(SEL_SLOTS)[:, None]
    s = j * SEL_LEN
    ovl = (c < s + SEL_LEN) & (c + CMP_LEN > s) & (np.arange(n_cmp_slots)[None, :] < n_cmp) & (j < n_blocks)
    return jnp.asarray(ovl.astype(np.float32), BF16)


def _row_tile(m, want):
    t = want
    while m % t:
        t //= 2
    return t


def _layer(h, seq, g_attn, w_in, b_f, b_gate, cmpk, cmpv, g_fox, g_nsa, w_out, g_mlp, w_up, w_down, g_out):
    m, d = h.shape
    b = m // seq
    n_chunks = seq // CMP_STRIDE
    n_cmp = (seq - CMP_LEN) // CMP_STRIDE + 1
    n_blocks = seq // SEL_LEN
    assert seq % FOX_TILE == 0 and seq % NSA_TILE == 0 and n_chunks % LANES == 0
    assert SEL_TOPK <= n_blocks <= SEL_SLOTS
    assert n_cmp == n_chunks - 1

    cos_n, sin_n = _rope_tables(np.arange(seq), N_KV)
    cos_c, sin_c = _rope_tables(np.arange(n_chunks) * CMP_STRIDE + CMP_LEN - 1, N_KV)
    cos_t = np.ascontiguousarray(cos_n.T)
    sin_t = np.ascontiguousarray(sin_n.T)

    tm = _row_tile(seq, 512)
    w_nat, w_tr = _prep_w_in(w_in)
    kf, kc, vc, ks, kw, qft, vft, qnt, vst, vwt, fgt = _inproj(
        h, g_attn[None, :], w_nat, w_tr, cos_n, sin_n, cos_t, sin_t, seq, tm)
    r3 = lambda a: a.reshape(b, seq, a.shape[-1])

    bf_pad = jnp.zeros((1, LANES), F32).at[0, :N_HEADS].set(b_f)
    bg_col = jnp.zeros((LANES,), F32).at[N_HEADS:4 * N_HEADS].set(b_gate)
    bg_t = jnp.broadcast_to(bg_col[:, None], (LANES, NSA_TILE))
    crow, kaug = _cumgate(fgt, bf_pad, r3(kf))

    kcmp, vcmpt = _compress(r3(kc), r3(vc), cos_c, sin_c, _prep_compress(*cmpk), _prep_compress(*cmpv))

    ofox = _fox(qft, kaug, vft, crow)
    onsa = _nsa(qnt, kcmp, vcmpt, r3(ks), vst, r3(kw), vwt, fgt, bg_t,
                _overlap_t(n_chunks, n_cmp, n_blocks), min(SEL_TOPK, n_blocks))

    w_out_p = jnp.concatenate([w_out[:WIDTH], _perm_heads(w_out[WIDTH:], 0)], axis=0).astype(BF16)
    return _post(h, ofox.reshape(m, WIDTH), onsa.reshape(m, WIDTH), g_fox[None, :], _perm_heads(g_nsa, 0)[None, :],
                 w_out_p, g_mlp[None, :], w_up.astype(BF16), w_down.astype(BF16), g_out[None, :],
                 _row_tile(seq, 512))


def kernel(x, g_attn, w_in, b_f, b_gate, cmpk_pe, cmpk_w1, cmpk_b1, cmpk_w2, cmpk_b2, cmpv_pe, cmpv_w1, cmpv_b1,
           cmpv_w2, cmpv_b2, g_fox, g_nsa, w_out, g_mlp, w_up, w_down, g_final):
    b, seq, d = x.shape
    depth = g_attn.shape[0]
    assert depth == 1, "the final rmsnorm is fused into the (single) layer's last kernel"
    h = x.reshape(b * seq, d)
    out = _layer(h, seq, g_attn[0], w_in[0], b_f[0], b_gate[0],
                 (cmpk_pe[0], cmpk_w1[0], cmpk_b1[0], cmpk_w2[0], cmpk_b2[0]),
                 (cmpv_pe[0], cmpv_w1[0], cmpv_b1[0], cmpv_w2[0], cmpv_b2[0]),
                 g_fox[0], g_nsa[0], w_out[0], g_mlp[0], w_up[0], w_down[0], g_final)
    return out.reshape(b, seq, d)
```

```python
import functools

import numpy as np
import jax
import jax.numpy as jnp
from jax import lax
from jax.experimental import pallas as pl
from jax.experimental.pallas import tpu as pltpu

F32 = jnp.float32
BF16 = jnp.bfloat16

HEAD_DIM = 64
N_HEADS = 8
N_KV = 2
GROUP = N_HEADS // N_KV
WIDTH = N_HEADS * HEAD_DIM
KV_WIDTH = N_KV * HEAD_DIM
CMP_LEN = 32
CMP_STRIDE = 16
CMP_HIDDEN = 256
SEL_LEN = 64
SEL_SHIFT = 6
SEL_TOPK = 16
SEL_SLOTS = 64
WINDOW = 512
ROPE_THETA = 10000.0
EPS = 1e-6
NEG = -1e30
MASK_FILL = -(2.0 ** 100)
FORCE_BONUS = 1e4
LOG2E = 1.4426950408889634
LANES = 128
HALF = LANES // 2
BF16_ROWS = 16

VMEM_LIMIT = 56 * 1024 * 1024


def _dot(a, b, precision=None):
    return jnp.dot(a, b, preferred_element_type=F32, precision=precision)


def _dot_nt(a, b, precision=None):
    return lax.dot_general(a, b, (((1,), (1,)), ((), ())), preferred_element_type=F32, precision=precision)


def _rms(x, g):
    return x * lax.rsqrt(jnp.mean(x * x, axis=-1, keepdims=True) + EPS) * g


def _rope(x, cos, sin_signed):
    w = x.shape[-1]
    lane = lax.broadcasted_iota(jnp.int32, (1, w), 1)
    first = (lane % HEAD_DIM) < (HEAD_DIM // 2)
    partner = jnp.where(first, pltpu.roll(x, w - HEAD_DIM // 2, 1), pltpu.roll(x, HEAD_DIM // 2, 1))
    return x * cos + partner * sin_signed


def _rope_t(x, cos, sin_signed):
    r = x.shape[0]
    row = lax.broadcasted_iota(jnp.int32, (r, 1), 0)
    first = (row % HEAD_DIM) < (HEAD_DIM // 2)
    partner = jnp.where(first, pltpu.roll(x, r - HEAD_DIM // 2, 0), pltpu.roll(x, HEAD_DIM // 2, 0))
    return x * cos + partner * sin_signed


def _split3(c):
    c1 = c.astype(BF16).astype(F32)
    r1 = c - c1
    c2 = r1.astype(BF16).astype(F32)
    return c1, c2, r1 - c2


NAT_KF, NAT_KC, NAT_VC, NAT_KS, NAT_KW, NAT_COLS = 0, 512, 640, 768, 896, 1024
TR_QF, TR_VF, TR_QN, TR_VS, TR_VW, TR_FG, TR_ROWS = 0, 512, 1024, 1536, 1664, 1792, 1920


def _inproj_kernel(x_ref, g_ref, wn_ref, wt_ref, cos_ref, sin_ref, cost_ref, sint_ref,
                   kf_ref, kc_ref, vc_ref, ks_ref, kw_ref,
                   qft_ref, vft_ref, qnt_ref, vst_ref, vwt_ref, fgt_ref):
    hb = _rms(x_ref[...], g_ref[...]).astype(BF16)

    def nat(lo, width):
        return _dot(hb, wn_ref[:, lo:lo + width])

    def tr(lo, rows):
        return _dot_nt(wt_ref[lo:lo + rows, :], hb)

    cos = cos_ref[...]
    sin = sin_ref[...]
    kf_ref[...] = nat(NAT_KF, WIDTH).astype(BF16)
    kc_ref[...] = nat(NAT_KC, KV_WIDTH)
    vc_ref[...] = nat(NAT_VC, KV_WIDTH)
    ks_ref[...] = _rope(nat(NAT_KS, KV_WIDTH), cos, sin).astype(BF16)
    kw_ref[...] = _rope(nat(NAT_KW, KV_WIDTH), cos, sin).astype(BF16)
    qft_ref[...] = tr(TR_QF, WIDTH).astype(BF16)
    vft_ref[...] = tr(TR_VF, WIDTH).astype(BF16)
    pairs = WIDTH // KV_WIDTH
    cos_rows = jnp.concatenate([cost_ref[...]] * pairs, axis=0)
    sin_rows = jnp.concatenate([sint_ref[...]] * pairs, axis=0)
    qnt_ref[...] = _rope_t(tr(TR_QN, WIDTH), cos_rows, sin_rows).astype(BF16)
    vst_ref[...] = tr(TR_VS, KV_WIDTH).astype(BF16)
    vwt_ref[...] = tr(TR_VW, KV_WIDTH).astype(BF16)
    fgt_ref[...] = tr(TR_FG, LANES)


def _inproj(x2, g_attn, w_nat, w_tr, cos_n, sin_n, cos_t, sin_t, seq, tm):
    m, d = x2.shape
    sblocks = seq // tm
    row = lambda i: (i, 0)
    col = lambda i: (0, i)
    fixed = lambda i: (0, 0)
    tab = lambda i: (i % sblocks, 0)
    tab_t = lambda i: (0, i % sblocks)
    sds = jax.ShapeDtypeStruct
    out_shape = [sds((m, WIDTH), BF16)] + [sds((m, KV_WIDTH), F32)] * 2 + [sds((m, KV_WIDTH), BF16)] * 2 + [
                 sds((WIDTH, m), BF16), sds((WIDTH, m), BF16), sds((WIDTH, m), BF16),
                 sds((KV_WIDTH, m), BF16), sds((KV_WIDTH, m), BF16), sds((LANES, m), F32)]
    out_specs = ([pl.BlockSpec((tm, WIDTH), row)] + [pl.BlockSpec((tm, KV_WIDTH), row)] * 4
                 + [pl.BlockSpec((WIDTH, tm), col)] * 3 + [pl.BlockSpec((KV_WIDTH, tm), col)] * 2
                 + [pl.BlockSpec((LANES, tm), col)])
    return pl.pallas_call(
        _inproj_kernel,
        grid=(m // tm,),
        in_specs=[pl.BlockSpec((tm, d), row), pl.BlockSpec((1, d), fixed),
                  pl.BlockSpec(w_nat.shape, fixed), pl.BlockSpec(w_tr.shape, fixed),
                  pl.BlockSpec((tm, KV_WIDTH), tab), pl.BlockSpec((tm, KV_WIDTH), tab),
                  pl.BlockSpec((KV_WIDTH, tm), tab_t), pl.BlockSpec((KV_WIDTH, tm), tab_t)],
        out_specs=out_specs,
        out_shape=out_shape,
        compiler_params=pltpu.CompilerParams(dimension_semantics=("parallel",), vmem_limit_bytes=VMEM_LIMIT),
        name="inproj",
    )(x2, g_attn, w_nat, w_tr, cos_n, sin_n, cos_t, sin_t)


CUM_BLOCK = 256
CUM_UNROLL = 8
N_BIAS = 3


def _bias_lane_base(h):
    return HALF if h % 2 == 0 else 0


def _bias_placement():
    p = np.zeros((LANES, N_HEADS * LANES), np.float32)
    for h in range(N_HEADS):
        base = h * LANES + _bias_lane_base(h)
        for part in range(N_BIAS):
            p[part * N_HEADS + h, base + N_BIAS + part] = -1.0
            p[N_BIAS * N_HEADS, base + part] = 1.0
    return jnp.asarray(p, BF16)


def _cumgate_kernel(fgt_ref, bf_ref, kf_ref, place_ref, crow_ref, kaug_ref):
    seq = fgt_ref.shape[1]
    r = lax.broadcasted_iota(jnp.int32, (CUM_BLOCK, CUM_BLOCK), 0)
    c = lax.broadcasted_iota(jnp.int32, (CUM_BLOCK, CUM_BLOCK), 1)
    tri = jnp.where(r >= c, 1.0, 0.0).astype(BF16)
    er = lax.broadcasted_iota(jnp.int32, (N_HEADS, LANES), 0)
    ec = lax.broadcasted_iota(jnp.int32, (N_HEADS, LANES), 1)
    pick = jnp.where(er == ec, 1.0, 0.0).astype(BF16)
    bias = bf_ref[...]
    lane = lax.broadcasted_iota(jnp.int32, (CUM_BLOCK, LANES), 1)

    def body(blk, carry):
        off = pl.multiple_of(blk * CUM_BLOCK, CUM_BLOCK)
        z = fgt_ref[:, pl.ds(off, CUM_BLOCK)].T + bias
        logf = jnp.minimum(z, 0.0) - jnp.log(1.0 + jnp.exp(-jnp.abs(z)))
        cs = sum(_dot(tri, part.astype(BF16)) for part in _split3(logf)) + carry
        c2 = cs * LOG2E
        c1, cb, cc = _split3(jnp.where(lane < N_HEADS, c2, 0.0))
        crow_ref[0, :, pl.ds(off, CUM_BLOCK)] = sum(_dot_nt(pick, part.astype(BF16)) for part in (c1, cb, cc))
        parts = (c1 + pltpu.roll(cb, N_HEADS, 1) + pltpu.roll(cc, 2 * N_HEADS, 1)
                 + jnp.where(lane == N_BIAS * N_HEADS, 1.0, 0.0))
        feat = _dot(parts.astype(BF16), place_ref[...]).astype(BF16)
        for h in range(N_HEADS):
            mine = (lane < HALF) if h % 2 == 0 else (lane >= HALF)
            kcol = kf_ref[0, pl.ds(off, CUM_BLOCK), (h // 2) * LANES:(h // 2 + 1) * LANES]
            kaug_ref[0, h, pl.ds(off, CUM_BLOCK), :] = jnp.where(mine, kcol, feat[:, h * LANES:(h + 1) * LANES])
        return cs[CUM_BLOCK - 1:CUM_BLOCK, :]

    lax.fori_loop(0, seq // CUM_BLOCK, body, jnp.zeros((1, LANES), F32), unroll=CUM_UNROLL)


def _cumgate(fgt, bf_pad, kf3):
    b, seq, _ = kf3.shape
    place = _bias_placement()
    return pl.pallas_call(
        _cumgate_kernel,
        grid=(b,),
        in_specs=[pl.BlockSpec((LANES, seq), lambda i: (0, i)), pl.BlockSpec((1, LANES), lambda i: (0, 0)),
                  pl.BlockSpec((1, seq, WIDTH), lambda i: (i, 0, 0)), pl.BlockSpec(place.shape, lambda i: (0, 0))],
        out_specs=[pl.BlockSpec((1, N_HEADS, seq), lambda i: (i, 0, 0)),
                   pl.BlockSpec((1, N_HEADS, seq, LANES), lambda i: (i, 0, 0, 0))],
        out_shape=[jax.ShapeDtypeStruct((b, N_HEADS, seq), F32),
                   jax.ShapeDtypeStruct((b, N_HEADS, seq, LANES), BF16)],
        compiler_params=pltpu.CompilerParams(dimension_semantics=("parallel",), vmem_limit_bytes=VMEM_LIMIT),
        name="cumgate",
    )(fgt, bf_pad, kf3, place)


def _compress_one(x_ref, pea_ref, peb_ref, wa_ref, wb_ref, b1_ref, w2_ref, b2_ref):
    n = x_ref.shape[1] // CMP_STRIDE
    xk = jnp.concatenate([x_ref[0, pl.ds(l, n, stride=CMP_STRIDE), :].astype(BF16) for l in range(CMP_STRIDE)],
                         axis=1)
    wa = wa_ref[...]
    wb = wb_ref[...]
    first = _dot(xk, wa)
    second = _dot(xk, wb)
    pe_term = _dot(pea_ref[...].astype(BF16), wa) + _dot(peb_ref[...].astype(BF16), wb)
    hidden = first + pltpu.roll(second, n - 1, 0) + pe_term[0:1, :] + b1_ref[...]
    act = jax.nn.gelu(hidden)
    return _dot(act.astype(BF16), w2_ref[...]) + b2_ref[...]


def _compress_kernel(kc_ref, vc_ref, cos_ref, sin_ref,
                     kpea, kpeb, kwa, kwb, kb1, kw2, kb2,
                     vpea, vpeb, vwa, vwb, vb1, vw2, vb2,
                     kcmp_ref, vcmpt_ref):
    kcmp = _compress_one(kc_ref, kpea, kpeb, kwa, kwb, kb1, kw2, kb2)
    kcmp_ref[0] = _rope(kcmp, cos_ref[...], sin_ref[...]).astype(BF16)
    vcmpt_ref[0] = _compress_one(vc_ref, vpea, vpeb, vwa, vwb, vb1, vw2, vb2).T.astype(BF16)


def _compress(kc3, vc3, cos_c, sin_c, kparams, vparams):
    b, seq, w = kc3.shape
    n = seq // CMP_STRIDE
    x_spec = pl.BlockSpec((1, seq, w), lambda i: (i, 0, 0))
    full = lambda a: pl.BlockSpec(a.shape, lambda i: (0,) * a.ndim)
    params = list(kparams) + list(vparams)
    return pl.pallas_call(
        _compress_kernel,
        grid=(b,),
        in_specs=[x_spec, x_spec, full(cos_c), full(sin_c)] + [full(p) for p in params],
        out_specs=[pl.BlockSpec((1, n, KV_WIDTH), lambda i: (i, 0, 0)),
                   pl.BlockSpec((1, KV_WIDTH, n), lambda i: (i, 0, 0))],
        out_shape=[jax.ShapeDtypeStruct((b, n, KV_WIDTH), BF16), jax.ShapeDtypeStruct((b, KV_WIDTH, n), BF16)],
        compiler_params=pltpu.CompilerParams(dimension_semantics=("parallel",), vmem_limit_bytes=VMEM_LIMIT),
        name="compress",
    )(kc3, vc3, cos_c, sin_c, *params)


ACC_ROWS = HEAD_DIM + BF16_ROWS


def _flash_step_t(s, vt, m_ref, acc_ref, idx):
    row = slice(idx, idx + 1)
    m_old = m_ref[row, :]
    m_new = jnp.maximum(m_old, jnp.max(s, axis=0, keepdims=True))
    alpha = jnp.exp2(m_old - m_new)
    p = jnp.exp2((s - m_new).astype(BF16))
    m_ref[row, :] = m_new
    vt_ones = jnp.concatenate([vt, jnp.ones((BF16_ROWS, vt.shape[1]), BF16)], axis=0)
    acc_ref[idx] = alpha * acc_ref[idx] + _dot(vt_ones, p)


def _flash_reset(m_ref, acc_ref):
    m_ref[...] = jnp.full(m_ref.shape, NEG, F32)
    acc_ref[...] = jnp.zeros(acc_ref.shape, F32)


def _flash_finish(acc_ref, idx):
    acc = acc_ref[idx]
    return acc[0:HEAD_DIM] * (1.0 / acc[HEAD_DIM:HEAD_DIM + 1])


def _for_tiles(n, unroll, visit):
    groups = n // unroll

    def body(jj, carry):
        visit(jj * unroll, unroll)
        return carry

    lax.fori_loop(0, groups, body, 0)
    done = groups * unroll
    rem = n - done
    p = unroll // 2
    while p >= 1:
        first = done + (rem & ~(2 * p - 1))

        @pl.when((rem & p) != 0)
        def _(first=first, p=p):
            visit(first, p)

        p //= 2


FOX_TILE = 256
FOX_KEYS = 256
FOX_AHEAD = 5
FOX_UNROLL = 8


def _fox_kernel(qt_ref, kaug_ref, vt_ref, crow_ref, o_ref, qaug_ref, m_ref, acc_ref):
    t = FOX_TILE
    tk = FOX_KEYS
    assert t == tk
    i = pl.program_id(1)
    kr = lax.broadcasted_iota(jnp.int32, (tk, t), 0)
    qc = lax.broadcasted_iota(jnp.int32, (tk, t), 1)
    frow = lax.broadcasted_iota(jnp.int32, (BF16_ROWS, t), 0)
    pad = jnp.zeros((HALF - BF16_ROWS, t), BF16)

    _flash_reset(m_ref, acc_ref)
    for h in range(N_HEADS):
        c1, c2, c3 = _split3(crow_ref[0, h:h + 1, :])
        feat = jnp.where(frow == 0, c1, jnp.where(frow == 1, c2, jnp.where(frow == 2, c3,
               jnp.where(frow < 2 * N_BIAS, 1.0, 0.0)))).astype(BF16)
        qh = qt_ref[h * HEAD_DIM:(h + 1) * HEAD_DIM, :]
        parts = [qh, feat, pad] if _bias_lane_base(h) == HALF else [feat, pad, qh]
        qaug_ref[h] = jnp.concatenate(parts, axis=0)

    def step(tiles):
        chains = [(off, mask, h) for off, mask in tiles for h in range(N_HEADS)]

        def scores(c):
            off, _, h = chains[c]
            return _dot(kaug_ref[0, h, pl.ds(off, tk), :], qaug_ref[h])

        pending = {c: scores(c) for c in range(FOX_AHEAD)}
        for c, (off, mask, h) in enumerate(chains):
            if c + FOX_AHEAD < len(chains):
                pending[c + FOX_AHEAD] = scores(c + FOX_AHEAD)
            s = pending.pop(c)
            if mask is not None:
                s = jnp.where(mask, s, MASK_FILL)
            _flash_step_t(s, vt_ref[h * HEAD_DIM:(h + 1) * HEAD_DIM, pl.ds(off, tk)], m_ref, acc_ref, h)

    def tile_off(j):
        return pl.multiple_of(j * tk, tk)

    groups = i // FOX_UNROLL

    def body(jj, carry):
        step([(tile_off(jj * FOX_UNROLL + u), None) for u in range(FOX_UNROLL)])
        return carry

    lax.fori_loop(0, groups, body, 0)
    done = groups * FOX_UNROLL
    for left in range(FOX_UNROLL):
        @pl.when(i - done == left)
        def _(left=left):
            step([(tile_off(done + u), None) for u in range(left)] + [(tile_off(i), kr <= qc)])

    for col in range(WIDTH // LANES):
        pair = [_flash_finish(acc_ref, h) for h in (2 * col, 2 * col + 1)]
        o_ref[0, :, col * LANES:(col + 1) * LANES] = jnp.concatenate(pair, axis=0).T.astype(BF16)


def _fox(qft, kaug, vft, crow):
    b, _, seq, _ = kaug.shape
    t = FOX_TILE
    nq = seq // t
    return pl.pallas_call(
        _fox_kernel,
        grid=(b, nq),
        in_specs=[pl.BlockSpec((WIDTH, t), lambda bi, i: (0, bi * nq + i)),
                  pl.BlockSpec((1, N_HEADS, seq, LANES), lambda bi, i: (bi, 0, 0, 0)),
                  pl.BlockSpec((WIDTH, seq), lambda bi, i: (0, bi)),
                  pl.BlockSpec((1, N_HEADS, t), lambda bi, i: (bi, 0, i))],
        out_specs=pl.BlockSpec((1, t, WIDTH), lambda bi, i: (bi, i, 0)),
        out_shape=jax.ShapeDtypeStruct((b, seq, WIDTH), BF16),
        scratch_shapes=[pltpu.VMEM((N_HEADS, LANES, t), BF16), pltpu.VMEM((N_HEADS, t), F32),
                        pltpu.VMEM((N_HEADS, ACC_ROWS, t), F32)],
        compiler_params=pltpu.CompilerParams(dimension_semantics=("parallel", "arbitrary"),
                                             vmem_limit_bytes=VMEM_LIMIT),
        name="fox",
    )(qft, kaug, vft, crow)


NSA_TILE = 256
NSA_KEYS = 128
STAT_ROWS = 8
NSA_UNROLL = 8
NSA_AHEAD = 4


def _nsa_kernel(qt_ref, kcmp_ref, vcmpt_ref, ks_ref, vst_ref, kw_ref, vwt_ref, fgt_ref, bg_ref, ovl_ref, o_ref,
                qaug_ref, m_ref, acc_ref, ocmp_ref, *, topk):
    t = NSA_TILE
    tk = NSA_KEYS
    per_q = t // tk
    wide = GROUP * t
    n_cmp = kcmp_ref.shape[1]
    i = pl.program_id(1)
    start = i * t
    lane = lax.broadcasted_iota(jnp.int32, (1, LANES), 1)
    mine = [lane < HALF, lane >= HALF]
    zero_bf = jnp.zeros((), BF16)

    q4 = jnp.concatenate([qt_ref[n * LANES:(n + 1) * LANES, :] for n in range(GROUP)], axis=1)
    qpos4 = start + (lax.broadcasted_iota(jnp.int32, (1, wide), 1) & (t - 1))
    cur = (start + lax.broadcasted_iota(jnp.int32, (1, t), 1)) >> SEL_SHIFT
    sub8 = lax.broadcasted_iota(jnp.int32, (STAT_ROWS, t), 0)

    def cmp_rank(n_rows, n_slots):
        cmp_end = lax.broadcasted_iota(jnp.int32, (n_rows, 1), 0) * CMP_STRIDE + (CMP_LEN - 1)
        vis = cmp_end <= qpos4
        slot = lax.broadcasted_iota(jnp.int32, (n_slots, t), 0)
        valid = slot <= cur
        forced = (slot == 0) | (slot == cur) | (slot == cur - 1)
        for g in range(N_KV):
            s = jnp.where(vis, _dot(jnp.where(mine[g], kcmp_ref[0, 0:n_rows, :], zero_bf), q4), NEG)
            e = jnp.where(vis, jnp.exp2(s - jnp.max(s, axis=0, keepdims=True)), 0.0)
            denom = jnp.sum(e, axis=0, keepdims=True)
            p = e * jnp.where(denom > 0.0, 1.0 / denom, 0.0)
            ocmp_ref[g] = _dot(vcmpt_ref[0, g * HEAD_DIM:(g + 1) * HEAD_DIM, 0:n_rows], p.astype(BF16))
            psum = p[:, 0:t]
            for n in range(1, GROUP):
                psum = psum + p[:, n * t:(n + 1) * t]
            ovl = ovl_ref[0:n_slots, 0:n_rows]
            imp = sum(_dot(ovl, part.astype(BF16)) for part in _split3(psum))
            score = jnp.where(valid, imp + jnp.where(forced, FORCE_BONUS, 0.0), -1.0)
            tiles = [score[r:r + STAT_ROWS, :] for r in range(0, n_slots, STAT_ROWS)]
            ranks = [jnp.zeros((STAT_ROWS, t), F32) for _ in tiles]
            for jp in range(n_slots):
                other = jnp.broadcast_to(score[jp:jp + 1, :], (STAT_ROWS, t))
                for k, tile in enumerate(tiles):
                    first = k * STAT_ROWS
                    ge = jnp.where(other >= tile, 1.0, 0.0)
                    gt = jnp.where(other > tile, 1.0, 0.0)
                    if first > jp:
                        ahead = ge
                    elif first + STAT_ROWS - 1 <= jp:
                        ahead = gt
                    else:
                        ahead = jnp.where(sub8 + first > jp, ge, gt)
                    ranks[k] = ranks[k] + ahead
            rank = jnp.concatenate(ranks, axis=0)
            selb = jnp.where(rank < topk, 0.0, MASK_FILL).astype(BF16)
            if n_slots < SEL_SLOTS:
                selb = jnp.concatenate([selb, jnp.full((SEL_SLOTS - n_slots, t), MASK_FILL, BF16)], axis=0)
            selb4 = jnp.concatenate([selb] * GROUP, axis=1)
            qaug_ref[g] = jnp.concatenate([q4[0:HALF], selb4] if g == 0 else [selb4, q4[HALF:LANES]], axis=0)

    slots_per_tile = t // SEL_LEN
    lo = 0
    for n_slots in range(SEL_SLOTS // 4, SEL_SLOTS + 1, SEL_SLOTS // 4):
        hi = n_slots // slots_per_tile
        n_rows = min(n_cmp, -(-(hi * t // CMP_STRIDE) // LANES) * LANES)
        last = n_slots == SEL_SLOTS

        @pl.when((i >= lo) if last else ((i >= lo) & (i < hi)))
        def _(n_rows=n_rows, n_slots=n_slots):
            cmp_rank(n_rows, n_slots)

        lo = hi
    o_cmp = [ocmp_ref[g] for g in range(N_KV)]

    kr = lax.broadcasted_iota(jnp.int32, (tk, wide), 0)
    q_local = lax.broadcasted_iota(jnp.int32, (tk, wide), 1) & (t - 1)
    key_row = lax.broadcasted_iota(jnp.int32, (tk, LANES), 0)
    lane_full = lax.broadcasted_iota(jnp.int32, (tk, LANES), 1)
    slot_minus_row = [lane_full - HALF - (key_row >> SEL_SHIFT), lane_full - (key_row >> SEL_SHIFT)]

    def sel_score(off, g):
        onehot = jnp.where(slot_minus_row[g] == (off >> SEL_SHIFT), 1.0, 0.0).astype(BF16)
        return _dot(jnp.where(mine[g], ks_ref[0, pl.ds(off, tk), :], onehot), qaug_ref[g])

    def win_score(off, g):
        return _dot(jnp.where(mine[g], kw_ref[0, pl.ds(off, tk), :], zero_bf), q4)

    def sel_chains(tiles):
        return [(sel_score, vst_ref, g, g, off, mask) for off, mask in tiles for g in range(N_KV)]

    def win_chains(tiles):
        return [(win_score, vwt_ref, g, N_KV + g, off, mask) for off, mask in tiles for g in range(N_KV)]

    def sweep(chains):
        def issue(c):
            score, _, g, _, off, _ = chains[c]
            return score(off, g)

        pending = {c: issue(c) for c in range(min(NSA_AHEAD, len(chains)))}
        for c, (_, vt_ref_, g, slot_id, off, mask) in enumerate(chains):
            if c + NSA_AHEAD < len(chains):
                pending[c + NSA_AHEAD] = issue(c + NSA_AHEAD)
            s = pending.pop(c)
            if mask is not None:
                s = jnp.where(mask, s, MASK_FILL)
            _flash_step_t(s, vt_ref_[g * HEAD_DIM:(g + 1) * HEAD_DIM, pl.ds(off, tk)], m_ref, acc_ref, slot_id)

    def tile_off(j):
        return pl.multiple_of(j * tk, tk)

    n_before = i * per_q
    n_back = WINDOW // tk
    groups = n_before // NSA_UNROLL
    _flash_reset(m_ref, acc_ref)

    def sel_body(jj, carry):
        sweep(sel_chains([(tile_off(jj * NSA_UNROLL + u), None) for u in range(NSA_UNROLL)]))
        return carry

    lax.fori_loop(0, groups, sel_body, 0)
    done = groups * NSA_UNROLL

    own = [(tile_off(n_before + u), kr + u * tk <= q_local) for u in range(per_q)]

    for left in range(0, NSA_UNROLL, per_q):
        @pl.when((n_before >= n_back) & (n_before - done == left))
        def _(left=left):
            back = [(tile_off(n_before - n_back + u), (q_local - kr < u * tk) if u < per_q else None)
                    for u in range(n_back)]
            sweep(sel_chains([(tile_off(done + u), None) for u in range(left)] + own) + win_chains(back + own))

    for early in range(0, n_back, per_q):
        @pl.when(n_before == early)
        def _(early=early):
            before = [(tile_off(u), None) for u in range(early)]
            sweep(sel_chains(before + own) + win_chains(before + own))

    o_slc = [_flash_finish(acc_ref, g) for g in range(N_KV)]
    o_win = [_flash_finish(acc_ref, N_KV + g) for g in range(N_KV)]

    gates = jax.nn.sigmoid(fgt_ref[...] + bg_ref[...])
    for n in range(GROUP):
        cols = slice(n * t, (n + 1) * t)
        mixed = []
        for g in range(N_KV):
            base = N_HEADS + (g * GROUP + n) * 3
            mixed.append(gates[base:base + 1, :] * o_cmp[g][:, cols]
                         + gates[base + 1:base + 2, :] * o_slc[g][:, cols]
                         + gates[base + 2:base + 3, :] * o_win[g][:, cols])
        o_ref[0, :, n * LANES:(n + 1) * LANES] = jnp.concatenate(mixed, axis=0).T.astype(BF16)


def _nsa(qnt, kcmp, vcmpt, ks, vst, kw, vwt, fgt, bg_t, ovl_t, topk):
    b, seq, _ = ks.shape
    t = NSA_TILE
    nq = seq // t
    n_cmp = kcmp.shape[1]
    whole = lambda bi, i: (bi, 0, 0)
    whole_t = lambda bi, i: (0, bi)
    tile_t = lambda bi, i: (0, bi * nq + i)
    fixed = lambda bi, i: (0, 0)
    return pl.pallas_call(
        functools.partial(_nsa_kernel, topk=topk),
        grid=(b, nq),
        in_specs=[pl.BlockSpec((WIDTH, t), tile_t),
                  pl.BlockSpec((1, n_cmp, KV_WIDTH), whole), pl.BlockSpec((1, KV_WIDTH, n_cmp), whole),
                  pl.BlockSpec((1, seq, KV_WIDTH), whole), pl.BlockSpec((KV_WIDTH, seq), whole_t),
                  pl.BlockSpec((1, seq, KV_WIDTH), whole), pl.BlockSpec((KV_WIDTH, seq), whole_t),
                  pl.BlockSpec((LANES, t), tile_t), pl.BlockSpec((LANES, t), fixed),
                  pl.BlockSpec(ovl_t.shape, fixed)],
        out_specs=pl.BlockSpec((1, t, WIDTH), lambda bi, i: (bi, i, 0)),
        out_shape=jax.ShapeDtypeStruct((b, seq, WIDTH), BF16),
        scratch_shapes=[pltpu.VMEM((N_KV, LANES, GROUP * t), BF16), pltpu.VMEM((STAT_ROWS, GROUP * t), F32),
                        pltpu.VMEM((2 * N_KV, ACC_ROWS, GROUP * t), F32),
                        pltpu.VMEM((N_KV, HEAD_DIM, GROUP * t), F32)],
        compiler_params=pltpu.CompilerParams(dimension_semantics=("parallel", "arbitrary"),
                                             vmem_limit_bytes=VMEM_LIMIT),
        name="nsa",
    )(qnt, kcmp, vcmpt, ks, vst, kw, vwt, fgt, bg_t, ovl_t)


def _post_kernel(x_ref, of_ref, on_ref, gf_ref, gn_ref, wo_ref, gm_ref, wu_ref, wd_ref, gl_ref, o_ref):
    yf = _rms(of_ref[...].astype(F32), gf_ref[...]).astype(BF16)
    yn = _rms(on_ref[...].astype(F32), gn_ref[...]).astype(BF16)
    h1 = x_ref[...] + _dot(yf, wo_ref[0:WIDTH, :]) + _dot(yn, wo_ref[WIDTH:2 * WIDTH, :])
    u = _dot(_rms(h1, gm_ref[...]).astype(BF16), wu_ref[...])
    act = jnp.square(jnp.maximum(u, 0.0)).astype(BF16)
    h2 = h1 + _dot(act, wd_ref[...])
    o_ref[...] = _rms(h2, gl_ref[...])


def _post(x2, ofox, onsa, g_fox, g_nsa, w_out, g_mlp, w_up, w_down, g_final, tm):
    m, d = x2.shape
    row = lambda i: (i, 0)
    fixed = lambda i: (0, 0)
    full = lambda a: pl.BlockSpec(a.shape, fixed, pipeline_mode=pl.Buffered(1))
    return pl.pallas_call(
        _post_kernel,
        grid=(m // tm,),
        in_specs=[pl.BlockSpec((tm, d), row), pl.BlockSpec((tm, WIDTH), row), pl.BlockSpec((tm, WIDTH), row),
                  full(g_fox), full(g_nsa), full(w_out), full(g_mlp), full(w_up), full(w_down), full(g_final)],
        out_specs=pl.BlockSpec((tm, d), row),
        out_shape=jax.ShapeDtypeStruct((m, d), F32),
        compiler_params=pltpu.CompilerParams(dimension_semantics=("parallel",), vmem_limit_bytes=VMEM_LIMIT),
        name="post",
    )(x2, ofox, onsa, g_fox, g_nsa, w_out, g_mlp, w_up, w_down, g_final)


_NSA_PERM = np.array([(p % 2) * GROUP + p // 2 for p in range(N_HEADS)])


def _perm_heads(a, axis):
    shape = a.shape
    a = a.reshape(shape[:axis] + (N_HEADS, HEAD_DIM) + shape[axis + 1:])
    a = jnp.take(a, _NSA_PERM, axis=axis)
    return a.reshape(shape)


def _prep_w_in(w):
    qscale = HEAD_DIM ** -0.5 * LOG2E
    sizes = [WIDTH, WIDTH, WIDTH, N_HEADS, WIDTH] + [KV_WIDTH] * 6 + [3 * N_HEADS]
    offs = np.cumsum([0] + sizes)
    qf, kf, vf, fl, qn, kc, vc, ks, vs, kw, vw, gate = [w[:, offs[k]:offs[k + 1]] for k in range(len(sizes))]
    fg = jnp.concatenate([fl, gate, jnp.zeros((w.shape[0], LANES - 4 * N_HEADS), w.dtype)], axis=1)
    w_nat = jnp.concatenate([kf, kc, vc, ks, kw], axis=1).astype(BF16)
    w_tr = jnp.concatenate([qf * qscale, vf, _perm_heads(qn * qscale, 1), vs, vw, fg], axis=1).T.astype(BF16)
    assert w_nat.shape[1] == NAT_COLS and w_tr.shape[0] == TR_ROWS
    return w_nat, w_tr


def _prep_compress(pe, w1, b1, w2, b2):
    half = CMP_LEN // 2
    eye = jnp.eye(N_KV, dtype=F32)

    def expand_w1(wpart):
        w3 = wpart.reshape(half, HEAD_DIM, CMP_HIDDEN)
        return jnp.einsum('ldh,gk->lgdkh', w3, eye).reshape(half * KV_WIDTH, N_KV * CMP_HIDDEN).astype(BF16)

    def expand_pe(ppart):
        flat = jnp.broadcast_to(ppart[:, None, :], (half, N_KV, HEAD_DIM)).reshape(1, half * KV_WIDTH)
        return jnp.broadcast_to(flat, (8, half * KV_WIDTH))

    w2b = jnp.einsum('hd,gk->ghkd', w2, eye).reshape(N_KV * CMP_HIDDEN, KV_WIDTH).astype(BF16)
    return (expand_pe(pe[:half]), expand_pe(pe[half:]),
            expand_w1(w1[:half * HEAD_DIM]), expand_w1(w1[half * HEAD_DIM:]),
            jnp.tile(b1, N_KV)[None, :], w2b, jnp.tile(b2, N_KV)[None, :])


def _rope_tables(pos, reps):
    half = HEAD_DIM // 2
    inv = ROPE_THETA ** (-np.arange(half, dtype=np.float64) / half)
    ang = np.asarray(pos, np.float64)[:, None] * inv[None, :]
    cos = np.cos(ang)
    sin = np.sin(ang)
    return (np.tile(np.concatenate([cos, cos], axis=1), (1, reps)).astype(np.float32),
            np.tile(np.concatenate([-sin, sin], axis=1), (1, reps)).astype(np.float32))


def _overlap_t(n_cmp_slots, n_cmp, n_blocks):
    c = np.arange(n_cmp_slots)[None, :] * CMP_STRIDE
    j = np.arange(SEL_SLOTS)[:, None]
    s = j * SEL_LEN
    ovl = (c < s + SEL_LEN) & (c + CMP_LEN > s) & (np.arange(n_cmp_slots)[None, :] < n_cmp) & (j < n_blocks)
    return jnp.asarray(ovl.astype(np.float32), BF16)


def _row_tile(m, want):
    t = want
    while m % t:
        t //= 2
    return t


def _layer(h, seq, g_attn, w_in, b_f, b_gate, cmpk, cmpv, g_fox, g_nsa, w_out, g_mlp, w_up, w_down, g_out):
    m, d = h.shape
    b = m // seq
    n_chunks = seq // CMP_STRIDE
    n_cmp = (seq - CMP_LEN) // CMP_STRIDE + 1
    n_blocks = seq // SEL_LEN
    assert seq % FOX_TILE == 0 and seq % NSA_TILE == 0 and n_chunks % LANES == 0
    assert SEL_TOPK <= n_blocks <= SEL_SLOTS
    assert n_cmp == n_chunks - 1

    cos_n, sin_n = _rope_tables(np.arange(seq), N_KV)
    cos_c, sin_c = _rope_tables(np.arange(n_chunks) * CMP_STRIDE + CMP_LEN - 1, N_KV)
    cos_t = np.ascontiguousarray(cos_n.T)
    sin_t = np.ascontiguousarray(sin_n.T)

    tm = _row_tile(seq, 512)
    w_nat, w_tr = _prep_w_in(w_in)
    kf, kc, vc, ks, kw, qft, vft, qnt, vst, vwt, fgt = _inproj(
        h, g_attn[None, :], w_nat, w_tr, cos_n, sin_n, cos_t, sin_t, seq, tm)
    r3 = lambda a: a.reshape(b, seq, a.shape[-1])

    bf_pad = jnp.zeros((1, LANES), F32).at[0, :N_HEADS].set(b_f)
    bg_col = jnp.zeros((LANES,), F32).at[N_HEADS:4 * N_HEADS].set(b_gate)
    bg_t = jnp.broadcast_to(bg_col[:, None], (LANES, NSA_TILE))
    crow, kaug = _cumgate(fgt, bf_pad, r3(kf))

    kcmp, vcmpt = _compress(r3(kc), r3(vc), cos_c, sin_c, _prep_compress(*cmpk), _prep_compress(*cmpv))

    ofox = _fox(qft, kaug, vft, crow)
    onsa = _nsa(qnt, kcmp, vcmpt, r3(ks), vst, r3(kw), vwt, fgt, bg_t,
                _overlap_t(n_chunks, n_cmp, n_blocks), min(SEL_TOPK, n_blocks))

    w_out_p = jnp.concatenate([w_out[:WIDTH], _perm_heads(w_out[WIDTH:], 0)], axis=0).astype(BF16)
    return _post(h, ofox.reshape(m, WIDTH), onsa.reshape(m, WIDTH), g_fox[None, :], _perm_heads(g_nsa, 0)[None, :],
                 w_out_p, g_mlp[None, :], w_up.astype(BF16), w_down.astype(BF16), g_out[None, :],
                 _row_tile(seq, 512))


def kernel(x, g_attn, w_in, b_f, b_gate, cmpk_pe, cmpk_w1, cmpk_b1, cmpk_w2, cmpk_b2, cmpv_pe, cmpv_w1, cmpv_b1,
           cmpv_w2, cmpv_b2, g_fox, g_nsa, w_out, g_mlp, w_up, w_down, g_final):
    b, seq, d = x.shape
    depth = g_attn.shape[0]
    assert depth == 1, "the final rmsnorm is fused into the (single) layer's last kernel"
    h = x.reshape(b * seq, d)
    out = _layer(h, seq, g_attn[0], w_in[0], b_f[0], b_gate[0],
                 (cmpk_pe[0], cmpk_w1[0], cmpk_b1[0], cmpk_w2[0], cmpk_b2[0]),
                 (cmpv_pe[0], cmpv_w1[0], cmpv_b1[0], cmpv_w2[0], cmpv_b2[0]),
                 g_fox[0], g_nsa[0], w_out[0], g_mlp[0], w_up[0], w_down[0], g_final)
    return out.reshape(b, seq, d)
```
